```python
import math
import jax, jax.numpy as jnp
from jax import lax
import numpy as np

D_MODEL = 1024
BATCH = 16
SEQ = 2048
DEPTH = 2
DEC_BATCH = 32
DEC_SEQ = 8
PAST_LEN = 16384
PAGE_SIZE = 128

D_FF = 2752
DN_ALPHA = (2 * DEPTH) ** 0.25
DN_BETA = (8 * DEPTH) ** -0.25
LN_EPS = 1e-5
POOL_WIDTH = D_MODEL // 2
POOL_GROUPS = 4
POOL_GW = POOL_WIDTH // POOL_GROUPS
POOL_WINDOWS = (2, 4, 8, 16)
POOL_STATE = max(POOL_WINDOWS) - 1
GMLP_WIDTH = D_MODEL // 2
GMLP_HEADS = 4
GMLP_HD = GMLP_WIDTH // GMLP_HEADS
GMLP_CHUNK = 128
NSA_HEADS = 16
NSA_KV = 4
NSA_HPG = NSA_HEADS // NSA_KV
NSA_HD = 64
CMP_LEN = 32
CMP_STRIDE = 16
CMP_HIDDEN = 128
SEL_BLOCK = 64
N_SEL = 16
WINDOW = 512
SEL_QBLK = 16
WIN_QBLK = 128

kernel_name = "pool_gmlp_nsa_macaron_deepnorm_step"


def layer_norm(x, g, b):
    xf = x.astype(jnp.float32)
    mu = jnp.mean(xf, -1, keepdims=True)
    var = jnp.mean(jnp.square(xf - mu), -1, keepdims=True)
    return ((xf - mu) * lax.rsqrt(var + LN_EPS) * g + b).astype(x.dtype)


def swiglu(x, w_in, w_out):
    gate, up = jnp.split(x @ w_in, 2, axis=-1)
    return (jax.nn.silu(gate) * up) @ w_out


def alibi_slopes(n_heads):
    return 2.0 ** (-8.0 * (jnp.arange(n_heads, dtype=jnp.float32) + 1.0) / n_heads)


def masked_softmax(s, mask):
    s = jnp.where(mask, s.astype(jnp.float32), -1e30)
    m = jnp.max(s, -1, keepdims=True)
    e = jnp.where(mask, jnp.exp(s - m), 0.0)
    return e / jnp.maximum(jnp.sum(e, -1, keepdims=True), 1e-30)


def pool_gmlp_mixer(h, pos0, pool_past, w_in, pool_w, pool_scale, gm_ln_g, gm_ln_b, gm_ws, gm_b, w_out):
    B, T, _ = h.shape
    z = h @ w_in
    p = z[..., :POOL_WIDTH]
    u = z[..., POOL_WIDTH:POOL_WIDTH + GMLP_WIDTH]
    v = layer_norm(z[..., POOL_WIDTH + GMLP_WIDTH:], gm_ln_g, gm_ln_b)
    ext = jnp.concatenate([pool_past.astype(p.dtype), p], axis=1)
    ext32 = ext.astype(jnp.float32)
    csum = jnp.pad(jnp.cumsum(ext32, axis=1), ((0, 0), (1, 0), (0, 0)))
    hi = csum[:, POOL_STATE + 1:]
    tpos = pos0 + jnp.arange(T)
    means = []
    for g, w in enumerate(POOL_WINDOWS):
        ch = slice(g * POOL_GW, (g + 1) * POOL_GW)
        lo = csum[:, POOL_STATE + 1 - w:POOL_STATE + 1 - w + T, ch]
        cnt = jnp.minimum(w, tpos + 1).astype(jnp.float32)[None, :, None]
        means.append((hi[..., ch] - lo) / cnt)
    d = (jnp.concatenate(means, -1) - ext32[:, POOL_STATE:]).astype(h.dtype)
    a = jnp.einsum('btgc,gcd->btgd', d.reshape(B, T, POOL_GROUPS, POOL_GW), pool_w).reshape(B, T, POOL_WIDTH) * pool_scale
    C = min(T, GMLP_CHUNK)
    nc = T // C
    causal = jnp.tril(jnp.ones((C, C), dtype=bool))
    ws = jnp.where(causal, gm_ws[:, :C, :C], 0)
    mix = jnp.einsum('hts,bcshd->bcthd', ws, v.reshape(B, nc, C, GMLP_HEADS, GMLP_HD)) + gm_b[:, :C].T[None, None, :, :, None]
    gb = u * mix.reshape(B, T, GMLP_WIDTH)
    y = jnp.concatenate([a, gb], axis=-1) @ w_out
    return y, ext[:, -POOL_STATE:], v


def compress_blocks(x, pe, w1, b1, w2):
    B, L = x.shape[:2]
    rc = CMP_LEN // CMP_STRIDE
    nch = -(-L // CMP_STRIDE)
    xc = jnp.pad(x, ((0, 0), (0, nch * CMP_STRIDE - L), (0, 0), (0, 0))).reshape(B, nch, CMP_STRIDE, NSA_KV, NSA_HD)
    pe = pe.reshape(rc, CMP_STRIDE, NSA_HD)
    w1 = w1.reshape(rc, CMP_STRIDE, NSA_HD, CMP_HIDDEN)
    nb = nch - rc + 1
    hid = b1
    for r in range(rc):
        hid = hid + jnp.einsum('bcpgd,pdf->bcgf', xc[:, r:r + nb] + pe[r][:, None, :], w1[r])
    return jax.nn.gelu(hid) @ w2


def selection_importance(pc, nsb):
    rs = SEL_BLOCK // CMP_STRIDE
    rc = CMP_LEN // CMP_STRIDE
    nb = pc.shape[-1]
    pp = jnp.pad(pc, [(0, 0)] * (pc.ndim - 1) + [(rc - 1, rs * nsb - nb)])
    imp = None
    for n in range(rc):
        seg = pp[..., rc - 1 - n:rc - 1 - n + rs * nsb]
        part = seg.reshape(seg.shape[:-1] + (nsb, rs)).sum(-1)
        imp = part if imp is None else imp + part
    return imp


def nsa_mixer(h, pos0, cmp_past, slc_past, win_past, w_in, cmp_pe, cmp_w1, cmp_b1, cmp_w2, w_out):
    B, T, _ = h.shape
    qw = NSA_HEADS * NSA_HD
    kvw = 3 * 2 * NSA_KV * NSA_HD
    z = h @ w_in
    q = z[..., :qw].reshape(B, T, NSA_KV, NSA_HPG, NSA_HD)
    kv = z[..., qw:qw + kvw].reshape(B, T, 3, 2, NSA_KV, NSA_HD)
    gates = jax.nn.sigmoid(z[..., qw + kvw:].astype(jnp.float32)).reshape(B, T, 3, NSA_KV, NSA_HPG)
    kv_cmp, kv_slc, kv_win = kv[:, :, 0], kv[:, :, 1], kv[:, :, 2]
    tpos = pos0 + jnp.arange(T)
    slopes = alibi_slopes(NSA_HEADS).reshape(NSA_KV, NSA_HPG)
    scale = NSA_HD ** -0.5

    full_cmp = kv_cmp if cmp_past is None else jnp.concatenate([cmp_past, kv_cmp], axis=1)
    L = full_cmp.shape[1]
    kc = compress_blocks(full_cmp[:, :, 0], cmp_pe[0], cmp_w1[0], cmp_b1[0], cmp_w2[0])
    vc = compress_blocks(full_cmp[:, :, 1], cmp_pe[1], cmp_w1[1], cmp_b1[1], cmp_w2[1])
    nb = kc.shape[1]
    cend = jnp.arange(nb) * CMP_STRIDE + CMP_LEN - 1
    dist_c = tpos[:, None] - cend[None, :]
    s_c = jnp.einsum('btghd,bngd->bghtn', q, kc).astype(jnp.float32) * scale - slopes[:, :, None, None] * dist_c
    p_cmp = masked_softmax(s_c, dist_c >= 0)
    o_cmp = jnp.einsum('bghtn,bngd->btghd', p_cmp.astype(vc.dtype), vc)

    full_slc = kv_slc if slc_past is None else jnp.concatenate([slc_past, kv_slc], axis=1)
    nsb = -(-L // SEL_BLOCK)
    imp = selection_importance(jnp.sum(p_cmp, axis=2), nsb)
    jblk = jnp.arange(nsb)[None, :]
    jt = (tpos // SEL_BLOCK)[:, None]
    forced = (jblk == 0) | (jblk == jt) | (jblk == jt - 1)
    score = jnp.where(forced, jnp.inf, jnp.where(jblk <= jt, imp, -jnp.inf))
    n_sel = min(N_SEL, nsb)
    _, idx = lax.top_k(score, n_sel)
    idx = idx.transpose(0, 2, 1, 3)
    kvb = jnp.pad(full_slc, ((0, 0), (0, nsb * SEL_BLOCK - L), (0, 0), (0, 0), (0, 0)))
    kvb = kvb.reshape(B, nsb, SEL_BLOCK, 2, NSA_KV, NSA_HD).transpose(0, 4, 1, 2, 3, 5)
    bi = jnp.arange(B)[:, None, None, None]
    gi = jnp.arange(NSA_KV)[None, None, :, None]
    qb = SEL_QBLK if T % SEL_QBLK == 0 else T
    nq = T // qb

    def sel_block(args):
        q_b, i_b, t_b = args
        g = kvb[bi, gi, i_b].reshape(B, qb, NSA_KV, n_sel * SEL_BLOCK, 2, NSA_HD)
        kpos = (i_b[..., None] * SEL_BLOCK + jnp.arange(SEL_BLOCK)).reshape(B, qb, NSA_KV, n_sel * SEL_BLOCK)
        dd = t_b[None, :, None, None] - kpos
        s = jnp.einsum('bqghd,bqgkd->bqghk', q_b, g[..., 0, :]).astype(jnp.float32) * scale - slopes[None, None, :, :, None] * dd[:, :, :, None, :]
        pr = masked_softmax(s, (dd >= 0)[:, :, :, None, :])
        return jnp.einsum('bqghk,bqgkd->bqghd', pr.astype(g.dtype), g[..., 1, :])

    def to_blocks(a, blk, n):
        return a.reshape((B, n, blk) + a.shape[2:]).swapaxes(0, 1)

    o_slc = lax.map(sel_block, (to_blocks(q, qb, nq), to_blocks(idx, qb, nq), tpos.reshape(nq, qb)))
    o_slc = o_slc.swapaxes(0, 1).reshape(B, T, NSA_KV, NSA_HPG, NSA_HD)

    if win_past is None:
        ext = jnp.pad(kv_win, ((0, 0), (WINDOW, 0), (0, 0), (0, 0), (0, 0)))
        new_win = kv_win[:, -min(WINDOW, T):]
    else:
        wb = win_past.shape[1]
        buf = jnp.concatenate([win_past, kv_win], axis=1)
        new_win = buf[:, -wb:]
        ext = jnp.pad(buf, ((0, 0), (WINDOW - wb, 0), (0, 0), (0, 0), (0, 0)))
    wq = WIN_QBLK if T % WIN_QBLK == 0 else T
    nw = T // wq

    def win_block(args):
        q_b, c = args
        kvs = lax.dynamic_slice_in_dim(ext, c * wq, WINDOW + wq, axis=1)
        kpos = pos0 - WINDOW + c * wq + jnp.arange(WINDOW + wq)
        tq = pos0 + c * wq + jnp.arange(wq)
        dd = tq[:, None] - kpos[None, :]
        mask = (dd >= 0) & (dd < WINDOW) & (kpos[None, :] >= 0)
        s = jnp.einsum('bqghd,bkgd->bghqk', q_b, kvs[:, :, 0]).astype(jnp.float32) * scale - slopes[:, :, None, None] * dd
        pr = masked_softmax(s, mask)
        return jnp.einsum('bghqk,bkgd->bqghd', pr.astype(kvs.dtype), kvs[:, :, 1])

    o_win = lax.map(win_block, (to_blocks(q, wq, nw), jnp.arange(nw)))
    o_win = o_win.swapaxes(0, 1).reshape(B, T, NSA_KV, NSA_HPG, NSA_HD)

    o = (gates[:, :, 0, :, :, None] * o_cmp + gates[:, :, 1, :, :, None] * o_slc + gates[:, :, 2, :, :, None] * o_win)
    y = o.reshape(B, T, NSA_HEADS * NSA_HD).astype(h.dtype) @ w_out
    return y, kv_cmp, kv_slc, new_win


def trunk(x, pos0, pool_past, cmp_past, slc_past, win_past, w_ffn_in, w_ffn_out, ln_g, ln_b,
          w_in_even, pool_w, pool_scale, gm_ln_g, gm_ln_b, gm_ws, gm_b, w_out_even,
          w_in_odd, cmp_pe, cmp_w1, cmp_b1, cmp_w2, w_out_odd):
    for layer in range(DEPTH):
        x = layer_norm(DN_ALPHA * x + 0.5 * swiglu(x, w_ffn_in[layer, 0], w_ffn_out[layer, 0]), ln_g[layer, 0], ln_b[layer, 0])
        if layer % 2 == 0:
            y, pool_state, gm_v = pool_gmlp_mixer(x, pos0, pool_past, w_in_even, pool_w, pool_scale,
                                                  gm_ln_g, gm_ln_b, gm_ws, gm_b, w_out_even)
        else:
            y, cmp_rows, slc_rows, win_buf = nsa_mixer(x, pos0, cmp_past, slc_past, win_past, w_in_odd,
                                                       cmp_pe, cmp_w1, cmp_b1, cmp_w2, w_out_odd)
        x = layer_norm(DN_ALPHA * x + y, ln_g[layer, 1], ln_b[layer, 1])
        x = layer_norm(DN_ALPHA * x + 0.5 * swiglu(x, w_ffn_in[layer, 1], w_ffn_out[layer, 1]), ln_g[layer, 2], ln_b[layer, 2])
    return x, pool_state, gm_v, cmp_rows, slc_rows, win_buf


def setup_inputs(seed: int = 0) -> dict:
    key = jax.random.key(seed)
    ks = jax.random.split(key, 24)
    n_pages = PAST_LEN // PAGE_SIZE
    n_used = DEC_BATCH * n_pages
    n_phys = n_used + n_used // 4
    wb = min(WINDOW, PAST_LEN)

    def nrm(k, shape, s):
        return jax.random.normal(k, shape, jnp.float32) * s

    kv_scale = jnp.array([1.0, DN_BETA], jnp.float32)
    x_prompt = nrm(ks[0], (BATCH, SEQ, D_MODEL), 1.0)
    x_sample = nrm(ks[1], (DEC_BATCH, DEC_SEQ, D_MODEL), 1.0)
    state_l0_pool = nrm(ks[2], (DEC_BATCH, POOL_STATE, POOL_WIDTH), 1.0)
    cache_l1_cmp_kv = nrm(ks[3], (n_phys, PAGE_SIZE, 2, NSA_KV, NSA_HD), 1.0) * kv_scale[:, None, None]
    cache_l1_slc_kv = nrm(ks[4], (n_phys, PAGE_SIZE, 2, NSA_KV, NSA_HD), 1.0) * kv_scale[:, None, None]
    cache_l1_win_kv = nrm(ks[5], (DEC_BATCH, wb, 2, NSA_KV, NSA_HD), 1.0) * kv_scale[:, None, None]
    page_table = jax.random.permutation(ks[6], n_phys)[:n_used].reshape(DEC_BATCH, n_pages).astype(jnp.int32)
    w_ffn_in = nrm(ks[7], (DEPTH, 2, D_MODEL, 2 * D_FF), D_MODEL ** -0.5)
    w_ffn_out = nrm(ks[8], (DEPTH, 2, D_FF, D_MODEL), D_FF ** -0.5 * DN_BETA)
    ln_g = 1.0 + nrm(ks[9], (DEPTH, 3, D_MODEL), 0.02)
    ln_b = nrm(ks[10], (DEPTH, 3, D_MODEL), 0.02)
    w_in_even = nrm(ks[11], (D_MODEL, POOL_WIDTH + 2 * GMLP_WIDTH), D_MODEL ** -0.5)
    pool_w = nrm(ks[12], (POOL_GROUPS, POOL_GW, POOL_GW), POOL_GW ** -0.5)
    pool_scale = 1.0 + nrm(ks[13], (POOL_WIDTH,), 0.1)
    gm_ln_g = 1.0 + nrm(ks[14], (GMLP_WIDTH,), 0.02)
    gm_ln_b = nrm(ks[15], (GMLP_WIDTH,), 0.02)
    gm_ws = nrm(ks[16], (GMLP_HEADS, GMLP_CHUNK, GMLP_CHUNK), GMLP_CHUNK ** -0.5)
    gm_b = 1.0 + nrm(ks[17], (GMLP_HEADS, GMLP_CHUNK), 0.1)
    w_out_even = nrm(ks[18], (POOL_WIDTH + GMLP_WIDTH, D_MODEL), (POOL_WIDTH + GMLP_WIDTH) ** -0.5 * DN_BETA)
    kq, kkv, kg = jax.random.split(ks[19], 3)
    wq = nrm(kq, (D_MODEL, NSA_HEADS * NSA_HD), D_MODEL ** -0.5)
    wkv = nrm(kkv, (D_MODEL, 3, 2, NSA_KV * NSA_HD), D_MODEL ** -0.5) * kv_scale[:, None]
    wg = nrm(kg, (D_MODEL, 3 * NSA_HEADS), D_MODEL ** -0.5)
    w_in_odd = jnp.concatenate([wq, wkv.reshape(D_MODEL, -1), wg], axis=1)
    cmp_pe = nrm(ks[20], (2, CMP_LEN, NSA_HD), 0.1)
    k1, k2 = jax.random.split(ks[21])
    cmp_w1 = nrm(k1, (2, CMP_LEN * NSA_HD, CMP_HIDDEN), (CMP_LEN * NSA_HD) ** -0.5)
    cmp_b1 = nrm(k2, (2, CMP_HIDDEN), 0.02)
    cmp_w2 = nrm(ks[22], (2, CMP_HIDDEN, NSA_HD), CMP_HIDDEN ** -0.5)
    w_out_odd = nrm(ks[23], (NSA_HEADS * NSA_HD, D_MODEL), (NSA_HEADS * NSA_HD) ** -0.5 * DN_BETA)
    return {"x_prompt": x_prompt, "x_sample": x_sample, "state_l0_pool": state_l0_pool,
            "cache_l1_cmp_kv": cache_l1_cmp_kv, "cache_l1_slc_kv": cache_l1_slc_kv,
            "cache_l1_win_kv": cache_l1_win_kv, "page_table": page_table,
            "w_ffn_in": w_ffn_in, "w_ffn_out": w_ffn_out, "ln_g": ln_g, "ln_b": ln_b,
            "w_in_even": w_in_even, "pool_w": pool_w, "pool_scale": pool_scale,
            "gm_ln_g": gm_ln_g, "gm_ln_b": gm_ln_b, "gm_ws": gm_ws, "gm_b": gm_b, "w_out_even": w_out_even,
            "w_in_odd": w_in_odd, "cmp_pe": cmp_pe, "cmp_w1": cmp_w1, "cmp_b1": cmp_b1, "cmp_w2": cmp_w2,
            "w_out_odd": w_out_odd}


def reference(x_prompt, x_sample, state_l0_pool, cache_l1_cmp_kv, cache_l1_slc_kv, cache_l1_win_kv, page_table,
              w_ffn_in, w_ffn_out, ln_g, ln_b, w_in_even, pool_w, pool_scale, gm_ln_g, gm_ln_b, gm_ws, gm_b,
              w_out_even, w_in_odd, cmp_pe, cmp_w1, cmp_b1, cmp_w2, w_out_odd):
    pool0 = jnp.zeros((x_prompt.shape[0], POOL_STATE, POOL_WIDTH), x_prompt.dtype)
    y_prompt, pool_p, _, cmp_p, slc_p, win_p = trunk(
        x_prompt, 0, pool0, None, None, None, w_ffn_in, w_ffn_out, ln_g, ln_b,
        w_in_even, pool_w, pool_scale, gm_ln_g, gm_ln_b, gm_ws, gm_b, w_out_even,
        w_in_odd, cmp_pe, cmp_w1, cmp_b1, cmp_w2, w_out_odd)
    db, n_pages = page_table.shape
    cmp_past = cache_l1_cmp_kv[page_table].reshape((db, n_pages * PAGE_SIZE) + cache_l1_cmp_kv.shape[2:])
    slc_past = cache_l1_slc_kv[page_table].reshape((db, n_pages * PAGE_SIZE) + cache_l1_slc_kv.shape[2:])
    y_sample, pool_s, gmv_s, cmp_s, slc_s, win_s = trunk(
        x_sample, n_pages * PAGE_SIZE, state_l0_pool, cmp_past, slc_past, cache_l1_win_kv,
        w_ffn_in, w_ffn_out, ln_g, ln_b,
        w_in_even, pool_w, pool_scale, gm_ln_g, gm_ln_b, gm_ws, gm_b, w_out_even,
        w_in_odd, cmp_pe, cmp_w1, cmp_b1, cmp_w2, w_out_odd)
    return (y_prompt, y_sample, pool_p, pool_s, gmv_s, cmp_p, slc_p, win_p, cmp_s, slc_s, win_s)
```

```python
import functools

import numpy as np
import jax
import jax.numpy as jnp
from jax import lax
from jax.experimental import pallas as pl
from jax.experimental.pallas import tpu as pltpu

F32 = jnp.float32
BF16 = jnp.bfloat16

DEPTH = 2
DN_ALPHA = (2 * DEPTH) ** 0.25
LN_EPS = 1e-5
POOL_WINDOWS = (2, 4, 8, 16)
POOL_STATE = max(POOL_WINDOWS) - 1
POOL_HALO = 16
GMLP_CHUNK = 128
NSA_HEADS = 16
NSA_KV = 4
NSA_HPG = NSA_HEADS // NSA_KV
NSA_HD = 64
CMP_LEN = 32
CMP_STRIDE = 16
CMP_HIDDEN = 128
SEL_BLOCK = 64
N_SEL = 16
WINDOW = 512
PAGE = 128

LANE = 128
MASK_NEG = -1e30
SEL_BIG = 2.0 ** 100
VMEM_BIG = 56 << 20

_NT = (((1,), (1,)), ((), ()))


def _cparams(n_axes, vmem=None):
    return pltpu.CompilerParams(dimension_semantics=("arbitrary",) * n_axes, vmem_limit_bytes=vmem)


def _resident(shape):
    nd = len(shape)
    return pl.BlockSpec(shape, lambda *_: (0,) * nd, pipeline_mode=pl.Buffered(1))


def _whole(shape):
    nd = len(shape)
    return pl.BlockSpec(shape, lambda *_: (0,) * nd)


def _dot(a, b):
    return jnp.dot(a, b, preferred_element_type=F32)


def _dot_nt(a, b):
    return lax.dot_general(a, b, _NT, preferred_element_type=F32)


def _ln(x, g, b):
    mu = jnp.mean(x, -1, keepdims=True)
    xc = x - mu
    var = jnp.mean(xc * xc, -1, keepdims=True)
    return xc * lax.rsqrt(var + LN_EPS) * g + b


def _masked_softmax_rows(s, ok):
    s = jnp.where(ok, s, MASK_NEG)
    m = jnp.max(s, -1, keepdims=True)
    e = jnp.where(ok, jnp.exp(s - m), 0.0)
    return e / jnp.maximum(jnp.sum(e, -1, keepdims=True), 1e-30)


def _ffn_kernel(x_ref, wg_ref, wu_ref, wo_ref, g_ref, b_ref, o_ref):
    x = x_ref[...]
    xb = x.astype(BF16)
    gate = _dot(xb, wg_ref[...])
    up = _dot(xb, wu_ref[...])
    act = (gate * jax.nn.sigmoid(gate) * up).astype(BF16)
    y = _dot(act, wo_ref[...])
    o_ref[...] = _ln(DN_ALPHA * x + 0.5 * y, g_ref[...], b_ref[...])


def _ffn_ln(x, wg, wu, wo, g, b, tm):
    m, d = x.shape
    fp = wg.shape[1]
    return pl.pallas_call(
        _ffn_kernel,
        grid=(m // tm,),
        in_specs=[pl.BlockSpec((tm, d), lambda i: (i, 0)),
                  _resident((d, fp)), _resident((d, fp)), _resident((fp, d)),
                  _resident((1, d)), _resident((1, d))],
        out_specs=pl.BlockSpec((tm, d), lambda i: (i, 0)),
        out_shape=jax.ShapeDtypeStruct((m, d), F32),
        compiler_params=_cparams(1, VMEM_BIG),
        name="ffn_ln",
    )(x, wg, wu, wo, g, b)


def _outproj_kernel(x_ref, a_ref, w_ref, g_ref, b_ref, o_ref):
    y = _dot(a_ref[...].astype(BF16), w_ref[...])
    o_ref[...] = _ln(DN_ALPHA * x_ref[...] + y, g_ref[...], b_ref[...])


def _outproj_ln(x, a, w, g, b, tm):
    m, d = x.shape
    k = a.shape[1]
    return pl.pallas_call(
        _outproj_kernel,
        grid=(m // tm,),
        in_specs=[pl.BlockSpec((tm, d), lambda i: (i, 0)),
                  pl.BlockSpec((tm, k), lambda i: (i, 0)),
                  _resident((k, d)), _resident((1, d)), _resident((1, d))],
        out_specs=pl.BlockSpec((tm, d), lambda i: (i, 0)),
        out_shape=jax.ShapeDtypeStruct((m, d), F32),
        compiler_params=_cparams(1),
        name="outproj_ln",
    )(x, a, w, g, b)


def _pool_groups(read_window, p, cnt, poolw_ref, gw):
    outs = []
    for g, w in enumerate(POOL_WINDOWS):
        lanes = slice(g * gw, (g + 1) * gw)
        tot = p[..., lanes]
        for j in range(1, w):
            tot = tot + read_window(j, lanes)
        d = tot / jnp.minimum(float(w), cnt) - p[..., lanes]
        d2 = d.reshape(-1, gw).astype(BF16)
        outs.append(_dot(d2, poolw_ref[g]))
    return jnp.concatenate(outs, axis=-1)


def _mixer0_prompt_kernel(x_ref, win_ref, poolw_ref, pscale_ref, glng_ref, glnb_ref, ws_ref, gmb_ref,
                          wout_ref, lng_ref, lnb_ref, y_ref, pstate_ref, ext_ref, *, tm, pw, gw, hw):
    t = pl.program_id(1)
    x = x_ref[0]
    z = _dot(x.astype(BF16), win_ref[...])
    p = z[:, :pw]
    u = z[:, pw:pw + 4 * hw]
    v = _ln(z[:, pw + 4 * hw:], glng_ref[...], glnb_ref[...])

    @pl.when(t == 0)
    def _():
        ext_ref[0:POOL_HALO, :] = jnp.zeros((POOL_HALO, pw), F32)

    ext_ref[POOL_HALO:POOL_HALO + tm, :] = p
    cnt = (t * tm + lax.broadcasted_iota(jnp.int32, (tm, 1), 0) + 1).astype(F32)
    a = _pool_groups(lambda j, lanes: ext_ref[POOL_HALO - j:POOL_HALO - j + tm, lanes], p, cnt, poolw_ref, gw)
    a = a * pscale_ref[...]
    tail = ext_ref[tm:tm + POOL_HALO, :]
    pstate_ref[0] = tail
    ext_ref[0:POOL_HALO, :] = tail

    vb = v.astype(BF16)
    rows = []
    for c in range(tm // GMLP_CHUNK):
        r = slice(c * GMLP_CHUNK, (c + 1) * GMLP_CHUNK)
        rows.append(jnp.concatenate(
            [_dot(ws_ref[h], vb[r, h * hw:(h + 1) * hw]) for h in range(4)], axis=-1) + gmb_ref[...])
    gb = u * jnp.concatenate(rows, axis=0)
    y = _dot(a.astype(BF16), wout_ref[0:pw, :]) + _dot(gb.astype(BF16), wout_ref[pw:, :])
    y_ref[0] = _ln(DN_ALPHA * x + y, lng_ref[...], lnb_ref[...])


def _mixer0_prompt(x, w, lng, lnb, tm):
    b, t, d = x.shape
    pw = w["pool_scale"].shape[1]
    gw = pw // len(POOL_WINDOWS)
    hw = w["gm_ws"].shape[1]
    kern = functools.partial(_mixer0_prompt_kernel, tm=tm, pw=pw, gw=gw, hw=hw)
    y, pstate = pl.pallas_call(
        kern,
        grid=(b, t // tm),
        in_specs=[pl.BlockSpec((1, tm, d), lambda i, j: (i, j, 0)),
                  _resident(w["w_in_even"].shape), _resident(w["pool_w"].shape), _resident(w["pool_scale"].shape),
                  _resident(w["gm_ln_g"].shape), _resident(w["gm_ln_b"].shape), _resident(w["gm_ws"].shape),
                  _resident(w["gm_b_full"].shape), _resident(w["w_out_even"].shape),
                  _resident((1, d)), _resident((1, d))],
        out_specs=[pl.BlockSpec((1, tm, d), lambda i, j: (i, j, 0)),
                   pl.BlockSpec((1, POOL_HALO, pw), lambda i, j: (i, 0, 0))],
        out_shape=[jax.ShapeDtypeStruct((b, t, d), F32), jax.ShapeDtypeStruct((b, POOL_HALO, pw), F32)],
        scratch_shapes=[pltpu.VMEM((POOL_HALO + tm, pw), F32)],
        compiler_params=_cparams(2),
        name="mixer0_prompt",
    )(x, w["w_in_even"], w["pool_w"], w["pool_scale"], w["gm_ln_g"], w["gm_ln_b"], w["gm_ws"],
      w["gm_b_full"], w["w_out_even"], lng, lnb)
    return y, pstate[:, POOL_HALO - POOL_STATE:]


def _mixer0_sample_kernel(x_ref, past_ref, win_ref, poolw_ref, pscale_ref, glng_ref, glnb_ref, wk_ref, gmb_ref,
                          wout_ref, lng_ref, lnb_ref, y_ref, pstate_ref, gmv_ref, ext_ref,
                          *, nb, ts, pos0, pw, gw, hw):
    x = x_ref[...]
    z = _dot(x.astype(BF16), win_ref[...])
    p = z[:, :pw]
    u = z[:, pw:pw + 4 * hw]
    v = _ln(z[:, pw + 4 * hw:], glng_ref[...], glnb_ref[...])
    gmv_ref[...] = v

    p3 = p.reshape(nb, ts, pw)
    ext_ref[:, 0:POOL_HALO, :] = past_ref[...]
    ext_ref[:, POOL_HALO:POOL_HALO + ts, :] = p3
    cnt = (pos0 + lax.broadcasted_iota(jnp.int32, (1, ts, 1), 1) + 1).astype(F32)
    a = _pool_groups(lambda j, lanes: ext_ref[:, POOL_HALO - j:POOL_HALO - j + ts, lanes], p3, cnt, poolw_ref, gw)
    a = a * pscale_ref[...]
    pstate_ref[...] = ext_ref[:, ts:ts + POOL_HALO, :]

    vb = v.astype(BF16)
    mix = jnp.concatenate([_dot(wk_ref[h], vb[:, h * hw:(h + 1) * hw]) for h in range(4)], axis=-1) + gmb_ref[...]
    gb = u * mix
    y = _dot(a.astype(BF16), wout_ref[0:pw, :]) + _dot(gb.astype(BF16), wout_ref[pw:, :])
    y_ref[...] = _ln(DN_ALPHA * x + y, lng_ref[...], lnb_ref[...])


def _mixer0_sample(x, past, w, lng, lnb, nb, ts, pos0):
    m, d = x.shape
    pw = w["pool_scale"].shape[1]
    gw = pw // len(POOL_WINDOWS)
    hw = w["gm_ws"].shape[1]
    kern = functools.partial(_mixer0_sample_kernel, nb=nb, ts=ts, pos0=pos0, pw=pw, gw=gw, hw=hw)
    past16 = jnp.pad(past, ((0, 0), (POOL_HALO - POOL_STATE, 0), (0, 0)))
    args = (x, past16, w["w_in_even"], w["pool_w"], w["pool_scale"], w["gm_ln_g"], w["gm_ln_b"],
            w["gm_ws_kron"], w["gm_b_rows"], w["w_out_even"], lng, lnb)
    y, pstate, gmv = pl.pallas_call(
        kern,
        grid=(1,),
        in_specs=[_resident(a.shape) for a in args],
        out_specs=[_whole((m, d)), _whole((nb, POOL_HALO, pw)), _whole((m, 4 * hw))],
        out_shape=[jax.ShapeDtypeStruct((m, d), F32), jax.ShapeDtypeStruct((nb, POOL_HALO, pw), F32),
                   jax.ShapeDtypeStruct((m, 4 * hw), F32)],
        scratch_shapes=[pltpu.VMEM((nb, POOL_HALO + ts, pw), F32)],
        compiler_params=_cparams(1),
        name="mixer0_sample",
    )(*args)
    return y, pstate[:, POOL_HALO - POOL_STATE:], gmv


def _nsa_proj_kernel(x_ref, wq_ref, wkv_ref, wg_ref, q_ref, kc_ref, ks_ref, kw_ref, g_ref, *, kvw):
    xb = x_ref[...].astype(BF16)
    q_ref[...] = (_dot(xb, wq_ref[...]) * (NSA_HD ** -0.5)).astype(BF16)
    kv = _dot(xb, wkv_ref[...])
    kc_ref[...] = kv[:, 0:kvw]
    ks_ref[...] = kv[:, kvw:2 * kvw]
    kw_ref[...] = kv[:, 2 * kvw:3 * kvw]
    g_ref[...] = jax.nn.sigmoid(_dot(xb, wg_ref[...]))


def _nsa_proj(x, wq, wkv, wg, tm):
    m, d = x.shape
    qw = wq.shape[1]
    kvw = wkv.shape[1] // 3
    row = lambda n: pl.BlockSpec((tm, n), lambda i: (i, 0))
    return pl.pallas_call(
        functools.partial(_nsa_proj_kernel, kvw=kvw),
        grid=(m // tm,),
        in_specs=[row(d), _resident(wq.shape), _resident(wkv.shape), _resident(wg.shape)],
        out_specs=[row(qw), row(kvw), row(kvw), row(kvw), row(LANE)],
        out_shape=[jax.ShapeDtypeStruct((m, qw), BF16)] + [jax.ShapeDtypeStruct((m, kvw), F32)] * 3
        + [jax.ShapeDtypeStruct((m, LANE), F32)],
        compiler_params=_cparams(1),
        name="nsa_proj",
    )(x, wq, wkv, wg)


def _compress_chunks(xs_ref, n, kv, pe_ref, w1_ref, b1_ref, w2_ref):
    roww = 2 * NSA_KV * NSA_HD
    half = NSA_KV * NSA_HD

    def lhs(r):
        parts = [xs_ref[r:r + n, p * roww + kv * half:p * roww + (kv + 1) * half] for p in range(CMP_STRIDE)]
        return (jnp.concatenate(parts, axis=-1) + pe_ref[kv, r]).astype(BF16)

    hid = _dot(lhs(0), w1_ref[kv, 0]) + _dot(lhs(1), w1_ref[kv, 1]) + b1_ref[kv]
    return _dot(jax.nn.gelu(hid, approximate=True).astype(BF16), w2_ref[kv])


def _compress_prompt_kernel(x_ref, pe_ref, w1_ref, b1_ref, w2_ref, kc_ref, vc_ref, xs_ref, *, n):
    xs_ref[0:n, :] = x_ref[0]
    xs_ref[n:n + 8, :] = jnp.zeros((8, xs_ref.shape[1]), F32)
    valid = lax.broadcasted_iota(jnp.int32, (n, 1), 0) < n - 1
    for kv, o_ref in ((0, kc_ref), (1, vc_ref)):
        c = _compress_chunks(xs_ref, n, kv, pe_ref, w1_ref, b1_ref, w2_ref)
        o_ref[0] = jnp.where(valid, c, 0.0).astype(BF16)


def _compress_prompt(kv_rows, w):
    b, t, roww = kv_rows.shape
    n = t // CMP_STRIDE
    x = kv_rows.reshape(b, n, CMP_STRIDE * roww)
    half = roww // 2
    blk = pl.BlockSpec((1, n, half), lambda i: (i, 0, 0))
    return pl.pallas_call(
        functools.partial(_compress_prompt_kernel, n=n),
        grid=(b,),
        in_specs=[pl.BlockSpec((1, n, CMP_STRIDE * roww), lambda i: (i, 0, 0)),
                  _resident(w["cmp_pe_rows"].shape), _resident(w["cmp_w1_bd"].shape),
                  _resident(w["cmp_b1_rows"].shape), _resident(w["cmp_w2_bd"].shape)],
        out_specs=[blk, blk],
        out_shape=[jax.ShapeDtypeStruct((b, n, half), BF16)] * 2,
        scratch_shapes=[pltpu.VMEM((n + 8, CMP_STRIDE * roww), F32)],
        compiler_params=_cparams(1, VMEM_BIG),
        name="compress_prompt",
    )(x, w["cmp_pe_rows"], w["cmp_w1_bd"], w["cmp_b1_rows"], w["cmp_w2_bd"])


def _select_topk_cols(score, n_sel):
    nblk = score.shape[0]
    jj = lax.broadcasted_iota(jnp.int32, score.shape, 0)
    rank = jnp.zeros(score.shape, F32)
    for jp in range(nblk):
        row = score[jp:jp + 1, :]
        rank = rank + jnp.where(row > score, 1.0, jnp.where((row == score) & (jj > jp), 1.0, 0.0))
    return rank < float(n_sel)


def _cmp_prompt_kernel(slopes_ref, q_ref, kc_ref, vc_ref, gate_ref, smat_ref, o_ref, sel_ref, *, tq, nblk, nsb):
    t = pl.program_id(1)
    tpos = t * tq + lax.broadcasted_iota(jnp.int32, (tq, 1), 0)
    cend = lax.broadcasted_iota(jnp.int32, (1, nblk), 1) * CMP_STRIDE + (CMP_LEN - 1)
    dist_i = tpos - cend
    ok = dist_i >= 0
    dist = dist_i.astype(F32)
    tcol = t * tq + lax.broadcasted_iota(jnp.int32, (1, tq), 1)
    jt = tcol // SEL_BLOCK
    jblk = lax.broadcasted_iota(jnp.int32, (nsb, tq), 0)
    forced = (jblk == 0) | (jblk == jt) | (jblk == jt - 1)
    outs = []
    for g in range(NSA_KV):
        qg = q_ref[0, g * NSA_HPG:(g + 1) * NSA_HPG].reshape(NSA_HPG * tq, NSA_HD)
        s = _dot_nt(qg, kc_ref[0, g])
        pc = None
        for h in range(NSA_HPG):
            sh = s[h * tq:(h + 1) * tq] - slopes_ref[g * NSA_HPG + h] * dist
            p = _masked_softmax_rows(sh, ok)
            o = _dot(p.astype(BF16), vc_ref[0, g])
            outs.append(o * gate_ref[0, 0, g * NSA_HPG + h])
            pc = p if pc is None else pc + p
        imp = lax.dot_general(smat_ref[...], pc, _NT, precision=lax.Precision.HIGHEST,
                              preferred_element_type=F32)
        score = jnp.where(forced, jnp.inf, jnp.where(jblk <= jt, imp, -jnp.inf))
        sel_ref[0, g] = jnp.where(_select_topk_cols(score, min(N_SEL, nsb)), 0.0, -1.0)
    o_ref[0] = jnp.concatenate(outs, axis=-1)


def _cmp_prompt(slopes, q4, kc4, vc4, gates, smat, tq):
    b, nh, t, hd = q4.shape
    nblk = kc4.shape[2]
    nsb = smat.shape[0]
    kern = functools.partial(_cmp_prompt_kernel, tq=tq, nblk=nblk, nsb=nsb)
    return pl.pallas_call(
        kern,
        grid=(b, t // tq),
        in_specs=[pl.BlockSpec(memory_space=pltpu.SMEM),
                  pl.BlockSpec((1, nh, tq, hd), lambda i, j: (i, 0, j, 0)),
                  pl.BlockSpec((1, NSA_KV, nblk, hd), lambda i, j: (i, 0, 0, 0)),
                  pl.BlockSpec((1, NSA_KV, nblk, hd), lambda i, j: (i, 0, 0, 0)),
                  pl.BlockSpec((1, 1, nh, tq, 1), lambda i, j: (i, 0, 0, j, 0)),
                  _resident(smat.shape)],
        out_specs=[pl.BlockSpec((1, tq, nh * hd), lambda i, j: (i, j, 0)),
                   pl.BlockSpec((1, NSA_KV, nsb, tq), lambda i, j: (i, 0, 0, j))],
        out_shape=[jax.ShapeDtypeStruct((b, t, nh * hd), F32), jax.ShapeDtypeStruct((b, NSA_KV, nsb, t), F32)],
        compiler_params=_cparams(2),
        name="cmp_select_prompt",
    )(slopes, q4, kc4, vc4, gates, smat)


def _flash_branch(q, k_ref, v_ref, slopes, tpos, lo, hi, tk, use_sel, window):
    nh = len(slopes)
    tq = q.shape[0] // nh
    hd = v_ref.shape[-1]

    def body(kt, carry):
        off = pl.multiple_of(kt * tk, tk)
        k = k_ref[0, 0, pl.ds(off, tk), :]
        v = v_ref[0, 0, pl.ds(off, tk), :]
        s_all = _dot_nt(q, k)
        kpos = off + lax.broadcasted_iota(jnp.int32, (1, tk), 1)
        dist_i = tpos - kpos
        valid = dist_i >= 0
        if window is not None:
            valid = valid & (dist_i < window)
        dist = dist_i.astype(F32)
        new = []
        for h in range(nh):
            m, l, acc = carry[h]
            s = s_all[h * tq:(h + 1) * tq]
            ok = valid & (s > -0.5 * SEL_BIG) if use_sel else valid
            s = jnp.where(ok, s - slopes[h] * dist, MASK_NEG)
            m_new = jnp.maximum(m, jnp.max(s, -1, keepdims=True))
            alpha = jnp.exp(m - m_new)
            e = jnp.where(ok, jnp.exp(s - m_new), 0.0)
            l = alpha * l + jnp.sum(e, -1, keepdims=True)
            acc = alpha * acc + _dot(e.astype(BF16), v)
            new.append((m_new, l, acc))
        return tuple(new)

    init = tuple((jnp.full((tq, 1), MASK_NEG, F32), jnp.zeros((tq, 1), F32), jnp.zeros((tq, hd), F32))
                 for _ in range(nh))
    fin = lax.fori_loop(lo, hi, body, init)
    return [acc / jnp.maximum(l, 1e-30) for (_, l, acc) in fin]


def _attn_prompt_kernel(slopes_ref, qa_ref, ka_ref, vs_ref, kw_ref, vw_ref, gs_ref, gw_ref, oc_ref, o_ref,
                        *, tq, tk):
    g = pl.program_id(1)
    t = pl.program_id(2)
    tpos = t * tq + lax.broadcasted_iota(jnp.int32, (tq, 1), 0)
    slopes = [slopes_ref[g * NSA_HPG + h] for h in range(NSA_HPG)]
    q = qa_ref[0].reshape(NSA_HPG * tq, qa_ref.shape[-1])
    hi = (t * tq) // tk + tq // tk
    o_slc = _flash_branch(q, ka_ref, vs_ref, slopes, tpos, 0, hi, tk, True, None)
    lo = jnp.maximum((t * tq - WINDOW) // tk, 0)
    o_win = _flash_branch(q, kw_ref, vw_ref, slopes, tpos, lo, hi, tk, False, WINDOW)
    merged = jnp.concatenate([o_slc[h] * gs_ref[0, 0, h] for h in range(NSA_HPG)], axis=-1)
    merged = oc_ref[0] + merged
    merged = merged + jnp.concatenate([o_win[h] * gw_ref[0, 0, h] for h in range(NSA_HPG)], axis=-1)
    o_ref[0] = merged


def _attn_prompt(slopes, qa, ka, vs, kwa, vw, gates, o_cmp, tq, tk):
    b, nh, t, qd = qa.shape
    hd = vs.shape[-1]
    gw = NSA_HPG * hd
    kern = functools.partial(_attn_prompt_kernel, tq=tq, tk=tk)
    seq = lambda n: pl.BlockSpec((1, 1, t, n), lambda i, g, j: (i, g, 0, 0))
    gate = lambda br: pl.BlockSpec((1, 1, NSA_HPG, tq, 1), lambda i, g, j: (i, br, g, j, 0))
    return pl.pallas_call(
        kern,
        grid=(b, NSA_KV, t // tq),
        in_specs=[pl.BlockSpec(memory_space=pltpu.SMEM),
                  pl.BlockSpec((1, NSA_HPG, tq, qd), lambda i, g, j: (i, g, j, 0)),
                  seq(qd), seq(hd), seq(qd), seq(hd), gate(1), gate(2),
                  pl.BlockSpec((1, tq, gw), lambda i, g, j: (i, j, g))],
        out_specs=pl.BlockSpec((1, tq, gw), lambda i, g, j: (i, j, g)),
        out_shape=jax.ShapeDtypeStruct((b, t, nh * hd), F32),
        compiler_params=_cparams(3),
        name="attn_prompt",
    )(slopes, qa, ka, vs, kwa, vw, gates, gates, o_cmp)


def _diag_heads(o, rows_per_g):
    return jnp.concatenate(
        [o[g * rows_per_g:(g + 1) * rows_per_g, g * NSA_HD:(g + 1) * NSA_HD] for g in range(NSA_KV)], axis=0)


def _select_topk_rows(score, n_sel):
    jj = lax.broadcasted_iota(jnp.int32, score.shape, 1)
    big = jnp.int32(score.shape[1])
    taken = jnp.zeros(score.shape, jnp.bool_)
    for _ in range(n_sel):
        cur = jnp.where(taken, -jnp.inf, score)
        m = jnp.max(cur, -1, keepdims=True)
        cand = jnp.where(jnp.logical_not(taken) & (cur >= m), jj, big)
        pick = jnp.min(cand, -1, keepdims=True)
        taken = taken | (jj == pick)
    return taken


def _cmp_sample_kernel(pt_ref, *refs, npg, nsteps, ts, pos0, nblk, nsb, nsbp):
    pages = refs[:npg]
    (nxt_ref, new_ref, pe_ref, w1_ref, b1_ref, w2_ref, q_ref, slope_ref, gate_ref, smat_ref,
     o_ref, sel_ref, xs_ref, kc_ref, vc_ref) = refs[npg:]
    s = pl.program_id(1)
    cpp = PAGE // CMP_STRIDE
    n = npg * cpp
    for k in range(npg):
        xs_ref[k * cpp:(k + 1) * cpp, :] = pages[k][0]
    xs_ref[n:n + cpp, :] = jnp.where(s == nsteps - 1, new_ref[0], nxt_ref[0])
    off = pl.multiple_of(s * n, n)
    for kv, c_ref in ((0, kc_ref), (1, vc_ref)):
        c_ref[pl.ds(off, n), :] = _compress_chunks(xs_ref, n, kv, pe_ref, w1_ref, b1_ref, w2_ref).astype(BF16)

    @pl.when(s == nsteps - 1)
    def _():
        nrow = NSA_HEADS * ts
        trow = pos0 + lax.broadcasted_iota(jnp.int32, (nrow, 1), 0) % ts
        cend = lax.broadcasted_iota(jnp.int32, (1, nblk), 1) * CMP_STRIDE + (CMP_LEN - 1)
        dist_i = trow - cend
        sc = _dot_nt(q_ref[0], kc_ref[...]) - slope_ref[...] * dist_i.astype(F32)
        p = _masked_softmax_rows(sc, dist_i >= 0)
        o = _diag_heads(_dot(p.astype(BF16), vc_ref[...]), NSA_HPG * ts)
        o_ref[0] = o * gate_ref[0, 0]
        p4 = p.reshape(NSA_KV, NSA_HPG, ts, nblk)
        pc = p4[:, 0]
        for h in range(1, NSA_HPG):
            pc = pc + p4[:, h]
        imp = jnp.dot(pc.reshape(NSA_KV * ts, nblk), smat_ref[...], precision=lax.Precision.HIGHEST,
                      preferred_element_type=F32)
        tsel = pos0 + lax.broadcasted_iota(jnp.int32, (NSA_KV * ts, 1), 0) % ts
        jt = tsel // SEL_BLOCK
        jblk = lax.broadcasted_iota(jnp.int32, (NSA_KV * ts, nsbp), 1)
        forced = (jblk == 0) | (jblk == jt) | (jblk == jt - 1)
        score = jnp.where(forced, jnp.inf, jnp.where(jblk <= jt, imp, -jnp.inf))
        score = jnp.where(jblk < nsb, score, -jnp.inf)
        sel_ref[0] = jnp.where(_select_topk_rows(score, min(N_SEL, nsb)), 0.0, -1.0)


def _cmp_sample(page_table, cache, new_chunk, qbd, slope_col, gates, smat, w, npg, ts, pos0):
    nb, npages = page_table.shape
    nphys = cache.shape[0]
    roww = 2 * NSA_KV * NSA_HD
    cpp = PAGE // CMP_STRIDE
    cw = CMP_STRIDE * roww
    cache_c = cache.reshape(nphys, cpp, cw)
    nsteps = npages // npg
    nblk = npages * cpp
    nsbp = smat.shape[1]
    nsb = -(-(npages * PAGE + ts) // SEL_BLOCK)
    nrow = NSA_HEADS * ts
    half = NSA_KV * NSA_HD
    kern = functools.partial(_cmp_sample_kernel, npg=npg, nsteps=nsteps, ts=ts, pos0=pos0, nblk=nblk, nsb=nsb,
                             nsbp=nsbp)
    page_spec = lambda k: pl.BlockSpec((1, cpp, cw), lambda i, s, pt: (pt[i, s * npg + k], 0, 0))
    nxt_spec = pl.BlockSpec((1, cpp, cw), lambda i, s, pt: (pt[i, jnp.minimum((s + 1) * npg, npages - 1)], 0, 0))
    per_b = lambda shape: pl.BlockSpec((1,) + shape, lambda i, s, pt: (i,) + (0,) * len(shape))
    fixed = lambda a: pl.BlockSpec(a.shape, lambda i, s, pt: (0,) * a.ndim, pipeline_mode=pl.Buffered(1))
    grid_spec = pltpu.PrefetchScalarGridSpec(
        num_scalar_prefetch=1,
        grid=(nb, nsteps),
        in_specs=[page_spec(k) for k in range(npg)] + [
            nxt_spec, per_b((cpp, cw)),
            fixed(w["cmp_pe_rows"]), fixed(w["cmp_w1_bd"]), fixed(w["cmp_b1_rows"]), fixed(w["cmp_w2_bd"]),
            per_b((nrow, half)), fixed(slope_col),
            pl.BlockSpec((1, 1, nrow, 1), lambda i, s, pt: (0, i, 0, 0)), fixed(smat)],
        out_specs=[per_b((nrow, NSA_HD)), per_b((NSA_KV * ts, nsbp))],
        scratch_shapes=[pltpu.VMEM((npg * cpp + cpp, cw), F32), pltpu.VMEM((nblk, half), BF16),
                        pltpu.VMEM((nblk, half), BF16)],
    )
    return pl.pallas_call(
        kern,
        grid_spec=grid_spec,
        out_shape=[jax.ShapeDtypeStruct((nb, nrow, NSA_HD), F32), jax.ShapeDtypeStruct((nb, NSA_KV * ts, nsbp), F32)],
        compiler_params=_cparams(2, VMEM_BIG),
        name="cmp_select_sample",
    )(page_table, *([cache_c] * npg), cache_c, new_chunk, w["cmp_pe_rows"], w["cmp_w1_bd"], w["cmp_b1_rows"],
      w["cmp_w2_bd"], qbd, slope_col, gates, smat)


def _flash_rows(q, k, v, slope, dist_i, ok, m_ref, l_ref, acc_ref):
    s = _dot_nt(q, k) - slope * dist_i.astype(F32)
    s = jnp.where(ok, s, MASK_NEG)
    m_old = m_ref[...]
    m_new = jnp.maximum(m_old, jnp.max(s, -1, keepdims=True))
    alpha = jnp.exp(m_old - m_new)
    e = jnp.where(ok, jnp.exp(s - m_new), 0.0)
    l_ref[...] = alpha * l_ref[...] + jnp.sum(e, -1, keepdims=True)
    acc_ref[...] = alpha * acc_ref[...] + _dot(e.astype(BF16), v)
    m_ref[...] = m_new


def _slc_sample_kernel(pt_ref, *refs, npg, nsteps, ts, pos0):
    pages = refs[:npg]
    (new_ref, q_ref, slope_ref, gate_ref, sel_ref, selt_ref, prev_ref, o_ref, m_ref, l_ref, acc_ref) = refs[npg:]
    s = pl.program_id(1)
    half = NSA_KV * NSA_HD
    nrow = NSA_HEADS * ts
    bps = PAGE // SEL_BLOCK * npg
    trow = pos0 + lax.broadcasted_iota(jnp.int32, (nrow, 1), 0) % ts

    @pl.when(s == 0)
    def _():
        m_ref[...] = jnp.full(m_ref.shape, MASK_NEG, F32)
        l_ref[...] = jnp.zeros(l_ref.shape, F32)
        acc_ref[...] = jnp.zeros(acc_ref.shape, F32)

    rows = jnp.concatenate([pages[k][0] for k in range(npg)], axis=0)
    nk = npg * PAGE
    kpos = s * nk + lax.broadcasted_iota(jnp.int32, (1, nk), 1)
    expand = (lax.broadcasted_iota(jnp.int32, (bps, nk), 0)
              == lax.broadcasted_iota(jnp.int32, (bps, nk), 1) // SEL_BLOCK).astype(BF16)
    unsel = _dot(sel_ref[0, 0], expand)
    dist_i = trow - kpos
    ok = (dist_i >= 0) & (unsel > -0.5)
    _flash_rows(q_ref[0], rows[:, :half].astype(BF16), rows[:, half:].astype(BF16), slope_ref[...], dist_i, ok,
                m_ref, l_ref, acc_ref)

    @pl.when(s == nsteps - 1)
    def _():
        new = new_ref[0]
        kpos_n = nsteps * nk + lax.broadcasted_iota(jnp.int32, (1, PAGE), 1)
        dist_n = trow - kpos_n
        ok_n = (dist_n >= 0) & (selt_ref[0] > -0.5) & (kpos_n < nsteps * nk + SEL_BLOCK)
        _flash_rows(q_ref[0], new[:, :half].astype(BF16), new[:, half:].astype(BF16), slope_ref[...], dist_n, ok_n,
                    m_ref, l_ref, acc_ref)
        o = _diag_heads(acc_ref[...] / jnp.maximum(l_ref[...], 1e-30), NSA_HPG * ts)
        o_ref[0] = prev_ref[0] + o * gate_ref[0, 0]


def _slc_sample(page_table, cache, new_page, qbd, slope_col, gates, sel_steps, sel_tail, prev, npg, ts, pos0):
    nb, npages = page_table.shape
    roww = cache.shape[-1]
    nsteps = npages // npg
    nrow = NSA_HEADS * ts
    half = roww // 2
    bps = sel_steps.shape[-1]
    kern = functools.partial(_slc_sample_kernel, npg=npg, nsteps=nsteps, ts=ts, pos0=pos0)
    page_spec = lambda k: pl.BlockSpec((1, PAGE, roww), lambda i, s, pt: (pt[i, s * npg + k], 0, 0))
    per_b = lambda shape: pl.BlockSpec((1,) + shape, lambda i, s, pt: (i,) + (0,) * len(shape))
    grid_spec = pltpu.PrefetchScalarGridSpec(
        num_scalar_prefetch=1,
        grid=(nb, nsteps),
        in_specs=[page_spec(k) for k in range(npg)] + [
            per_b((PAGE, roww)), per_b((nrow, half)),
            pl.BlockSpec((nrow, 1), lambda i, s, pt: (0, 0)),
            pl.BlockSpec((1, 1, nrow, 1), lambda i, s, pt: (1, i, 0, 0)),
            pl.BlockSpec((1, 1, nrow, bps), lambda i, s, pt: (i, s, 0, 0)),
            per_b((nrow, 1)), per_b((nrow, NSA_HD))],
        out_specs=per_b((nrow, NSA_HD)),
        scratch_shapes=[pltpu.VMEM((nrow, 1), F32), pltpu.VMEM((nrow, 1), F32), pltpu.VMEM((nrow, half), F32)],
    )
    return pl.pallas_call(
        kern,
        grid_spec=grid_spec,
        out_shape=jax.ShapeDtypeStruct((nb, nrow, NSA_HD), F32),
        compiler_params=_cparams(2),
        name="slc_sample",
    )(page_table, *([cache] * npg), new_page, qbd, slope_col, gates, sel_steps, sel_tail, prev)


def _win_sample_kernel(past_ref, new_ref, q_ref, slope_ref, gate_ref, prev_ref, o_ref, *, ts, pos0, wb):
    half = NSA_KV * NSA_HD
    nrow = NSA_HEADS * ts
    trow = pos0 + lax.broadcasted_iota(jnp.int32, (nrow, 1), 0) % ts
    rows = jnp.concatenate([past_ref[0], new_ref[0]], axis=0)
    nk = wb + ts
    kpos = pos0 - wb + lax.broadcasted_iota(jnp.int32, (1, nk), 1)
    dist_i = trow - kpos
    ok = (dist_i >= 0) & (dist_i < WINDOW) & (kpos >= 0)
    s = _dot_nt(q_ref[0], rows[:, :half].astype(BF16)) - slope_ref[...] * dist_i.astype(F32)
    p = _masked_softmax_rows(s, ok)
    o = _diag_heads(_dot(p.astype(BF16), rows[:, half:].astype(BF16)), NSA_HPG * ts)
    o_ref[0] = prev_ref[0] + o * gate_ref[0, 0]


def _win_sample(win_past, new_rows, qbd, slope_col, gates, prev, ts, pos0):
    nb, wb, roww = win_past.shape
    nrow = NSA_HEADS * ts
    half = roww // 2
    per_b = lambda shape: pl.BlockSpec((1,) + shape, lambda i: (i,) + (0,) * len(shape))
    return pl.pallas_call(
        functools.partial(_win_sample_kernel, ts=ts, pos0=pos0, wb=wb),
        grid=(nb,),
        in_specs=[per_b((wb, roww)), per_b((ts, roww)), per_b((nrow, half)),
                  pl.BlockSpec((nrow, 1), lambda i: (0, 0)),
                  pl.BlockSpec((1, 1, nrow, 1), lambda i: (2, i, 0, 0)),
                  per_b((nrow, NSA_HD))],
        out_specs=per_b((nrow, NSA_HD)),
        out_shape=jax.ShapeDtypeStruct((nb, nrow, NSA_HD), F32),
        compiler_params=_cparams(1),
        name="win_sample",
    )(win_past, new_rows, qbd, slope_col, gates, prev)


def _importance_matrix(nblk, nsb):
    rs = SEL_BLOCK // CMP_STRIDE
    rc = CMP_LEN // CMP_STRIDE
    s = np.zeros((nsb, nblk), np.float32)
    for j in range(nsb):
        for n in range(rc):
            for r in range(rs):
                m = rs * j + r - n
                if 0 <= m < nblk:
                    s[j, m] += 1.0
    return s


def _prep_weights(w_ffn_in, w_ffn_out, ln_g, ln_b, w_in_even, pool_w, pool_scale, gm_ln_g, gm_ln_b, gm_ws, gm_b,
                  w_out_even, w_in_odd, cmp_pe, cmp_w1, cmp_b1, cmp_w2, w_out_odd, ts, nb_s):
    d = w_in_even.shape[0]
    dff = w_ffn_out.shape[2]
    fp = -(-dff // (2 * LANE)) * (2 * LANE)
    w = {}
    w["ffn_g"] = jnp.pad(w_ffn_in[..., :dff], ((0, 0),) * 3 + ((0, fp - dff),)).astype(BF16)
    w["ffn_u"] = jnp.pad(w_ffn_in[..., dff:], ((0, 0),) * 3 + ((0, fp - dff),)).astype(BF16)
    w["ffn_o"] = jnp.pad(w_ffn_out, ((0, 0), (0, 0), (0, fp - dff), (0, 0))).astype(BF16)
    w["ln_g"] = ln_g.reshape(ln_g.shape[0], ln_g.shape[1], 1, d)
    w["ln_b"] = ln_b.reshape(ln_b.shape[0], ln_b.shape[1], 1, d)
    hw = gm_ws.shape[1]
    w["w_in_even"] = w_in_even.astype(BF16)
    w["pool_w"] = pool_w.astype(BF16)
    w["pool_scale"] = pool_scale.reshape(1, -1)
    w["gm_ln_g"] = gm_ln_g.reshape(1, -1)
    w["gm_ln_b"] = gm_ln_b.reshape(1, -1)
    w["gm_ws"] = jnp.tril(gm_ws).astype(BF16)
    w["gm_b_full"] = jnp.repeat(gm_b.T, hw, axis=1)
    cs = min(ts, GMLP_CHUNK)
    ws_s = jnp.tril(gm_ws[:, :cs, :cs])
    w["gm_ws_kron"] = jnp.einsum("ab,hts->hatbs", jnp.eye(nb_s * (ts // cs), dtype=F32), ws_s).reshape(
        gm_ws.shape[0], nb_s * ts, nb_s * ts).astype(BF16)
    w["gm_b_rows"] = jnp.tile(jnp.repeat(gm_b[:, :cs].T, hw, axis=1), (nb_s * (ts // cs), 1))
    w["w_out_even"] = w_out_even.astype(BF16)
    qw = NSA_HEADS * NSA_HD
    kvw = 3 * 2 * NSA_KV * NSA_HD
    ng = w_in_odd.shape[1] - qw - kvw
    w["wq"] = w_in_odd[:, :qw].astype(BF16)
    w["wkv"] = w_in_odd[:, qw:qw + kvw].astype(BF16)
    w["wg"] = jnp.pad(w_in_odd[:, qw + kvw:], ((0, 0), (0, LANE - ng))).astype(BF16)
    w["w_out_odd"] = w_out_odd.astype(BF16)
    rc = CMP_LEN // CMP_STRIDE
    eye = jnp.eye(NSA_KV, dtype=F32)
    pe = cmp_pe.reshape(2, rc, CMP_STRIDE, 1, NSA_HD)
    w["cmp_pe_rows"] = jnp.broadcast_to(pe, (2, rc, CMP_STRIDE, NSA_KV, NSA_HD)).reshape(2, rc, 1, -1)
    w1 = cmp_w1.reshape(2, rc, CMP_STRIDE, NSA_HD, CMP_HIDDEN)
    w["cmp_w1_bd"] = jnp.einsum("krpdf,gh->krpgdhf", w1, eye).reshape(
        2, rc, CMP_STRIDE * NSA_KV * NSA_HD, NSA_KV * CMP_HIDDEN).astype(BF16)
    w["cmp_b1_rows"] = jnp.tile(cmp_b1, (1, NSA_KV)).reshape(2, 1, -1)
    w["cmp_w2_bd"] = jnp.einsum("kfd,gh->kgfhd", cmp_w2, eye).reshape(
        2, NSA_KV * CMP_HIDDEN, NSA_KV * NSA_HD).astype(BF16)
    w["slopes"] = 2.0 ** (-8.0 * (jnp.arange(NSA_HEADS, dtype=F32) + 1.0) / NSA_HEADS)
    return w


def _ffn(x, w, layer, which, tm):
    return _ffn_ln(x, w["ffn_g"][layer, which], w["ffn_u"][layer, which], w["ffn_o"][layer, which],
                   w["ln_g"][layer, which + (which > 0)], w["ln_b"][layer, which + (which > 0)], tm)


def _heads_major(a, b, t):
    return a.reshape(b, t, -1, NSA_HD).transpose(0, 2, 1, 3)


def _trunk_prompt(x3, w):
    b, t, d = x3.shape
    m = b * t
    tm = 256
    tq = 256
    x = _ffn(x3.reshape(m, d), w, 0, 0, tm)
    x, pool_state = _mixer0_prompt(x.reshape(b, t, d), w, w["ln_g"][0, 1], w["ln_b"][0, 1], tm)
    x = _ffn(x.reshape(m, d), w, 0, 1, tm)
    x = _ffn(x, w, 1, 0, tm)

    q, kvc, kvs, kvw, gates = _nsa_proj(x, w["wq"], w["wkv"], w["wg"], tm)
    half = NSA_KV * NSA_HD
    kc, vc = _compress_prompt(kvc.reshape(b, t, 2 * half), w)
    nblk = kc.shape[1]
    nsb = -(-t // SEL_BLOCK)
    kc4 = kc.reshape(b, nblk, NSA_KV, NSA_HD).transpose(0, 2, 1, 3)
    vc4 = vc.reshape(b, nblk, NSA_KV, NSA_HD).transpose(0, 2, 1, 3)
    q4 = _heads_major(q, b, t)
    gates5 = gates[:, :3 * NSA_HEADS].reshape(b, t, 3, NSA_HEADS).transpose(0, 2, 3, 1)[..., None]
    smat = jnp.asarray(_importance_matrix(nblk, nsb))
    o_cmp, sel_t = _cmp_prompt(w["slopes"], q4, kc4, vc4, gates5, smat, tq)

    selm = jnp.broadcast_to(sel_t.transpose(0, 1, 3, 2)[:, :, None], (b, NSA_KV, NSA_HPG, t, nsb))
    selm = selm.reshape(b, NSA_HEADS, t, nsb).astype(BF16)
    padq = jnp.zeros((b, NSA_HEADS, t, LANE - NSA_HD - nsb), BF16)
    qa = jnp.concatenate([q4, selm, padq], axis=-1)
    blk_of_key = (jnp.arange(t)[:, None] // SEL_BLOCK == jnp.arange(nsb)[None, :])
    epat = jnp.broadcast_to((blk_of_key * SEL_BIG).astype(BF16), (b, NSA_KV, t, nsb))
    padk = jnp.zeros((b, NSA_KV, t, LANE - NSA_HD - nsb), BF16)
    ks4 = _heads_major(kvs[:, :half], b, t).astype(BF16)
    vs4 = _heads_major(kvs[:, half:], b, t).astype(BF16)
    kw4 = _heads_major(kvw[:, :half], b, t).astype(BF16)
    vw4 = _heads_major(kvw[:, half:], b, t).astype(BF16)
    ka = jnp.concatenate([ks4, epat, padk], axis=-1)
    kwa = jnp.concatenate([kw4, jnp.zeros((b, NSA_KV, t, LANE - NSA_HD), BF16)], axis=-1)
    o = _attn_prompt(w["slopes"], qa, ka, vs4, kwa, vw4, gates5, o_cmp, tq, tq)

    x = _outproj_ln(x, o.reshape(m, -1), w["w_out_odd"], w["ln_g"][1, 1], w["ln_b"][1, 1], tm)
    x = _ffn(x, w, 1, 1, tm)
    kv5 = lambda a: a.reshape(b, t, 2, NSA_KV, NSA_HD)
    return x.reshape(b, t, d), pool_state, kv5(kvc), kv5(kvs), kv5(kvw)[:, -min(WINDOW, t):]


def _trunk_sample(x3, pool_past, cache_cmp, cache_slc, win_past, page_table, w):
    nb, ts, d = x3.shape
    m = nb * ts
    npages = page_table.shape[1]
    pos0 = npages * PAGE
    npg = 16
    x = _ffn(x3.reshape(m, d), w, 0, 0, m)
    x, pool_state, gm_v = _mixer0_sample(x, pool_past, w, w["ln_g"][0, 1], w["ln_b"][0, 1], nb, ts, pos0)
    x = _ffn(x, w, 0, 1, m)
    x = _ffn(x, w, 1, 0, m)

    q, kvc, kvs, kvw, gates = _nsa_proj(x, w["wq"], w["wkv"], w["wg"], m)
    half = NSA_KV * NSA_HD
    roww = 2 * half
    nrow = NSA_HEADS * ts
    qr = q.reshape(nb, ts, NSA_KV, NSA_HPG, NSA_HD).transpose(0, 2, 3, 1, 4).reshape(nb, NSA_KV, NSA_HPG * ts, NSA_HD)
    qbd = jnp.einsum("bgrd,gk->bgrkd", qr, jnp.eye(NSA_KV, dtype=BF16)).reshape(nb, nrow, half)
    slope_col = jnp.repeat(w["slopes"], ts).reshape(nrow, 1)
    gcols = gates[:, :3 * NSA_HEADS].reshape(nb, ts, 3, NSA_HEADS).transpose(2, 0, 3, 1).reshape(3, nb, nrow, 1)

    cpp = PAGE // CMP_STRIDE
    nblk = npages * cpp
    nsb = -(-(pos0 + ts) // SEL_BLOCK)
    nsbp = -(-nsb // LANE) * LANE
    smat = jnp.asarray(np.pad(_importance_matrix(nblk, nsb).T, ((0, 0), (0, nsbp - nsb))))
    new_chunk = jnp.pad(kvc.reshape(nb, 1, ts * roww), ((0, 0), (0, cpp - 1), (0, (CMP_STRIDE - ts) * roww)))
    o, sel = _cmp_sample(page_table, cache_cmp.reshape(cache_cmp.shape[0], PAGE, roww), new_chunk, qbd, slope_col,
                         gcols, smat, w, npg, ts, pos0)

    bpp = PAGE // SEL_BLOCK
    nsteps = npages // npg
    sel_rows = jnp.broadcast_to(sel.reshape(nb, NSA_KV, 1, ts, nsbp), (nb, NSA_KV, NSA_HPG, ts, nsbp))
    sel_rows = sel_rows.reshape(nb, nrow, nsbp)
    sel_steps = sel_rows[:, :, :npages * bpp].reshape(nb, nrow, nsteps, npg * bpp).transpose(0, 2, 1, 3).astype(BF16)
    sel_tail = sel_rows[:, :, npages * bpp:npages * bpp + 1]
    new_page = jnp.pad(kvs.reshape(nb, ts, roww), ((0, 0), (0, PAGE - ts), (0, 0)))
    o = _slc_sample(page_table, cache_slc.reshape(cache_slc.shape[0], PAGE, roww), new_page, qbd, slope_col, gcols,
                    sel_steps, sel_tail, o, npg, ts, pos0)
    wb = win_past.shape[1]
    win_rows = win_past.reshape(nb, wb, roww)
    new_win = kvw.reshape(nb, ts, roww)
    o = _win_sample(win_rows, new_win, qbd, slope_col, gcols, o, ts, pos0)

    o = o.reshape(nb, NSA_KV, NSA_HPG, ts, NSA_HD).transpose(0, 3, 1, 2, 4).reshape(m, NSA_HEADS * NSA_HD)
    x = _outproj_ln(x, o, w["w_out_odd"], w["ln_g"][1, 1], w["ln_b"][1, 1], m)
    x = _ffn(x, w, 1, 1, m)
    kv5 = lambda a: a.reshape(nb, ts, 2, NSA_KV, NSA_HD)
    win_buf = jnp.concatenate([win_rows, new_win], axis=1)[:, -wb:].reshape(nb, wb, 2, NSA_KV, NSA_HD)
    return x.reshape(nb, ts, d), pool_state, gm_v.reshape(nb, ts, -1), kv5(kvc), kv5(kvs), win_buf


def kernel(x_prompt, x_sample, state_l0_pool, cache_l1_cmp_kv, cache_l1_slc_kv, cache_l1_win_kv, page_table,
           w_ffn_in, w_ffn_out, ln_g, ln_b, w_in_even, pool_w, pool_scale, gm_ln_g, gm_ln_b, gm_ws, gm_b,
           w_out_even, w_in_odd, cmp_pe, cmp_w1, cmp_b1, cmp_w2, w_out_odd):
    w = _prep_weights(w_ffn_in, w_ffn_out, ln_g, ln_b, w_in_even, pool_w, pool_scale, gm_ln_g, gm_ln_b, gm_ws,
                      gm_b, w_out_even, w_in_odd, cmp_pe, cmp_w1, cmp_b1, cmp_w2, w_out_odd,
                      x_sample.shape[1], x_sample.shape[0])
    y_p, pool_p, cmp_p, slc_p, win_p = _trunk_prompt(x_prompt, w)
    y_s, pool_s, gmv_s, cmp_s, slc_s, win_s = _trunk_sample(
        x_sample, state_l0_pool, cache_l1_cmp_kv, cache_l1_slc_kv, cache_l1_win_kv, page_table, w)
    return (y_p, y_s, pool_p, pool_s, gmv_s, cmp_p, slc_p, win_p, cmp_s, slc_s, win_s)
```

```python
import functools

import numpy as np
import jax
import jax.numpy as jnp
from jax import lax
from jax.experimental import pallas as pl
from jax.experimental.pallas import tpu as pltpu

F32 = jnp.float32
BF16 = jnp.bfloat16

DEPTH = 2
DN_ALPHA = (2 * DEPTH) ** 0.25
LN_EPS = 1e-5
POOL_WINDOWS = (2, 4, 8, 16)
POOL_STATE = max(POOL_WINDOWS) - 1
POOL_HALO = 16
GMLP_CHUNK = 128
NSA_HEADS = 16
NSA_KV = 4
NSA_HPG = NSA_HEADS // NSA_KV
NSA_HD = 64
CMP_LEN = 32
CMP_STRIDE = 16
CMP_HIDDEN = 128
SEL_BLOCK = 64
N_SEL = 16
WINDOW = 512
PAGE = 128

LANE = 128
MASK_NEG = -1e30
SEL_BIG = 2.0 ** 100
KEY_POS_LANE = NSA_HD
KEY_SEL_LANE = NSA_HD + 16
VMEM_BIG = 56 << 20

_NT = (((1,), (1,)), ((), ()))


def _cparams(n_axes, vmem=None):
    return pltpu.CompilerParams(dimension_semantics=("arbitrary",) * n_axes, vmem_limit_bytes=vmem)


def _resident(shape):
    nd = len(shape)
    return pl.BlockSpec(shape, lambda *_: (0,) * nd, pipeline_mode=pl.Buffered(1))


def _whole(shape):
    nd = len(shape)
    return pl.BlockSpec(shape, lambda *_: (0,) * nd)


def _dot(a, b):
    return jnp.dot(a, b, preferred_element_type=F32)


def _dot_nt(a, b):
    return lax.dot_general(a, b, _NT, preferred_element_type=F32)


def _ln(x, g, b):
    mu = jnp.mean(x, -1, keepdims=True)
    xc = x - mu
    var = jnp.mean(xc * xc, -1, keepdims=True)
    return xc * lax.rsqrt(var + LN_EPS) * g + b


def _masked_softmax(s, ok, axis):
    s = jnp.where(ok, s, MASK_NEG)
    m = jnp.max(s, axis, keepdims=True)
    e = jnp.where(ok, jnp.exp(s - m), 0.0)
    return e / jnp.maximum(jnp.sum(e, axis, keepdims=True), 1e-30)


def _ffn_kernel(x_ref, wg_ref, wu_ref, wo_ref, g_ref, b_ref, o_ref):
    x = x_ref[...]
    xb = x.astype(BF16)
    gate = _dot(xb, wg_ref[...])
    up = _dot(xb, wu_ref[...])
    act = (gate * jax.nn.sigmoid(gate) * up).astype(BF16)
    y = _dot(act, wo_ref[...])
    o_ref[...] = _ln(DN_ALPHA * x + 0.5 * y, g_ref[...], b_ref[...])


def _ffn_ln(x, wg, wu, wo, g, b, tm):
    m, d = x.shape
    fp = wg.shape[1]
    return pl.pallas_call(
        _ffn_kernel,
        grid=(m // tm,),
        in_specs=[pl.BlockSpec((tm, d), lambda i: (i, 0)),
                  _resident((d, fp)), _resident((d, fp)), _resident((fp, d)),
                  _resident((1, d)), _resident((1, d))],
        out_specs=pl.BlockSpec((tm, d), lambda i: (i, 0)),
        out_shape=jax.ShapeDtypeStruct((m, d), F32),
        compiler_params=_cparams(1, VMEM_BIG),
        name="ffn_ln",
    )(x, wg, wu, wo, g, b)


def _outproj_kernel(x_ref, a_ref, w_ref, g_ref, b_ref, o_ref):
    y = _dot(a_ref[...].astype(BF16), w_ref[...])
    o_ref[...] = _ln(DN_ALPHA * x_ref[...] + y, g_ref[...], b_ref[...])


def _outproj_ln(x, a, w, g, b, tm):
    m, d = x.shape
    k = a.shape[1]
    return pl.pallas_call(
        _outproj_kernel,
        grid=(m // tm,),
        in_specs=[pl.BlockSpec((tm, d), lambda i: (i, 0)),
                  pl.BlockSpec((tm, k), lambda i: (i, 0)),
                  _resident((k, d)), _resident((1, d)), _resident((1, d))],
        out_specs=pl.BlockSpec((tm, d), lambda i: (i, 0)),
        out_shape=jax.ShapeDtypeStruct((m, d), F32),
        compiler_params=_cparams(1),
        name="outproj_ln",
    )(x, a, w, g, b)


def _pool_groups(read_window, p, cnt, poolw_ref, gw):
    outs = []
    for g, w in enumerate(POOL_WINDOWS):
        lanes = slice(g * gw, (g + 1) * gw)
        tot = p[..., lanes]
        for j in range(1, w):
            tot = tot + read_window(j, lanes)
        d = tot / jnp.minimum(float(w), cnt) - p[..., lanes]
        d2 = d.reshape(-1, gw).astype(BF16)
        outs.append(_dot(d2, poolw_ref[g]))
    return jnp.concatenate(outs, axis=-1)


def _mixer0_prompt_kernel(x_ref, win_ref, poolw_ref, pscale_ref, glng_ref, glnb_ref, ws_ref, gmb_ref,
                          wout_ref, lng_ref, lnb_ref, y_ref, pstate_ref, ext_ref, *, tm, pw, gw, hw):
    t = pl.program_id(1)
    x = x_ref[0]
    z = _dot(x.astype(BF16), win_ref[...])
    p = z[:, :pw]
    u = z[:, pw:pw + 4 * hw]
    v = _ln(z[:, pw + 4 * hw:], glng_ref[...], glnb_ref[...])

    @pl.when(t == 0)
    def _():
        ext_ref[0:POOL_HALO, :] = jnp.zeros((POOL_HALO, pw), F32)

    ext_ref[POOL_HALO:POOL_HALO + tm, :] = p
    cnt = (t * tm + lax.broadcasted_iota(jnp.int32, (tm, 1), 0) + 1).astype(F32)
    a = _pool_groups(lambda j, lanes: ext_ref[POOL_HALO - j:POOL_HALO - j + tm, lanes], p, cnt, poolw_ref, gw)
    a = a * pscale_ref[...]
    tail = ext_ref[tm:tm + POOL_HALO, :]
    pstate_ref[0] = tail
    ext_ref[0:POOL_HALO, :] = tail

    vb = v.astype(BF16)
    rows = []
    for c in range(tm // GMLP_CHUNK):
        r = slice(c * GMLP_CHUNK, (c + 1) * GMLP_CHUNK)
        rows.append(jnp.concatenate(
            [_dot(ws_ref[h], vb[r, h * hw:(h + 1) * hw]) for h in range(4)], axis=-1) + gmb_ref[...])
    gb = u * jnp.concatenate(rows, axis=0)
    y = _dot(a.astype(BF16), wout_ref[0:pw, :]) + _dot(gb.astype(BF16), wout_ref[pw:, :])
    y_ref[0] = _ln(DN_ALPHA * x + y, lng_ref[...], lnb_ref[...])


def _mixer0_prompt(x, w, lng, lnb, tm):
    b, t, d = x.shape
    pw = w["pool_scale"].shape[1]
    gw = pw // len(POOL_WINDOWS)
    hw = w["gm_ws"].shape[1]
    kern = functools.partial(_mixer0_prompt_kernel, tm=tm, pw=pw, gw=gw, hw=hw)
    y, pstate = pl.pallas_call(
        kern,
        grid=(b, t // tm),
        in_specs=[pl.BlockSpec((1, tm, d), lambda i, j: (i, j, 0)),
                  _resident(w["w_in_even"].shape), _resident(w["pool_w"].shape), _resident(w["pool_scale"].shape),
                  _resident(w["gm_ln_g"].shape), _resident(w["gm_ln_b"].shape), _resident(w["gm_ws"].shape),
                  _resident(w["gm_b_full"].shape), _resident(w["w_out_even"].shape),
                  _resident((1, d)), _resident((1, d))],
        out_specs=[pl.BlockSpec((1, tm, d), lambda i, j: (i, j, 0)),
                   pl.BlockSpec((1, POOL_HALO, pw), lambda i, j: (i, 0, 0))],
        out_shape=[jax.ShapeDtypeStruct((b, t, d), F32), jax.ShapeDtypeStruct((b, POOL_HALO, pw), F32)],
        scratch_shapes=[pltpu.VMEM((POOL_HALO + tm, pw), F32)],
        compiler_params=_cparams(2),
        name="mixer0_prompt",
    )(x, w["w_in_even"], w["pool_w"], w["pool_scale"], w["gm_ln_g"], w["gm_ln_b"], w["gm_ws"],
      w["gm_b_full"], w["w_out_even"], lng, lnb)
    return y, pstate[:, POOL_HALO - POOL_STATE:]


def _mixer0_sample_kernel(x_ref, past_ref, win_ref, poolw_ref, pscale_ref, glng_ref, glnb_ref, wk_ref, gmb_ref,
                          wout_ref, lng_ref, lnb_ref, y_ref, pstate_ref, gmv_ref, ext_ref,
                          *, nb, ts, pos0, pw, gw, hw):
    x = x_ref[...]
    z = _dot(x.astype(BF16), win_ref[...])
    p = z[:, :pw]
    u = z[:, pw:pw + 4 * hw]
    v = _ln(z[:, pw + 4 * hw:], glng_ref[...], glnb_ref[...])
    gmv_ref[...] = v

    p3 = p.reshape(nb, ts, pw)
    ext_ref[:, 0:POOL_HALO, :] = past_ref[...]
    ext_ref[:, POOL_HALO:POOL_HALO + ts, :] = p3
    cnt = (pos0 + lax.broadcasted_iota(jnp.int32, (1, ts, 1), 1) + 1).astype(F32)
    a = _pool_groups(lambda j, lanes: ext_ref[:, POOL_HALO - j:POOL_HALO - j + ts, lanes], p3, cnt, poolw_ref, gw)
    a = a * pscale_ref[...]
    pstate_ref[...] = ext_ref[:, ts:ts + POOL_HALO, :]

    vb = v.astype(BF16)
    mix = jnp.concatenate([_dot(wk_ref[h], vb[:, h * hw:(h + 1) * hw]) for h in range(4)], axis=-1) + gmb_ref[...]
    gb = u * mix
    y = _dot(a.astype(BF16), wout_ref[0:pw, :]) + _dot(gb.astype(BF16), wout_ref[pw:, :])
    y_ref[...] = _ln(DN_ALPHA * x + y, lng_ref[...], lnb_ref[...])


def _mixer0_sample(x, past, w, lng, lnb, nb, ts, pos0):
    m, d = x.shape
    pw = w["pool_scale"].shape[1]
    gw = pw // len(POOL_WINDOWS)
    hw = w["gm_ws"].shape[1]
    kern = functools.partial(_mixer0_sample_kernel, nb=nb, ts=ts, pos0=pos0, pw=pw, gw=gw, hw=hw)
    past16 = jnp.pad(past, ((0, 0), (POOL_HALO - POOL_STATE, 0), (0, 0)))
    args = (x, past16, w["w_in_even"], w["pool_w"], w["pool_scale"], w["gm_ln_g"], w["gm_ln_b"],
            w["gm_ws_kron"], w["gm_b_rows"], w["w_out_even"], lng, lnb)
    y, pstate, gmv = pl.pallas_call(
        kern,
        grid=(1,),
        in_specs=[_resident(a.shape) for a in args],
        out_specs=[_whole((m, d)), _whole((nb, POOL_HALO, pw)), _whole((m, 4 * hw))],
        out_shape=[jax.ShapeDtypeStruct((m, d), F32), jax.ShapeDtypeStruct((nb, POOL_HALO, pw), F32),
                   jax.ShapeDtypeStruct((m, 4 * hw), F32)],
        scratch_shapes=[pltpu.VMEM((nb, POOL_HALO + ts, pw), F32)],
        compiler_params=_cparams(1),
        name="mixer0_sample",
    )(*args)
    return y, pstate[:, POOL_HALO - POOL_STATE:], gmv


def _nsa_proj_kernel(x_ref, wq_ref, wkv_ref, wg_ref, q_ref, kc_ref, ks_ref, kw_ref, g_ref, *, kvw):
    xb = x_ref[...].astype(BF16)
    q_ref[...] = (_dot(xb, wq_ref[...]) * (NSA_HD ** -0.5)).astype(BF16)
    kv = _dot(xb, wkv_ref[...])
    kc_ref[...] = kv[:, 0:kvw]
    ks_ref[...] = kv[:, kvw:2 * kvw]
    kw_ref[...] = kv[:, 2 * kvw:3 * kvw]
    g_ref[...] = jax.nn.sigmoid(_dot(xb, wg_ref[...]))


def _nsa_proj(x, wq, wkv, wg, tm):
    m, d = x.shape
    qw = wq.shape[1]
    kvw = wkv.shape[1] // 3
    row = lambda n: pl.BlockSpec((tm, n), lambda i: (i, 0))
    return pl.pallas_call(
        functools.partial(_nsa_proj_kernel, kvw=kvw),
        grid=(m // tm,),
        in_specs=[row(d), _resident(wq.shape), _resident(wkv.shape), _resident(wg.shape)],
        out_specs=[row(qw), row(kvw), row(kvw), row(kvw), row(LANE)],
        out_shape=[jax.ShapeDtypeStruct((m, qw), BF16)] + [jax.ShapeDtypeStruct((m, kvw), F32)] * 3
        + [jax.ShapeDtypeStruct((m, LANE), F32)],
        compiler_params=_cparams(1),
        name="nsa_proj",
    )(x, wq, wkv, wg)


def _nsa_proj_prompt_kernel(x_ref, wqt_ref, wgt_ref, wkvt_ref, wkc_ref, wkp_ref,
                            qt_ref, gt_ref, ct_ref, st_ref, wt_ref, kvc_ref, kp_ref, vt_ref, *, kvw):
    xb = x_ref[0].astype(BF16)
    qt_ref[0] = (_dot_nt(wqt_ref[...], xb) * (NSA_HD ** -0.5)).astype(BF16)
    gt_ref[0] = jax.nn.sigmoid(_dot_nt(wgt_ref[...], xb))
    kvt = _dot_nt(wkvt_ref[...], xb)
    ct_ref[0] = kvt[0:kvw]
    st_ref[0] = kvt[kvw:2 * kvw]
    wt_ref[0] = kvt[2 * kvw:3 * kvw]
    half = kvw // 2
    vt_ref[0, 0] = jnp.concatenate([kvt[kvw + half:2 * kvw], kvt[2 * kvw + half:3 * kvw]], axis=0).astype(BF16)
    kvc_ref[0] = _dot(xb, wkc_ref[...])
    kp = _dot(xb, wkp_ref[...])
    tm = kp.shape[0]
    sub = lax.broadcasted_iota(jnp.int32, kp.shape, 1) % LANE
    grp = lax.broadcasted_iota(jnp.int32, kp.shape, 1) // LANE
    kabs = pl.program_id(1) * tm + lax.broadcasted_iota(jnp.int32, kp.shape, 0)
    blk = kabs // SEL_BLOCK
    pat = jnp.where((sub >= KEY_POS_LANE) & (sub < KEY_POS_LANE + 3), blk,
                    jnp.where((sub >= KEY_POS_LANE + 3) & (sub < KEY_POS_LANE + 6), kabs % SEL_BLOCK, 0))
    pat = jnp.where((grp < NSA_KV) & (sub - KEY_SEL_LANE == blk), 1, pat)
    kp_ref[0] = (kp + pat.astype(F32)).astype(BF16)


def _nsa_proj_prompt(x, w, tm):
    b, t, d = x.shape
    qw = w["wq_t"].shape[0]
    kvw = w["wkv_t"].shape[0] // 3
    kpw = w["wk_pad"].shape[1]
    feat = lambda n: pl.BlockSpec((1, n, tm), lambda i, j: (i, 0, j))
    tok = lambda n: pl.BlockSpec((1, tm, n), lambda i, j: (i, j, 0))
    return pl.pallas_call(
        functools.partial(_nsa_proj_prompt_kernel, kvw=kvw),
        grid=(b, t // tm),
        in_specs=[tok(d), _resident(w["wq_t"].shape), _resident(w["wg_t"].shape), _resident(w["wkv_t"].shape),
                  _resident(w["wkv_cmp"].shape), _resident(w["wk_pad"].shape)],
        out_specs=[feat(qw), feat(LANE), feat(kvw), feat(kvw), feat(kvw), tok(kvw), tok(kpw),
                   pl.BlockSpec((1, 1, kvw, tm), lambda i, j: (i, j, 0, 0))],
        out_shape=[jax.ShapeDtypeStruct((b, qw, t), BF16), jax.ShapeDtypeStruct((b, LANE, t), F32)]
        + [jax.ShapeDtypeStruct((b, kvw, t), F32)] * 3
        + [jax.ShapeDtypeStruct((b, t, kvw), F32), jax.ShapeDtypeStruct((b, t, kpw), BF16),
           jax.ShapeDtypeStruct((b, t // tm, kvw, tm), BF16)],
        compiler_params=_cparams(2),
        name="nsa_proj_prompt",
    )(x, w["wq_t"], w["wg_t"], w["wkv_t"], w["wkv_cmp"], w["wk_pad"])


def _compress_hidden(read, kv, pe_ref, w1_ref, b1_ref):
    def lhs(r):
        parts = [read(kv, r, p) for p in range(CMP_STRIDE)]
        return (jnp.concatenate(parts, axis=-1) + pe_ref[kv, r]).astype(BF16)

    hid = _dot(lhs(0), w1_ref[kv, 0]) + _dot(lhs(1), w1_ref[kv, 1]) + b1_ref[kv]
    return jax.nn.gelu(hid, approximate=True).astype(BF16)


def _compress_prompt_kernel(x_ref, pe_ref, w1_ref, b1_ref, w2k_ref, w2v_ref, kc_ref, vct_ref, xs_ref, *, n):
    roww = 2 * NSA_KV * NSA_HD
    half = NSA_KV * NSA_HD
    xs_ref[0:n, :] = x_ref[0]
    xs_ref[n:n + 8, :] = jnp.zeros((8, xs_ref.shape[1]), F32)
    valid = lax.broadcasted_iota(jnp.int32, (n, 1), 0) < n - 1

    def read(kv, r, p):
        return xs_ref[r:r + n, p * roww + kv * half:p * roww + (kv + 1) * half]

    kc = _dot(_compress_hidden(read, 0, pe_ref, w1_ref, b1_ref), w2k_ref[...])
    kc_ref[0] = jnp.where(valid, kc, 0.0).astype(BF16)
    vc = _dot(_compress_hidden(read, 1, pe_ref, w1_ref, b1_ref), w2v_ref[...])
    vct_ref[0] = jnp.where(valid, vc, 0.0).T.astype(BF16)


def _compress_prompt(kv_rows, w):
    b, t, roww = kv_rows.shape
    n = t // CMP_STRIDE
    x = kv_rows.reshape(b, n, CMP_STRIDE * roww)
    half = roww // 2
    return pl.pallas_call(
        functools.partial(_compress_prompt_kernel, n=n),
        grid=(b,),
        in_specs=[pl.BlockSpec((1, n, CMP_STRIDE * roww), lambda i: (i, 0, 0)),
                  _resident(w["cmp_pe_rows"].shape), _resident(w["cmp_w1_bd"].shape),
                  _resident(w["cmp_b1_rows"].shape), _resident(w["cmp_w2k_pad"].shape),
                  _resident(w["cmp_w2_bd"].shape[1:])],
        out_specs=[pl.BlockSpec((1, n, NSA_KV * LANE), lambda i: (i, 0, 0)),
                   pl.BlockSpec((1, half, n), lambda i: (i, 0, 0))],
        out_shape=[jax.ShapeDtypeStruct((b, n, NSA_KV * LANE), BF16), jax.ShapeDtypeStruct((b, half, n), BF16)],
        scratch_shapes=[pltpu.VMEM((n + 8, CMP_STRIDE * roww), F32)],
        compiler_params=_cparams(1, VMEM_BIG),
        name="compress_prompt",
    )(x, w["cmp_pe_rows"], w["cmp_w1_bd"], w["cmp_b1_rows"], w["cmp_w2k_pad"], w["cmp_w2_bd"][1])


def _select_topk_cols(score, n_sel):
    nblk = score.shape[0]
    jj = lax.broadcasted_iota(jnp.int32, score.shape, 0)
    rank = jnp.zeros(score.shape, F32)
    for jp in range(nblk):
        row = score[jp:jp + 1, :]
        rank = rank + jnp.where(row > score, 1.0, jnp.where((row == score) & (jj > jp), 1.0, 0.0))
    return rank < float(n_sel)


def _nsa_prompt_kernel(slopes_ref, qt_ref, gt_ref, srow_ref, kc_ref, vct_ref, ks_ref, vst_ref, kw_ref, vwt_ref,
                       smat_ref, o_ref, m_ref, l_ref, acc_ref, osum_ref, *, tq, tk, nblk, nsb):
    g = pl.program_id(1)
    t = pl.program_id(2)
    nh = NSA_HPG
    hd = NSA_HD
    slopes = [slopes_ref[g * nh + h] for h in range(nh)]
    gate = lambda br, h: gt_ref[0, pl.ds(br * NSA_HEADS + g * nh + h, 1), :]
    qt = qt_ref[0]
    tcol = t * tq + lax.broadcasted_iota(jnp.int32, (1, tq), 1)

    def stack_queries(extra_rows):
        return jnp.concatenate(
            [jnp.concatenate([qt[h * hd:(h + 1) * hd], extra_rows(h)], axis=0) for h in range(nh)], axis=1)

    sc = _dot(kc_ref[0], stack_queries(lambda h: jnp.zeros((LANE - hd, tq), BF16)))
    cend = lax.broadcasted_iota(jnp.int32, (nblk, 1), 0) * CMP_STRIDE + (CMP_LEN - 1)
    dist_i = tcol - cend
    ok = dist_i >= 0
    dist = dist_i.astype(F32)
    pc = None
    for h in range(nh):
        p = _masked_softmax(sc[:, h * tq:(h + 1) * tq] - slopes[h] * dist, ok, 0)
        osum_ref[h * hd:(h + 1) * hd, :] = _dot(vct_ref[0], p.astype(BF16)) * gate(0, h)
        pc = p if pc is None else pc + p
    imp = jnp.dot(smat_ref[...], pc, precision=lax.Precision.HIGHEST, preferred_element_type=F32)
    jt = tcol // SEL_BLOCK
    jblk = lax.broadcasted_iota(jnp.int32, (nsb, tq), 0)
    forced = (jblk == 0) | (jblk == jt) | (jblk == jt - 1)
    score = jnp.where(forced, jnp.inf, jnp.where(jblk <= jt, imp, -jnp.inf))
    unsel = jnp.where(_select_topk_cols(score, min(N_SEL, nsb)), 0.0, -SEL_BIG).astype(BF16)
    pad_rows = jnp.zeros((LANE - KEY_SEL_LANE - nsb, tq), BF16)
    q4 = stack_queries(lambda h: jnp.concatenate(
        [jnp.broadcast_to(srow_ref[h], (KEY_SEL_LANE - hd, tq)).astype(BF16), unsel, pad_rows], axis=0))

    def reset():
        m_ref[...] = jnp.full(m_ref.shape, MASK_NEG, F32)
        l_ref[...] = jnp.zeros(l_ref.shape, F32)
        acc_ref[...] = jnp.zeros(acc_ref.shape, F32)

    def tile(k_ref, vt_ref, kt, mask):
        off = pl.multiple_of(kt * tk, tk)
        s = _dot(k_ref[0, pl.ds(off, tk), :], q4)
        if mask is not None:
            d_i = (t * tq - kt * tk + lax.broadcasted_iota(jnp.int32, (1, tq), 1)
                   - lax.broadcasted_iota(jnp.int32, (tk, 1), 0))
            okm = d_i >= 0 if mask == "causal" else d_i < WINDOW
            s = jnp.where(jnp.concatenate([okm] * nh, axis=1), s, MASK_NEG)
        m_old = m_ref[...]
        m_new = jnp.maximum(m_old, jnp.max(s, 0, keepdims=True))
        alpha = jnp.exp(m_old - m_new)
        e = jnp.exp(s - m_new)
        l_ref[...] = alpha * l_ref[...] + jnp.sum(e, 0, keepdims=True)
        m_ref[...] = m_new
        eb = e.astype(BF16)
        vt = vt_ref[0, kt]
        for h in range(nh):
            rows = slice(h * hd, (h + 1) * hd)
            cols = slice(h * tq, (h + 1) * tq)
            acc_ref[rows, :] = alpha[:, cols] * acc_ref[rows, :] + _dot(vt, eb[:, cols])

    def finish(br):
        for h in range(nh):
            rows = slice(h * hd, (h + 1) * hd)
            o = acc_ref[rows, :] / jnp.maximum(l_ref[:, h * tq:(h + 1) * tq], 1e-30)
            osum_ref[rows, :] = osum_ref[rows, :] + o * gate(br, h)

    kd = (t * tq) // tk
    reset()
    tile(ks_ref, vst_ref, kd, "causal")

    def slc_body(kt, c):
        tile(ks_ref, vst_ref, kt, None)
        return c

    lax.fori_loop(0, kd, slc_body, 0)
    finish(1)
    reset()
    tile(kw_ref, vwt_ref, kd, "causal")
    nwin = WINDOW // tk
    for j in range(1, nwin + 1):
        @pl.when(kd >= j)
        def _(j=j):
            tile(kw_ref, vwt_ref, kd - j, "window" if j == nwin else None)
    finish(2)
    o_ref[0] = osum_ref[...].T


def _nsa_prompt(slopes, slope_rows, qt, gt, kc, vct, kp, vt, smat, tq):
    b, qw, t = qt.shape
    nblk = kc.shape[1]
    nsb = smat.shape[0]
    tk = vt.shape[-1]
    ntile = vt.shape[1]
    gw = NSA_HPG * NSA_HD
    assert tk % tq == 0 and WINDOW % tk == 0
    assert KEY_SEL_LANE + nsb <= LANE and nsb <= 256
    kern = functools.partial(_nsa_prompt_kernel, tq=tq, tk=tk, nblk=nblk, nsb=nsb)
    keys = lambda br: pl.BlockSpec((1, t, LANE), lambda i, g, j: (i, 0, br * NSA_KV + g))
    vals = lambda br: pl.BlockSpec((1, ntile, NSA_HD, tk), lambda i, g, j: (i, 0, br * NSA_KV + g, 0))
    return pl.pallas_call(
        kern,
        grid=(b, NSA_KV, t // tq),
        in_specs=[pl.BlockSpec(memory_space=pltpu.SMEM),
                  pl.BlockSpec((1, gw, tq), lambda i, g, j: (i, g, j)),
                  pl.BlockSpec((1, LANE, tq), lambda i, g, j: (i, 0, j)),
                  pl.BlockSpec((NSA_HPG,) + slope_rows.shape[1:], lambda i, g, j: (g, 0, 0)),
                  pl.BlockSpec((1, nblk, LANE), lambda i, g, j: (i, 0, g)),
                  pl.BlockSpec((1, NSA_HD, nblk), lambda i, g, j: (i, g, 0)),
                  keys(0), vals(0), keys(1), vals(1),
                  pl.BlockSpec(smat.shape, lambda i, g, j: (0, 0))],
        out_specs=pl.BlockSpec((1, tq, gw), lambda i, g, j: (i, j, g)),
        out_shape=jax.ShapeDtypeStruct((b, t, qw), F32),
        scratch_shapes=[pltpu.VMEM((1, NSA_HPG * tq), F32), pltpu.VMEM((1, NSA_HPG * tq), F32),
                        pltpu.VMEM((gw, tq), F32), pltpu.VMEM((gw, tq), F32)],
        compiler_params=_cparams(3),
        name="nsa_prompt",
    )(slopes, qt, gt, slope_rows, kc, vct, kp, vt, kp, vt, smat)


def _diag_heads(o, rows_per_g):
    return jnp.concatenate(
        [o[g * rows_per_g:(g + 1) * rows_per_g, g * NSA_HD:(g + 1) * NSA_HD] for g in range(NSA_KV)], axis=0)


def _select_topk_rows(score, n_sel):
    jj = lax.broadcasted_iota(jnp.int32, score.shape, 1)
    big = jnp.int32(score.shape[1])
    taken = jnp.zeros(score.shape, jnp.bool_)
    for _ in range(n_sel):
        cur = jnp.where(taken, -jnp.inf, score)
        m = jnp.max(cur, -1, keepdims=True)
        cand = jnp.where(jnp.logical_not(taken) & (cur >= m), jj, big)
        pick = jnp.min(cand, -1, keepdims=True)
        taken = taken | (jj == pick)
    return taken


def _page_rows(page_ref, kv):
    return page_ref[0, kv].reshape(NSA_KV * NSA_HD, PAGE)


def _cmp_sample_kernel(pt_ref, *refs, npg, nsteps, ts, pos0, nblk, nsb, nsbp):
    pages = refs[:npg]
    (nxt_ref, new_ref, pe_ref, w1_ref, b1_ref, w2_ref, q_ref, slope_ref, gate_ref, smat_ref,
     o_ref, sel_ref, xs_ref, kc_ref, vc_ref) = refs[npg:]
    s = pl.program_id(1)
    half = NSA_KV * NSA_HD
    n = npg * PAGE // CMP_STRIDE
    last = s == nsteps - 1
    nlh = half // LANE
    for kv in range(2):
        def put(rows, val):
            for c in range(nlh):
                xs_ref[kv, c, rows, :] = val[:, c * LANE:(c + 1) * LANE]

        for k in range(npg):
            put(slice(k * PAGE, (k + 1) * PAGE), _page_rows(pages[k], kv).T)
        nxt = _page_rows(nxt_ref, kv).T[0:CMP_STRIDE]
        new = new_ref[0][:, kv * half:(kv + 1) * half]
        put(slice(npg * PAGE, npg * PAGE + CMP_STRIDE), jnp.where(last, new, nxt))

    def read(kv, r, p):
        rows = pl.ds(r * CMP_STRIDE + p, n, stride=CMP_STRIDE)
        return jnp.concatenate([xs_ref[kv, c, rows, :] for c in range(nlh)], axis=-1)

    off = pl.multiple_of(s * n, n)
    for kv, c_ref in ((0, kc_ref), (1, vc_ref)):
        c = _dot(_compress_hidden(read, kv, pe_ref, w1_ref, b1_ref), w2_ref[kv])
        c_ref[pl.ds(off, n), :] = c.astype(BF16)

    @pl.when(last)
    def _():
        nrow = NSA_HEADS * ts
        trow = pos0 + lax.broadcasted_iota(jnp.int32, (nrow, 1), 0) % ts
        cend = lax.broadcasted_iota(jnp.int32, (1, nblk), 1) * CMP_STRIDE + (CMP_LEN - 1)
        dist_i = trow - cend
        sc = _dot_nt(q_ref[0], kc_ref[...]) - slope_ref[...] * dist_i.astype(F32)
        p = _masked_softmax(sc, dist_i >= 0, -1)
        o = _diag_heads(_dot(p.astype(BF16), vc_ref[...]), NSA_HPG * ts)
        o_ref[0] = o * gate_ref[0, 0]
        p4 = p.reshape(NSA_KV, NSA_HPG, ts, nblk)
        pc = p4[:, 0]
        for h in range(1, NSA_HPG):
            pc = pc + p4[:, h]
        imp = jnp.dot(pc.reshape(NSA_KV * ts, nblk), smat_ref[...], precision=lax.Precision.HIGHEST,
                      preferred_element_type=F32)
        tsel = pos0 + lax.broadcasted_iota(jnp.int32, (NSA_KV * ts, 1), 0) % ts
        jt = tsel // SEL_BLOCK
        jblk = lax.broadcasted_iota(jnp.int32, (NSA_KV * ts, nsbp), 1)
        forced = (jblk == 0) | (jblk == jt) | (jblk == jt - 1)
        score = jnp.where(forced, jnp.inf, jnp.where(jblk <= jt, imp, -jnp.inf))
        score = jnp.where(jblk < nsb, score, -jnp.inf)
        sel_ref[0] = jnp.where(_select_topk_rows(score, min(N_SEL, nsb)), 0.0, -1.0)


def _cmp_sample(page_table, cache_t, new_chunk, qbd, slope_col, gates, smat, w, npg, ts, pos0):
    nb, npages = page_table.shape
    half = NSA_KV * NSA_HD
    nsteps = npages // npg
    nblk = npages * PAGE // CMP_STRIDE
    nsbp = smat.shape[1]
    nsb = -(-(npages * PAGE + ts) // SEL_BLOCK)
    nrow = NSA_HEADS * ts
    kern = functools.partial(_cmp_sample_kernel, npg=npg, nsteps=nsteps, ts=ts, pos0=pos0, nblk=nblk, nsb=nsb,
                             nsbp=nsbp)
    pshape = (1,) + cache_t.shape[1:]
    page_spec = lambda k: pl.BlockSpec(pshape, lambda i, s, pt: (pt[i, s * npg + k], 0, 0, 0, 0))
    nxt_spec = pl.BlockSpec(pshape, lambda i, s, pt: (pt[i, jnp.minimum((s + 1) * npg, npages - 1)], 0, 0, 0, 0))
    per_b = lambda shape: pl.BlockSpec((1,) + shape, lambda i, s, pt: (i,) + (0,) * len(shape))
    fixed = lambda a: pl.BlockSpec(a.shape, lambda i, s, pt: (0,) * a.ndim, pipeline_mode=pl.Buffered(1))
    grid_spec = pltpu.PrefetchScalarGridSpec(
        num_scalar_prefetch=1,
        grid=(nb, nsteps),
        in_specs=[page_spec(k) for k in range(npg)] + [
            nxt_spec, per_b((CMP_STRIDE, 2 * half)),
            fixed(w["cmp_pe_rows"]), fixed(w["cmp_w1_bd"]), fixed(w["cmp_b1_rows"]), fixed(w["cmp_w2_bd"]),
            per_b((nrow, half)), fixed(slope_col),
            pl.BlockSpec((1, 1, nrow, 1), lambda i, s, pt: (0, i, 0, 0)), fixed(smat)],
        out_specs=[per_b((nrow, NSA_HD)), per_b((NSA_KV * ts, nsbp))],
        scratch_shapes=[pltpu.VMEM((2, half // LANE, npg * PAGE + CMP_STRIDE, LANE), F32),
                        pltpu.VMEM((nblk, half), BF16),
                        pltpu.VMEM((nblk, half), BF16)],
    )
    return pl.pallas_call(
        kern,
        grid_spec=grid_spec,
        out_shape=[jax.ShapeDtypeStruct((nb, nrow, NSA_HD), F32), jax.ShapeDtypeStruct((nb, NSA_KV * ts, nsbp), F32)],
        compiler_params=_cparams(2, VMEM_BIG),
        name="cmp_select_sample",
    )(page_table, *([cache_t] * npg), cache_t, new_chunk, w["cmp_pe_rows"], w["cmp_w1_bd"], w["cmp_b1_rows"],
      w["cmp_w2_bd"], qbd, slope_col, gates, smat)


def _flash_rows(s, ok, pv, m_ref, l_ref, acc_ref):
    s = jnp.where(ok, s, MASK_NEG)
    m_old = m_ref[...]
    m_new = jnp.maximum(m_old, jnp.max(s, -1, keepdims=True))
    alpha = jnp.exp(m_old - m_new)
    e = jnp.where(ok, jnp.exp(s - m_new), 0.0)
    l_ref[...] = alpha * l_ref[...] + jnp.sum(e, -1, keepdims=True)
    acc_ref[...] = alpha * acc_ref[...] + pv(e.astype(BF16))
    m_ref[...] = m_new


def _slc_sample_kernel(pt_ref, *refs, npg, nsteps, ts, pos0):
    pages = refs[:npg]
    (new_ref, q_ref, slope_ref, gate_ref, sel_ref, selt_ref, prev_ref, o_ref, m_ref, l_ref, acc_ref) = refs[npg:]
    s = pl.program_id(1)
    half = NSA_KV * NSA_HD
    nrow = NSA_HEADS * ts
    bps = PAGE // SEL_BLOCK * npg
    trow = pos0 + lax.broadcasted_iota(jnp.int32, (nrow, 1), 0) % ts

    @pl.when(s == 0)
    def _():
        m_ref[...] = jnp.full(m_ref.shape, MASK_NEG, F32)
        l_ref[...] = jnp.zeros(l_ref.shape, F32)
        acc_ref[...] = jnp.zeros(acc_ref.shape, F32)

    kt = jnp.concatenate([_page_rows(pages[k], 0) for k in range(npg)], axis=1).astype(BF16)
    vt = jnp.concatenate([_page_rows(pages[k], 1) for k in range(npg)], axis=1).astype(BF16)
    nk = npg * PAGE
    kpos = s * nk + lax.broadcasted_iota(jnp.int32, (1, nk), 1)
    expand = (lax.broadcasted_iota(jnp.int32, (bps, nk), 0)
              == lax.broadcasted_iota(jnp.int32, (bps, nk), 1) // SEL_BLOCK).astype(BF16)
    unsel = _dot(sel_ref[0, 0], expand)
    dist_i = trow - kpos
    ok = (dist_i >= 0) & (unsel > -0.5)
    sc = _dot(q_ref[0], kt) - slope_ref[...] * dist_i.astype(F32)
    _flash_rows(sc, ok, lambda e: _dot_nt(e, vt), m_ref, l_ref, acc_ref)

    @pl.when(s == nsteps - 1)
    def _():
        new = new_ref[0]
        kpos_n = nsteps * nk + lax.broadcasted_iota(jnp.int32, (1, PAGE), 1)
        dist_n = trow - kpos_n
        ok_n = (dist_n >= 0) & (selt_ref[0] > -0.5) & (kpos_n < nsteps * nk + SEL_BLOCK)
        sc_n = _dot_nt(q_ref[0], new[:, :half].astype(BF16)) - slope_ref[...] * dist_n.astype(F32)
        _flash_rows(sc_n, ok_n, lambda e: _dot(e, new[:, half:].astype(BF16)), m_ref, l_ref, acc_ref)
        o = _diag_heads(acc_ref[...] / jnp.maximum(l_ref[...], 1e-30), NSA_HPG * ts)
        o_ref[0] = prev_ref[0] + o * gate_ref[0, 0]


def _slc_sample(page_table, cache_t, new_page, qbd, slope_col, gates, sel_steps, sel_tail, prev, npg, ts, pos0):
    nb, npages = page_table.shape
    half = NSA_KV * NSA_HD
    nsteps = npages // npg
    nrow = NSA_HEADS * ts
    bps = sel_steps.shape[-1]
    kern = functools.partial(_slc_sample_kernel, npg=npg, nsteps=nsteps, ts=ts, pos0=pos0)
    pshape = (1,) + cache_t.shape[1:]
    page_spec = lambda k: pl.BlockSpec(pshape, lambda i, s, pt: (pt[i, s * npg + k], 0, 0, 0, 0))
    per_b = lambda shape: pl.BlockSpec((1,) + shape, lambda i, s, pt: (i,) + (0,) * len(shape))
    grid_spec = pltpu.PrefetchScalarGridSpec(
        num_scalar_prefetch=1,
        grid=(nb, nsteps),
        in_specs=[page_spec(k) for k in range(npg)] + [
            per_b((PAGE, 2 * half)), per_b((nrow, half)),
            pl.BlockSpec((nrow, 1), lambda i, s, pt: (0, 0)),
            pl.BlockSpec((1, 1, nrow, 1), lambda i, s, pt: (1, i, 0, 0)),
            pl.BlockSpec((1, 1, nrow, bps), lambda i, s, pt: (i, s, 0, 0)),
            per_b((nrow, 1)), per_b((nrow, NSA_HD))],
        out_specs=per_b((nrow, NSA_HD)),
        scratch_shapes=[pltpu.VMEM((nrow, 1), F32), pltpu.VMEM((nrow, 1), F32), pltpu.VMEM((nrow, half), F32)],
    )
    return pl.pallas_call(
        kern,
        grid_spec=grid_spec,
        out_shape=jax.ShapeDtypeStruct((nb, nrow, NSA_HD), F32),
        compiler_params=_cparams(2),
        name="slc_sample",
    )(page_table, *([cache_t] * npg), new_page, qbd, slope_col, gates, sel_steps, sel_tail, prev)


def _win_sample_kernel(past_ref, new_ref, q_ref, slope_ref, gate_ref, prev_ref, o_ref, *, ts, pos0, wb):
    half = NSA_KV * NSA_HD
    nrow = NSA_HEADS * ts
    trow = pos0 + lax.broadcasted_iota(jnp.int32, (nrow, 1), 0) % ts
    q = q_ref[0]
    new = new_ref[0]
    kt = past_ref[0, 0].reshape(half, wb).astype(BF16)
    vt = past_ref[0, 1].reshape(half, wb).astype(BF16)
    dist_p = trow - (pos0 - wb + lax.broadcasted_iota(jnp.int32, (1, wb), 1))
    dist_n = trow - (pos0 + lax.broadcasted_iota(jnp.int32, (1, ts), 1))
    ok_p = (dist_p >= 0) & (dist_p < WINDOW) & (trow - dist_p >= 0)
    ok_n = (dist_n >= 0) & (dist_n < WINDOW)
    s_p = jnp.where(ok_p, _dot(q, kt) - slope_ref[...] * dist_p.astype(F32), MASK_NEG)
    s_n = jnp.where(ok_n, _dot_nt(q, new[:, :half].astype(BF16)) - slope_ref[...] * dist_n.astype(F32), MASK_NEG)
    m = jnp.maximum(jnp.max(s_p, -1, keepdims=True), jnp.max(s_n, -1, keepdims=True))
    e_p = jnp.where(ok_p, jnp.exp(s_p - m), 0.0)
    e_n = jnp.where(ok_n, jnp.exp(s_n - m), 0.0)
    den = jnp.maximum(jnp.sum(e_p, -1, keepdims=True) + jnp.sum(e_n, -1, keepdims=True), 1e-30)
    o = _dot_nt((e_p / den).astype(BF16), vt) + _dot((e_n / den).astype(BF16), new[:, half:].astype(BF16))
    o_ref[0] = prev_ref[0] + _diag_heads(o, NSA_HPG * ts) * gate_ref[0, 0]


def _win_sample(win_t, new_rows, qbd, slope_col, gates, prev, ts, pos0):
    nb = win_t.shape[0]
    wb = win_t.shape[-1]
    nrow = NSA_HEADS * ts
    half = NSA_KV * NSA_HD
    per_b = lambda shape: pl.BlockSpec((1,) + shape, lambda i: (i,) + (0,) * len(shape))
    return pl.pallas_call(
        functools.partial(_win_sample_kernel, ts=ts, pos0=pos0, wb=wb),
        grid=(nb,),
        in_specs=[per_b(win_t.shape[1:]), per_b((ts, 2 * half)), per_b((nrow, half)),
                  pl.BlockSpec((nrow, 1), lambda i: (0, 0)),
                  pl.BlockSpec((1, 1, nrow, 1), lambda i: (2, i, 0, 0)),
                  per_b((nrow, NSA_HD))],
        out_specs=per_b((nrow, NSA_HD)),
        out_shape=jax.ShapeDtypeStruct((nb, nrow, NSA_HD), F32),
        compiler_params=_cparams(1),
        name="win_sample",
    )(win_t, new_rows, qbd, slope_col, gates, prev)


def _importance_matrix(nblk, nsb):
    rs = SEL_BLOCK // CMP_STRIDE
    rc = CMP_LEN // CMP_STRIDE
    s = np.zeros((nsb, nblk), np.float32)
    for j in range(nsb):
        for n in range(rc):
            for r in range(rs):
                m = rs * j + r - n
                if 0 <= m < nblk:
                    s[j, m] += 1.0
    return s


def _pad_heads(a):
    lead = a.shape[:-1]
    a = a.reshape(lead + (NSA_KV, NSA_HD))
    a = jnp.pad(a, ((0, 0),) * len(lead) + ((0, 0), (0, LANE - NSA_HD)))
    return a.reshape(lead + (NSA_KV * LANE,))


def _prep_weights(w_ffn_in, w_ffn_out, ln_g, ln_b, w_in_even, pool_w, pool_scale, gm_ln_g, gm_ln_b, gm_ws, gm_b,
                  w_out_even, w_in_odd, cmp_pe, cmp_w1, cmp_b1, cmp_w2, w_out_odd, ts, nb_s):
    d = w_in_even.shape[0]
    dff = w_ffn_out.shape[2]
    fp = -(-dff // (2 * LANE)) * (2 * LANE)
    w = {}
    w["ffn_g"] = jnp.pad(w_ffn_in[..., :dff], ((0, 0),) * 3 + ((0, fp - dff),)).astype(BF16)
    w["ffn_u"] = jnp.pad(w_ffn_in[..., dff:], ((0, 0),) * 3 + ((0, fp - dff),)).astype(BF16)
    w["ffn_o"] = jnp.pad(w_ffn_out, ((0, 0), (0, 0), (0, fp - dff), (0, 0))).astype(BF16)
    w["ln_g"] = ln_g.reshape(ln_g.shape[0], ln_g.shape[1], 1, d)
    w["ln_b"] = ln_b.reshape(ln_b.shape[0], ln_b.shape[1], 1, d)
    hw = gm_ws.shape[1]
    w["w_in_even"] = w_in_even.astype(BF16)
    w["pool_w"] = pool_w.astype(BF16)
    w["pool_scale"] = pool_scale.reshape(1, -1)
    w["gm_ln_g"] = gm_ln_g.reshape(1, -1)
    w["gm_ln_b"] = gm_ln_b.reshape(1, -1)
    w["gm_ws"] = jnp.tril(gm_ws).astype(BF16)
    w["gm_b_full"] = jnp.repeat(gm_b.T, hw, axis=1)
    cs = min(ts, GMLP_CHUNK)
    ws_s = jnp.tril(gm_ws[:, :cs, :cs])
    w["gm_ws_kron"] = jnp.einsum("ab,hts->hatbs", jnp.eye(nb_s * (ts // cs), dtype=F32), ws_s).reshape(
        gm_ws.shape[0], nb_s * ts, nb_s * ts).astype(BF16)
    w["gm_b_rows"] = jnp.tile(jnp.repeat(gm_b[:, :cs].T, hw, axis=1), (nb_s * (ts // cs), 1))
    w["w_out_even"] = w_out_even.astype(BF16)
    qw = NSA_HEADS * NSA_HD
    half = NSA_KV * NSA_HD
    kvw = 3 * 2 * half
    ng = w_in_odd.shape[1] - qw - kvw
    wq = w_in_odd[:, :qw]
    wkv = w_in_odd[:, qw:qw + kvw]
    wg = jnp.pad(w_in_odd[:, qw + kvw:], ((0, 0), (0, LANE - ng)))
    w["wq"] = wq.astype(BF16)
    w["wkv"] = wkv.astype(BF16)
    w["wg"] = wg.astype(BF16)
    w["wq_t"] = wq.T.astype(BF16)
    w["wkv_t"] = wkv.T.astype(BF16)
    w["wg_t"] = wg.T.astype(BF16)
    w["wkv_cmp"] = wkv[:, :2 * half].astype(BF16)
    w["wk_pad"] = jnp.concatenate([_pad_heads(wkv[:, 2 * half:3 * half]), _pad_heads(wkv[:, 4 * half:5 * half])],
                                  axis=1).astype(BF16)
    w["w_out_odd"] = w_out_odd.astype(BF16)
    rc = CMP_LEN // CMP_STRIDE
    eye = jnp.eye(NSA_KV, dtype=F32)
    pe = cmp_pe.reshape(2, rc, CMP_STRIDE, 1, NSA_HD)
    w["cmp_pe_rows"] = jnp.broadcast_to(pe, (2, rc, CMP_STRIDE, NSA_KV, NSA_HD)).reshape(2, rc, 1, -1)
    w1 = cmp_w1.reshape(2, rc, CMP_STRIDE, NSA_HD, CMP_HIDDEN)
    w["cmp_w1_bd"] = jnp.einsum("krpdf,gh->krpgdhf", w1, eye).reshape(
        2, rc, CMP_STRIDE * NSA_KV * NSA_HD, NSA_KV * CMP_HIDDEN).astype(BF16)
    w["cmp_b1_rows"] = jnp.tile(cmp_b1, (1, NSA_KV)).reshape(2, 1, -1)
    w2_bd = jnp.einsum("kfd,gh->kgfhd", cmp_w2, eye).reshape(2, NSA_KV * CMP_HIDDEN, half)
    w["cmp_w2_bd"] = w2_bd.astype(BF16)
    w["cmp_w2k_pad"] = _pad_heads(w2_bd[0]).astype(BF16)
    slopes = 2.0 ** (-8.0 * (jnp.arange(NSA_HEADS, dtype=F32) + 1.0) / NSA_HEADS)
    w["slopes"] = slopes
    hi = slopes.astype(BF16).astype(F32)
    mid = (slopes - hi).astype(BF16).astype(F32)
    lo = (slopes - hi - mid).astype(BF16).astype(F32)
    pieces = jnp.stack([hi, mid, lo], axis=1)
    rows = jnp.concatenate([pieces * SEL_BLOCK, pieces,
                            jnp.zeros((NSA_HEADS, KEY_SEL_LANE - KEY_POS_LANE - 6), F32)], axis=1)
    w["slope_rows"] = rows[:, :, None]
    return w


def _ffn(x, w, layer, which, tm):
    return _ffn_ln(x, w["ffn_g"][layer, which], w["ffn_u"][layer, which], w["ffn_o"][layer, which],
                   w["ln_g"][layer, which + (which > 0)], w["ln_b"][layer, which + (which > 0)], tm)


def _rows_from_feature_major(a_t):
    b, _, t = a_t.shape
    return a_t.reshape(b, 2, NSA_KV, NSA_HD, t).transpose(0, 4, 1, 2, 3)


def _trunk_prompt(x3, w):
    b, t, d = x3.shape
    m = b * t
    tm = 256
    tf = 512
    tq = min(512, t)
    tk = min(512, t)
    x = _ffn(x3.reshape(m, d), w, 0, 0, tf)
    x, pool_state = _mixer0_prompt(x.reshape(b, t, d), w, w["ln_g"][0, 1], w["ln_b"][0, 1], tm)
    x = _ffn(x.reshape(m, d), w, 0, 1, tf)
    x = _ffn(x, w, 1, 0, tf)

    qt, gt, ct, st, wt, kvc, kp, vt = _nsa_proj_prompt(x.reshape(b, t, d), w, tk)
    kc, vct = _compress_prompt(kvc, w)
    smat = jnp.asarray(_importance_matrix(kc.shape[1], -(-t // SEL_BLOCK)))
    o = _nsa_prompt(w["slopes"], w["slope_rows"], qt, gt, kc, vct, kp, vt, smat, tq)

    x = _outproj_ln(x, o.reshape(m, -1), w["w_out_odd"], w["ln_g"][1, 1], w["ln_b"][1, 1], tm)
    x = _ffn(x, w, 1, 1, tf)
    win = _rows_from_feature_major(wt[:, :, t - min(WINDOW, t):])
    return x.reshape(b, t, d), pool_state, _rows_from_feature_major(ct), _rows_from_feature_major(st), win


def _trunk_sample(x3, pool_past, cache_cmp, cache_slc, win_past, page_table, w):
    nb, ts, d = x3.shape
    m = nb * ts
    npages = page_table.shape[1]
    pos0 = npages * PAGE
    npg = 16
    x = _ffn(x3.reshape(m, d), w, 0, 0, m)
    x, pool_state, gm_v = _mixer0_sample(x, pool_past, w, w["ln_g"][0, 1], w["ln_b"][0, 1], nb, ts, pos0)
    x = _ffn(x, w, 0, 1, m)
    x = _ffn(x, w, 1, 0, m)

    q, kvc, kvs, kvw, gates = _nsa_proj(x, w["wq"], w["wkv"], w["wg"], m)
    half = NSA_KV * NSA_HD
    roww = 2 * half
    nrow = NSA_HEADS * ts
    qr = q.reshape(nb, ts, NSA_KV, NSA_HPG, NSA_HD).transpose(0, 2, 3, 1, 4).reshape(nb, NSA_KV, NSA_HPG * ts, NSA_HD)
    qbd = jnp.einsum("bgrd,gk->bgrkd", qr, jnp.eye(NSA_KV, dtype=BF16)).reshape(nb, nrow, half)
    slope_col = jnp.repeat(w["slopes"], ts).reshape(nrow, 1)
    gcols = gates[:, :3 * NSA_HEADS].reshape(nb, ts, 3, NSA_HEADS).transpose(2, 0, 3, 1).reshape(3, nb, nrow, 1)
    feature_major = lambda c: c.transpose(0, 2, 3, 4, 1)

    nblk = npages * PAGE // CMP_STRIDE
    nsb = -(-(pos0 + ts) // SEL_BLOCK)
    nsbp = -(-nsb // LANE) * LANE
    smat = jnp.asarray(np.pad(_importance_matrix(nblk, nsb).T, ((0, 0), (0, nsbp - nsb))))
    new_chunk = jnp.pad(kvc.reshape(nb, ts, roww), ((0, 0), (0, CMP_STRIDE - ts), (0, 0)))
    o, sel = _cmp_sample(page_table, feature_major(cache_cmp), new_chunk, qbd, slope_col, gcols, smat, w, npg, ts,
                         pos0)

    bpp = PAGE // SEL_BLOCK
    nsteps = npages // npg
    sel_rows = jnp.broadcast_to(sel.reshape(nb, NSA_KV, 1, ts, nsbp), (nb, NSA_KV, NSA_HPG, ts, nsbp))
    sel_rows = sel_rows.reshape(nb, nrow, nsbp)
    sel_steps = sel_rows[:, :, :npages * bpp].reshape(nb, nrow, nsteps, npg * bpp).transpose(0, 2, 1, 3).astype(BF16)
    sel_tail = sel_rows[:, :, npages * bpp:npages * bpp + 1]
    new_page = jnp.pad(kvs.reshape(nb, ts, roww), ((0, 0), (0, PAGE - ts), (0, 0)))
    o = _slc_sample(page_table, feature_major(cache_slc), new_page, qbd, slope_col, gcols, sel_steps, sel_tail, o,
                    npg, ts, pos0)
    win_t = feature_major(win_past)
    wb = win_t.shape[-1]
    new_win = kvw.reshape(nb, ts, roww)
    o = _win_sample(win_t, new_win, qbd, slope_col, gcols, o, ts, pos0)

    o = o.reshape(nb, NSA_KV, NSA_HPG, ts, NSA_HD).transpose(0, 3, 1, 2, 4).reshape(m, NSA_HEADS * NSA_HD)
    x = _outproj_ln(x, o, w["w_out_odd"], w["ln_g"][1, 1], w["ln_b"][1, 1], m)
    x = _ffn(x, w, 1, 1, m)
    kv5 = lambda a: a.reshape(nb, ts, 2, NSA_KV, NSA_HD)
    new_win_t = feature_major(kv5(kvw))
    win_buf = jnp.concatenate([win_t, new_win_t], axis=-1)[..., -wb:].transpose(0, 4, 1, 2, 3)
    return x.reshape(nb, ts, d), pool_state, gm_v.reshape(nb, ts, -1), kv5(kvc), kv5(kvs), win_buf


def kernel(x_prompt, x_sample, state_l0_pool, cache_l1_cmp_kv, cache_l1_slc_kv, cache_l1_win_kv, page_table,
           w_ffn_in, w_ffn_out, ln_g, ln_b, w_in_even, pool_w, pool_scale, gm_ln_g, gm_ln_b, gm_ws, gm_b,
           w_out_even, w_in_odd, cmp_pe, cmp_w1, cmp_b1, cmp_w2, w_out_odd):
    w = _prep_weights(w_ffn_in, w_ffn_out, ln_g, ln_b, w_in_even, pool_w, pool_scale, gm_ln_g, gm_ln_b, gm_ws,
                      gm_b, w_out_even, w_in_odd, cmp_pe, cmp_w1, cmp_b1, cmp_w2, w_out_odd,
                      x_sample.shape[1], x_sample.shape[0])
    y_p, pool_p, cmp_p, slc_p, win_p = _trunk_prompt(x_prompt, w)
    y_s, pool_s, gmv_s, cmp_s, slc_s, win_s = _trunk_sample(
        x_sample, state_l0_pool, cache_l1_cmp_kv, cache_l1_slc_kv, cache_l1_win_kv, page_table, w)
    return (y_p, y_s, pool_p, pool_s, gmv_s, cmp_p, slc_p, win_p, cmp_s, slc_s, win_s)
```

```python
import functools

import numpy as np
import jax
import jax.numpy as jnp
from jax import lax
from jax.experimental import pallas as pl
from jax.experimental.pallas import tpu as pltpu

F32 = jnp.float32
BF16 = jnp.bfloat16

DEPTH = 2
DN_ALPHA = (2 * DEPTH) ** 0.25
LN_EPS = 1e-5
POOL_WINDOWS = (2, 4, 8, 16)
POOL_STATE = max(POOL_WINDOWS) - 1
POOL_HALO = 16
GMLP_CHUNK = 128
NSA_HEADS = 16
NSA_KV = 4
NSA_HPG = NSA_HEADS // NSA_KV
NSA_HD = 64
CMP_LEN = 32
CMP_STRIDE = 16
CMP_HIDDEN = 128
SEL_BLOCK = 64
N_SEL = 16
WINDOW = 512
PAGE = 128

LANE = 128
MASK_NEG = -1e30
SEL_BIG = 2.0 ** 100
KEY_POS_LANE = NSA_HD
KEY_SEL_LANE = NSA_HD + 16
CHUNK_PITCH = 20
VMEM_BIG = 56 << 20

_NT = (((1,), (1,)), ((), ()))


def _cparams(n_axes, vmem=None, flags=None):
    return pltpu.CompilerParams(dimension_semantics=("arbitrary",) * n_axes, vmem_limit_bytes=vmem, flags=flags)


def _resident(shape):
    nd = len(shape)
    return pl.BlockSpec(shape, lambda *_: (0,) * nd, pipeline_mode=pl.Buffered(1))


def _whole(shape):
    nd = len(shape)
    return pl.BlockSpec(shape, lambda *_: (0,) * nd)


def _dot(a, b):
    return jnp.dot(a, b, preferred_element_type=F32)


def _dot_nt(a, b):
    return lax.dot_general(a, b, _NT, preferred_element_type=F32)


def _ln(x, g, b):
    mu = jnp.mean(x, -1, keepdims=True)
    xc = x - mu
    var = jnp.mean(xc * xc, -1, keepdims=True)
    return xc * lax.rsqrt(var + LN_EPS) * g + b


def _masked_softmax(s, ok, axis):
    s = jnp.where(ok, s, MASK_NEG)
    m = jnp.max(s, axis, keepdims=True)
    e = jnp.where(ok, jnp.exp(s - m), 0.0)
    return e / jnp.maximum(jnp.sum(e, axis, keepdims=True), 1e-30)


def _ffn_kernel(x_ref, wg_ref, wu_ref, wo_ref, g_ref, b_ref, o_ref):
    x = x_ref[...]
    xb = x.astype(BF16)
    gate = _dot(xb, wg_ref[...])
    up = _dot(xb, wu_ref[...])
    act = (gate * jax.nn.sigmoid(gate) * up).astype(BF16)
    y = _dot(act, wo_ref[...])
    o_ref[...] = _ln(DN_ALPHA * x + 0.5 * y, g_ref[...], b_ref[...])


def _ffn_ln(x, wg, wu, wo, g, b, tm):
    m, d = x.shape
    fp = wg.shape[1]
    return pl.pallas_call(
        _ffn_kernel,
        grid=(m // tm,),
        in_specs=[pl.BlockSpec((tm, d), lambda i: (i, 0)),
                  _resident((d, fp)), _resident((d, fp)), _resident((fp, d)),
                  _resident((1, d)), _resident((1, d))],
        out_specs=pl.BlockSpec((tm, d), lambda i: (i, 0)),
        out_shape=jax.ShapeDtypeStruct((m, d), F32),
        compiler_params=_cparams(1, VMEM_BIG),
        name="ffn_ln",
    )(x, wg, wu, wo, g, b)


def _outproj_kernel(x_ref, a_ref, w_ref, g_ref, b_ref, o_ref):
    y = _dot(a_ref[...].astype(BF16), w_ref[...])
    o_ref[...] = _ln(DN_ALPHA * x_ref[...] + y, g_ref[...], b_ref[...])


def _outproj_ln(x, a, w, g, b, tm):
    m, d = x.shape
    k = a.shape[1]
    return pl.pallas_call(
        _outproj_kernel,
        grid=(m // tm,),
        in_specs=[pl.BlockSpec((tm, d), lambda i: (i, 0)),
                  pl.BlockSpec((tm, k), lambda i: (i, 0)),
                  _resident((k, d)), _resident((1, d)), _resident((1, d))],
        out_specs=pl.BlockSpec((tm, d), lambda i: (i, 0)),
        out_shape=jax.ShapeDtypeStruct((m, d), F32),
        compiler_params=_cparams(1),
        name="outproj_ln",
    )(x, a, w, g, b)


def _pool_groups(read_window, p, cnt, poolw_ref, gw):
    outs = []
    for g, w in enumerate(POOL_WINDOWS):
        lanes = slice(g * gw, (g + 1) * gw)
        tot = p[..., lanes]
        for j in range(1, w):
            tot = tot + read_window(j, lanes)
        d = tot / jnp.minimum(float(w), cnt) - p[..., lanes]
        d2 = d.reshape(-1, gw).astype(BF16)
        outs.append(_dot(d2, poolw_ref[g]))
    return jnp.concatenate(outs, axis=-1)


def _mixer0_prompt_kernel(x_ref, win_ref, poolw_ref, pscale_ref, glng_ref, glnb_ref, ws_ref, gmb_ref,
                          wout_ref, lng_ref, lnb_ref, y_ref, pstate_ref, ext_ref, *, tm, pw, gw, hw):
    t = pl.program_id(1)
    x = x_ref[0]
    z = _dot(x.astype(BF16), win_ref[...])
    p = z[:, :pw]
    u = z[:, pw:pw + 4 * hw]
    v = _ln(z[:, pw + 4 * hw:], glng_ref[...], glnb_ref[...])

    @pl.when(t == 0)
    def _():
        ext_ref[0:POOL_HALO, :] = jnp.zeros((POOL_HALO, pw), F32)

    ext_ref[POOL_HALO:POOL_HALO + tm, :] = p
    cnt = (t * tm + lax.broadcasted_iota(jnp.int32, (tm, 1), 0) + 1).astype(F32)
    a = _pool_groups(lambda j, lanes: ext_ref[POOL_HALO - j:POOL_HALO - j + tm, lanes], p, cnt, poolw_ref, gw)
    a = a * pscale_ref[...]
    tail = ext_ref[tm:tm + POOL_HALO, :]
    pstate_ref[0] = tail
    ext_ref[0:POOL_HALO, :] = tail

    vb = v.astype(BF16)
    rows = []
    for c in range(tm // GMLP_CHUNK):
        r = slice(c * GMLP_CHUNK, (c + 1) * GMLP_CHUNK)
        rows.append(jnp.concatenate(
            [_dot(ws_ref[h], vb[r, h * hw:(h + 1) * hw]) for h in range(4)], axis=-1) + gmb_ref[...])
    gb = u * jnp.concatenate(rows, axis=0)
    y = _dot(a.astype(BF16), wout_ref[0:pw, :]) + _dot(gb.astype(BF16), wout_ref[pw:, :])
    y_ref[0] = _ln(DN_ALPHA * x + y, lng_ref[...], lnb_ref[...])


def _mixer0_prompt(x, w, lng, lnb, tm):
    b, t, d = x.shape
    pw = w["pool_scale"].shape[1]
    gw = pw // len(POOL_WINDOWS)
    hw = w["gm_ws"].shape[1]
    kern = functools.partial(_mixer0_prompt_kernel, tm=tm, pw=pw, gw=gw, hw=hw)
    y, pstate = pl.pallas_call(
        kern,
        grid=(b, t // tm),
        in_specs=[pl.BlockSpec((1, tm, d), lambda i, j: (i, j, 0)),
                  _resident(w["w_in_even"].shape), _resident(w["pool_w"].shape), _resident(w["pool_scale"].shape),
                  _resident(w["gm_ln_g"].shape), _resident(w["gm_ln_b"].shape), _resident(w["gm_ws"].shape),
                  _resident(w["gm_b_full"].shape), _resident(w["w_out_even"].shape),
                  _resident((1, d)), _resident((1, d))],
        out_specs=[pl.BlockSpec((1, tm, d), lambda i, j: (i, j, 0)),
                   pl.BlockSpec((1, POOL_HALO, pw), lambda i, j: (i, 0, 0))],
        out_shape=[jax.ShapeDtypeStruct((b, t, d), F32), jax.ShapeDtypeStruct((b, POOL_HALO, pw), F32)],
        scratch_shapes=[pltpu.VMEM((POOL_HALO + tm, pw), F32)],
        compiler_params=_cparams(2),
        name="mixer0_prompt",
    )(x, w["w_in_even"], w["pool_w"], w["pool_scale"], w["gm_ln_g"], w["gm_ln_b"], w["gm_ws"],
      w["gm_b_full"], w["w_out_even"], lng, lnb)
    return y, pstate[:, POOL_HALO - POOL_STATE:]


def _mixer0_sample_kernel(x_ref, past_ref, win_ref, poolw_ref, pscale_ref, glng_ref, glnb_ref, wk_ref, gmb_ref,
                          wout_ref, lng_ref, lnb_ref, y_ref, pstate_ref, gmv_ref, ext_ref,
                          *, nb, ts, pos0, pw, gw, hw):
    x = x_ref[...]
    z = _dot(x.astype(BF16), win_ref[...])
    p = z[:, :pw]
    u = z[:, pw:pw + 4 * hw]
    v = _ln(z[:, pw + 4 * hw:], glng_ref[...], glnb_ref[...])
    gmv_ref[...] = v

    p3 = p.reshape(nb, ts, pw)
    ext_ref[:, 0:POOL_HALO, :] = past_ref[...]
    ext_ref[:, POOL_HALO:POOL_HALO + ts, :] = p3
    cnt = (pos0 + lax.broadcasted_iota(jnp.int32, (1, ts, 1), 1) + 1).astype(F32)
    a = _pool_groups(lambda j, lanes: ext_ref[:, POOL_HALO - j:POOL_HALO - j + ts, lanes], p3, cnt, poolw_ref, gw)
    a = a * pscale_ref[...]
    pstate_ref[...] = ext_ref[:, ts:ts + POOL_HALO, :]

    vb = v.astype(BF16)
    mix = jnp.concatenate([_dot(wk_ref[h], vb[:, h * hw:(h + 1) * hw]) for h in range(4)], axis=-1) + gmb_ref[...]
    gb = u * mix
    y = _dot(a.astype(BF16), wout_ref[0:pw, :]) + _dot(gb.astype(BF16), wout_ref[pw:, :])
    y_ref[...] = _ln(DN_ALPHA * x + y, lng_ref[...], lnb_ref[...])


def _mixer0_sample(x, past, w, lng, lnb, nb, ts, pos0):
    m, d = x.shape
    pw = w["pool_scale"].shape[1]
    gw = pw // len(POOL_WINDOWS)
    hw = w["gm_ws"].shape[1]
    kern = functools.partial(_mixer0_sample_kernel, nb=nb, ts=ts, pos0=pos0, pw=pw, gw=gw, hw=hw)
    past16 = jnp.pad(past, ((0, 0), (POOL_HALO - POOL_STATE, 0), (0, 0)))
    args = (x, past16, w["w_in_even"], w["pool_w"], w["pool_scale"], w["gm_ln_g"], w["gm_ln_b"],
            w["gm_ws_kron"], w["gm_b_rows"], w["w_out_even"], lng, lnb)
    y, pstate, gmv = pl.pallas_call(
        kern,
        grid=(1,),
        in_specs=[_resident(a.shape) for a in args],
        out_specs=[_whole((m, d)), _whole((nb, POOL_HALO, pw)), _whole((m, 4 * hw))],
        out_shape=[jax.ShapeDtypeStruct((m, d), F32), jax.ShapeDtypeStruct((nb, POOL_HALO, pw), F32),
                   jax.ShapeDtypeStruct((m, 4 * hw), F32)],
        scratch_shapes=[pltpu.VMEM((nb, POOL_HALO + ts, pw), F32)],
        compiler_params=_cparams(1),
        name="mixer0_sample",
    )(*args)
    return y, pstate[:, POOL_HALO - POOL_STATE:], gmv


def _nsa_proj_kernel(x_ref, wq_ref, wkv_ref, wg_ref, q_ref, kc_ref, ks_ref, kw_ref, g_ref, *, kvw):
    xb = x_ref[...].astype(BF16)
    q_ref[...] = (_dot(xb, wq_ref[...]) * (NSA_HD ** -0.5)).astype(BF16)
    kv = _dot(xb, wkv_ref[...])
    kc_ref[...] = kv[:, 0:kvw]
    ks_ref[...] = kv[:, kvw:2 * kvw]
    kw_ref[...] = kv[:, 2 * kvw:3 * kvw]
    g_ref[...] = jax.nn.sigmoid(_dot(xb, wg_ref[...]))


def _nsa_proj(x, wq, wkv, wg, tm):
    m, d = x.shape
    qw = wq.shape[1]
    kvw = wkv.shape[1] // 3
    row = lambda n: pl.BlockSpec((tm, n), lambda i: (i, 0))
    return pl.pallas_call(
        functools.partial(_nsa_proj_kernel, kvw=kvw),
        grid=(m // tm,),
        in_specs=[row(d), _resident(wq.shape), _resident(wkv.shape), _resident(wg.shape)],
        out_specs=[row(qw), row(kvw), row(kvw), row(kvw), row(LANE)],
        out_shape=[jax.ShapeDtypeStruct((m, qw), BF16)] + [jax.ShapeDtypeStruct((m, kvw), F32)] * 3
        + [jax.ShapeDtypeStruct((m, LANE), F32)],
        compiler_params=_cparams(1),
        name="nsa_proj",
    )(x, wq, wkv, wg)


def _nsa_proj_prompt_kernel(x_ref, wqt_ref, wgt_ref, wkvt_ref, wkc_ref, wkp_ref,
                            qt_ref, gt_ref, ct_ref, st_ref, wt_ref, kvc_ref, kp_ref, vt_ref, *, kvw):
    xb = x_ref[0].astype(BF16)
    qt_ref[0] = (_dot_nt(wqt_ref[...], xb) * (NSA_HD ** -0.5)).astype(BF16)
    gt_ref[0] = jax.nn.sigmoid(_dot_nt(wgt_ref[...], xb))
    kvt = _dot_nt(wkvt_ref[...], xb)
    ct_ref[0] = kvt[0:kvw]
    st_ref[0] = kvt[kvw:2 * kvw]
    wt_ref[0] = kvt[2 * kvw:3 * kvw]
    half = kvw // 2
    vt_ref[0, 0] = jnp.concatenate([kvt[kvw + half:2 * kvw], kvt[2 * kvw + half:3 * kvw]], axis=0).astype(BF16)
    kvc_ref[0] = _dot(xb, wkc_ref[...])
    kp = _dot(xb, wkp_ref[...])
    tm = kp.shape[0]
    sub = lax.broadcasted_iota(jnp.int32, kp.shape, 1) % LANE
    grp = lax.broadcasted_iota(jnp.int32, kp.shape, 1) // LANE
    kabs = pl.program_id(1) * tm + lax.broadcasted_iota(jnp.int32, kp.shape, 0)
    blk = kabs // SEL_BLOCK
    pat = jnp.where((sub >= KEY_POS_LANE) & (sub < KEY_POS_LANE + 3), blk,
                    jnp.where((sub >= KEY_POS_LANE + 3) & (sub < KEY_POS_LANE + 6), kabs % SEL_BLOCK, 0))
    pat = jnp.where((grp < NSA_KV) & (sub - KEY_SEL_LANE == blk), 1, pat)
    kp_ref[0] = (kp + pat.astype(F32)).astype(BF16)


def _nsa_proj_prompt(x, w, tm):
    b, t, d = x.shape
    qw = w["wq_t"].shape[0]
    kvw = w["wkv_t"].shape[0] // 3
    kpw = w["wk_pad"].shape[1]
    feat = lambda n: pl.BlockSpec((1, n, tm), lambda i, j: (i, 0, j))
    tok = lambda n: pl.BlockSpec((1, tm, n), lambda i, j: (i, j, 0))
    return pl.pallas_call(
        functools.partial(_nsa_proj_prompt_kernel, kvw=kvw),
        grid=(b, t // tm),
        in_specs=[tok(d), _resident(w["wq_t"].shape), _resident(w["wg_t"].shape), _resident(w["wkv_t"].shape),
                  _resident(w["wkv_cmp"].shape), _resident(w["wk_pad"].shape)],
        out_specs=[feat(qw), feat(LANE), feat(kvw), feat(kvw), feat(kvw), tok(kvw), tok(kpw),
                   pl.BlockSpec((1, 1, kvw, tm), lambda i, j: (i, j, 0, 0))],
        out_shape=[jax.ShapeDtypeStruct((b, qw, t), BF16), jax.ShapeDtypeStruct((b, LANE, t), F32)]
        + [jax.ShapeDtypeStruct((b, kvw, t), F32)] * 3
        + [jax.ShapeDtypeStruct((b, t, kvw), F32), jax.ShapeDtypeStruct((b, t, kpw), BF16),
           jax.ShapeDtypeStruct((b, t // tm, kvw, tm), BF16)],
        compiler_params=_cparams(2),
        name="nsa_proj_prompt",
    )(x, w["wq_t"], w["wg_t"], w["wkv_t"], w["wkv_cmp"], w["wk_pad"])


def _compress_hidden(read, kv, pe_ref, w1_ref, b1_ref):
    def lhs(r):
        parts = [read(kv, r, p) for p in range(CMP_STRIDE)]
        return (jnp.concatenate(parts, axis=-1) + pe_ref[kv, r]).astype(BF16)

    hid = _dot(lhs(0), w1_ref[kv, 0]) + _dot(lhs(1), w1_ref[kv, 1]) + b1_ref[kv]
    return jax.nn.gelu(hid, approximate=True).astype(BF16)


def _compress_prompt_kernel(x_ref, pe_ref, w1_ref, b1_ref, w2k_ref, w2v_ref, kc_ref, vct_ref, xs_ref, *, n):
    roww = 2 * NSA_KV * NSA_HD
    half = NSA_KV * NSA_HD
    xs_ref[0:n, :] = x_ref[0]
    xs_ref[n:n + 8, :] = jnp.zeros((8, xs_ref.shape[1]), F32)
    valid = lax.broadcasted_iota(jnp.int32, (n, 1), 0) < n - 1

    def read(kv, r, p):
        return xs_ref[r:r + n, p * roww + kv * half:p * roww + (kv + 1) * half]

    kc = _dot(_compress_hidden(read, 0, pe_ref, w1_ref, b1_ref), w2k_ref[...])
    kc_ref[0] = jnp.where(valid, kc, 0.0).astype(BF16)
    vc = _dot(_compress_hidden(read, 1, pe_ref, w1_ref, b1_ref), w2v_ref[...])
    vct_ref[0] = jnp.where(valid, vc, 0.0).T.astype(BF16)


def _compress_prompt(kv_rows, w):
    b, t, roww = kv_rows.shape
    n = t // CMP_STRIDE
    x = kv_rows.reshape(b, n, CMP_STRIDE * roww)
    half = roww // 2
    return pl.pallas_call(
        functools.partial(_compress_prompt_kernel, n=n),
        grid=(b,),
        in_specs=[pl.BlockSpec((1, n, CMP_STRIDE * roww), lambda i: (i, 0, 0)),
                  _resident(w["cmp_pe_rows"].shape), _resident(w["cmp_w1_bd"].shape),
                  _resident(w["cmp_b1_rows"].shape), _resident(w["cmp_w2k_pad"].shape),
                  _resident(w["cmp_w2_bd"].shape[1:])],
        out_specs=[pl.BlockSpec((1, n, NSA_KV * LANE), lambda i: (i, 0, 0)),
                   pl.BlockSpec((1, half, n), lambda i: (i, 0, 0))],
        out_shape=[jax.ShapeDtypeStruct((b, n, NSA_KV * LANE), BF16), jax.ShapeDtypeStruct((b, half, n), BF16)],
        scratch_shapes=[pltpu.VMEM((n + 8, CMP_STRIDE * roww), F32)],
        compiler_params=_cparams(1, VMEM_BIG),
        name="compress_prompt",
    )(x, w["cmp_pe_rows"], w["cmp_w1_bd"], w["cmp_b1_rows"], w["cmp_w2k_pad"], w["cmp_w2_bd"][1])


def _select_topk_cols(score, n_sel):
    nblk = score.shape[0]
    jj = lax.broadcasted_iota(jnp.int32, score.shape, 0)
    rank = jnp.zeros(score.shape, F32)
    for jp in range(nblk):
        row = score[jp:jp + 1, :]
        rank = rank + jnp.where(row > score, 1.0, jnp.where((row == score) & (jj > jp), 1.0, 0.0))
    return rank < float(n_sel)


def _nsa_prompt_kernel(slopes_ref, qt_ref, gt_ref, srow_ref, kc_ref, vct_ref, ks_ref, vst_ref, kw_ref, vwt_ref,
                       smat_ref, o_ref, m_ref, l_ref, acc_ref, osum_ref, *, tq, tk, nblk, nsb):
    g = pl.program_id(1)
    t = pl.program_id(2)
    nh = NSA_HPG
    hd = NSA_HD
    slopes = [slopes_ref[g * nh + h] for h in range(nh)]
    gate = lambda br, h: gt_ref[0, pl.ds(br * NSA_HEADS + g * nh + h, 1), :]
    qt = qt_ref[0]
    tcol = t * tq + lax.broadcasted_iota(jnp.int32, (1, tq), 1)

    def stack_queries(extra_rows):
        return jnp.concatenate(
            [jnp.concatenate([qt[h * hd:(h + 1) * hd], extra_rows(h)], axis=0) for h in range(nh)], axis=1)

    sc = _dot(kc_ref[0], stack_queries(lambda h: jnp.zeros((LANE - hd, tq), BF16)))
    cend = lax.broadcasted_iota(jnp.int32, (nblk, 1), 0) * CMP_STRIDE + (CMP_LEN - 1)
    dist_i = tcol - cend
    ok = dist_i >= 0
    dist = dist_i.astype(F32)
    pc = None
    for h in range(nh):
        p = _masked_softmax(sc[:, h * tq:(h + 1) * tq] - slopes[h] * dist, ok, 0)
        osum_ref[h * hd:(h + 1) * hd, :] = _dot(vct_ref[0], p.astype(BF16)) * gate(0, h)
        pc = p if pc is None else pc + p
    imp = jnp.dot(smat_ref[...], pc, precision=lax.Precision.HIGHEST, preferred_element_type=F32)
    jt = tcol // SEL_BLOCK
    jblk = lax.broadcasted_iota(jnp.int32, (nsb, tq), 0)
    forced = (jblk == 0) | (jblk == jt) | (jblk == jt - 1)
    score = jnp.where(forced, jnp.inf, jnp.where(jblk <= jt, imp, -jnp.inf))
    unsel = jnp.where(_select_topk_cols(score, min(N_SEL, nsb)), 0.0, -SEL_BIG).astype(BF16)
    pad_rows = jnp.zeros((LANE - KEY_SEL_LANE - nsb, tq), BF16)
    q4 = stack_queries(lambda h: jnp.concatenate(
        [jnp.broadcast_to(srow_ref[h], (KEY_SEL_LANE - hd, tq)).astype(BF16), unsel, pad_rows], axis=0))

    def reset():
        m_ref[...] = jnp.full(m_ref.shape, MASK_NEG, F32)
        l_ref[...] = jnp.zeros(l_ref.shape, F32)
        acc_ref[...] = jnp.zeros(acc_ref.shape, F32)

    def tile_local(k_ref, vt_ref, kt, mask):
        off = pl.multiple_of(kt * tk, tk)
        s = _dot(k_ref[0, pl.ds(off, tk), :], q4)
        if mask is not None:
            d_i = (t * tq - kt * tk + lax.broadcasted_iota(jnp.int32, (1, tq), 1)
                   - lax.broadcasted_iota(jnp.int32, (tk, 1), 0))
            okm = d_i >= 0 if mask == "causal" else d_i < WINDOW
            s = jnp.where(jnp.concatenate([okm] * nh, axis=1), s, MASK_NEG)
        m_t = jnp.max(s, 0, keepdims=True)
        e = jnp.exp(s - m_t)
        l_t = jnp.sum(e, 0, keepdims=True)
        eb = e.astype(BF16)
        vt = vt_ref[0, kt]
        return m_t, l_t, [_dot(vt, eb[:, h * tq:(h + 1) * tq]) for h in range(nh)]

    def merge(piece):
        m_t, l_t, pv = piece
        m_old = m_ref[...]
        m_new = jnp.maximum(m_old, m_t)
        a_old = jnp.exp(m_old - m_new)
        a_t = jnp.where(m_t > MASK_NEG, jnp.exp(m_t - m_new), 0.0)
        l_ref[...] = a_old * l_ref[...] + a_t * l_t
        m_ref[...] = m_new
        for h in range(nh):
            rows = slice(h * hd, (h + 1) * hd)
            cols = slice(h * tq, (h + 1) * tq)
            acc_ref[rows, :] = a_old[:, cols] * acc_ref[rows, :] + a_t[:, cols] * pv[h]

    def tiles(k_ref, vt_ref, kt0, masks):
        pieces = [tile_local(k_ref, vt_ref, kt0 + i, mk) for i, mk in enumerate(masks)]
        for piece in pieces:
            merge(piece)

    def finish(br):
        for h in range(nh):
            rows = slice(h * hd, (h + 1) * hd)
            o = acc_ref[rows, :] / jnp.maximum(l_ref[:, h * tq:(h + 1) * tq], 1e-30)
            osum_ref[rows, :] = osum_ref[rows, :] + o * gate(br, h)

    nd = tq // tk
    kd = t * nd
    reset()
    tiles(ks_ref, vst_ref, kd, ["causal"] * nd)

    def slc_body(i, c):
        tiles(ks_ref, vst_ref, i * nd, [None] * nd)
        return c

    lax.fori_loop(0, t, slc_body, 0)
    finish(1)
    reset()
    tiles(kw_ref, vwt_ref, kd, ["causal"] * nd)
    nwin = WINDOW // tk

    @pl.when(kd >= nwin)
    def _():
        tiles(kw_ref, vwt_ref, kd - nwin, ["window"] * nwin)

    finish(2)
    o_ref[0] = osum_ref[...].T


def _nsa_prompt(slopes, slope_rows, qt, gt, kc, vct, kp, vt, smat, tq):
    b, qw, t = qt.shape
    nblk = kc.shape[1]
    nsb = smat.shape[0]
    tk = vt.shape[-1]
    ntile = vt.shape[1]
    gw = NSA_HPG * NSA_HD
    assert tq % tk == 0 and WINDOW % tk == 0 and tq >= WINDOW
    assert KEY_SEL_LANE + nsb <= LANE and nsb <= 256
    kern = functools.partial(_nsa_prompt_kernel, tq=tq, tk=tk, nblk=nblk, nsb=nsb)
    keys = lambda br: pl.BlockSpec((1, t, LANE), lambda i, g, j: (i, 0, br * NSA_KV + g))
    vals = lambda br: pl.BlockSpec((1, ntile, NSA_HD, tk), lambda i, g, j: (i, 0, br * NSA_KV + g, 0))
    return pl.pallas_call(
        kern,
        grid=(b, NSA_KV, t // tq),
        in_specs=[pl.BlockSpec(memory_space=pltpu.SMEM),
                  pl.BlockSpec((1, gw, tq), lambda i, g, j: (i, g, j)),
                  pl.BlockSpec((1, LANE, tq), lambda i, g, j: (i, 0, j)),
                  pl.BlockSpec((NSA_HPG,) + slope_rows.shape[1:], lambda i, g, j: (g, 0, 0)),
                  pl.BlockSpec((1, nblk, LANE), lambda i, g, j: (i, 0, g)),
                  pl.BlockSpec((1, NSA_HD, nblk), lambda i, g, j: (i, g, 0)),
                  keys(0), vals(0), keys(1), vals(1),
                  pl.BlockSpec(smat.shape, lambda i, g, j: (0, 0))],
        out_specs=pl.BlockSpec((1, tq, gw), lambda i, g, j: (i, j, g)),
        out_shape=jax.ShapeDtypeStruct((b, t, qw), F32),
        scratch_shapes=[pltpu.VMEM((1, NSA_HPG * tq), F32), pltpu.VMEM((1, NSA_HPG * tq), F32),
                        pltpu.VMEM((gw, tq), F32), pltpu.VMEM((gw, tq), F32)],
        compiler_params=_cparams(3),
        name="nsa_prompt",
    )(slopes, qt, gt, slope_rows, kc, vct, kp, vt, kp, vt, smat)


def _diag_heads(o, rows_per_g):
    return jnp.concatenate(
        [o[g * rows_per_g:(g + 1) * rows_per_g, g * NSA_HD:(g + 1) * NSA_HD] for g in range(NSA_KV)], axis=0)


def _select_topk_rows(score, n_sel):
    jj = lax.broadcasted_iota(jnp.int32, score.shape, 1)
    big = jnp.int32(score.shape[1])
    taken = jnp.zeros(score.shape, jnp.bool_)
    for _ in range(n_sel):
        cur = jnp.where(taken, -jnp.inf, score)
        m = jnp.max(cur, -1, keepdims=True)
        cand = jnp.where(jnp.logical_not(taken) & (cur >= m), jj, big)
        pick = jnp.min(cand, -1, keepdims=True)
        taken = taken | (jj == pick)
    return taken


def _page_rows(page_ref, kv):
    return page_ref[0, kv].reshape(NSA_KV * NSA_HD, PAGE)


def _cmp_sample_kernel(pt_ref, *refs, npg, nsteps, ts, pos0, nblk, nsb, nsbp):
    pages = refs[:npg]
    (nxt_ref, new_ref, w1_ref, bias_ref, w2_ref, q_ref, slope_ref, gate_ref, smat_ref,
     o_ref, sel_ref, xs_ref, kc_ref, vc_ref) = refs[npg:]
    s = pl.program_id(1)
    half = NSA_KV * NSA_HD
    n = npg * PAGE // CMP_STRIDE
    nc = n + 8
    last = s == nsteps - 1
    nlh = half // LANE
    cpp = PAGE // CMP_STRIDE
    for kv in range(2):
        def put_chunk(i, val):
            for c in range(nlh):
                xs_ref[kv, c, i * CHUNK_PITCH:i * CHUNK_PITCH + CMP_STRIDE, :] = val[:, c * LANE:(c + 1) * LANE]

        for k in range(npg):
            rows = _page_rows(pages[k], kv).T
            for i in range(cpp):
                put_chunk(k * cpp + i, rows[i * CMP_STRIDE:(i + 1) * CMP_STRIDE])
        nxt = _page_rows(nxt_ref, kv).T[0:CMP_STRIDE]
        new = new_ref[0][:, kv * half:(kv + 1) * half]
        put_chunk(n, jnp.where(last, new, nxt))
        for i in range(n + 1, nc):
            put_chunk(i, jnp.zeros((CMP_STRIDE, half), F32))

    low_lanes = lax.broadcasted_iota(jnp.int32, (nc, LANE), 1) < NSA_HD

    def chunk_rows(kv, plane):
        even, odd = [], []
        for j in range(CMP_STRIDE // 2):
            a = xs_ref[kv, plane, pl.ds(2 * j, nc, stride=CHUNK_PITCH), :]
            b = xs_ref[kv, plane, pl.ds(2 * j + 1, nc, stride=CHUNK_PITCH), :]
            even.append(jnp.where(low_lanes, a, pltpu.roll(b, NSA_HD, 1)))
            odd.append(jnp.where(low_lanes, pltpu.roll(a, NSA_HD, 1), b))
        return [jnp.concatenate(v, axis=-1).astype(BF16) for v in (even, odd)]

    off = pl.multiple_of(s * n, n)
    for kv, c_ref in ((0, kc_ref), (1, vc_ref)):
        heads = [x for plane in range(nlh) for x in chunk_rows(kv, plane)]
        r = _dot(jnp.concatenate(heads, axis=0), w1_ref[kv])
        hid = jnp.concatenate(
            [r[g * nc:g * nc + n, :CMP_HIDDEN] + r[g * nc + 1:g * nc + n + 1, CMP_HIDDEN:] for g in range(NSA_KV)],
            axis=-1) + bias_ref[kv]
        c = _dot(jax.nn.gelu(hid, approximate=True).astype(BF16), w2_ref[kv])
        c_ref[pl.ds(off, n), :] = c.astype(BF16)

    @pl.when(last)
    def _():
        nrow = NSA_HEADS * ts
        trow = pos0 + lax.broadcasted_iota(jnp.int32, (nrow, 1), 0) % ts
        cend = lax.broadcasted_iota(jnp.int32, (1, nblk), 1) * CMP_STRIDE + (CMP_LEN - 1)
        dist_i = trow - cend
        sc = _dot_nt(q_ref[0], kc_ref[...]) - slope_ref[...] * dist_i.astype(F32)
        p = _masked_softmax(sc, dist_i >= 0, -1)
        o = _diag_heads(_dot(p.astype(BF16), vc_ref[...]), NSA_HPG * ts)
        o_ref[0] = o * gate_ref[0, 0]
        p4 = p.reshape(NSA_KV, NSA_HPG, ts, nblk)
        pc = p4[:, 0]
        for h in range(1, NSA_HPG):
            pc = pc + p4[:, h]
        imp = jnp.dot(pc.reshape(NSA_KV * ts, nblk), smat_ref[...], precision=lax.Precision.HIGHEST,
                      preferred_element_type=F32)
        tsel = pos0 + lax.broadcasted_iota(jnp.int32, (NSA_KV * ts, 1), 0) % ts
        jt = tsel // SEL_BLOCK
        jblk = lax.broadcasted_iota(jnp.int32, (NSA_KV * ts, nsbp), 1)
        forced = (jblk == 0) | (jblk == jt) | (jblk == jt - 1)
        score = jnp.where(forced, jnp.inf, jnp.where(jblk <= jt, imp, -jnp.inf))
        score = jnp.where(jblk < nsb, score, -jnp.inf)
        sel_ref[0] = jnp.where(_select_topk_rows(score, min(N_SEL, nsb)), 0.0, -1.0)


def _cmp_sample(page_table, cache_t, new_chunk, qbd, slope_col, gates, smat, w, npg, ts, pos0):
    nb, npages = page_table.shape
    half = NSA_KV * NSA_HD
    nsteps = npages // npg
    nblk = npages * PAGE // CMP_STRIDE
    nsbp = smat.shape[1]
    nsb = -(-(npages * PAGE + ts) // SEL_BLOCK)
    nrow = NSA_HEADS * ts
    kern = functools.partial(_cmp_sample_kernel, npg=npg, nsteps=nsteps, ts=ts, pos0=pos0, nblk=nblk, nsb=nsb,
                             nsbp=nsbp)
    pshape = (1,) + cache_t.shape[1:]
    page_spec = lambda k: pl.BlockSpec(pshape, lambda i, s, pt: (pt[i, s * npg + k], 0, 0, 0, 0))
    nxt_spec = pl.BlockSpec(pshape, lambda i, s, pt: (pt[i, jnp.minimum((s + 1) * npg, npages - 1)], 0, 0, 0, 0))
    per_b = lambda shape: pl.BlockSpec((1,) + shape, lambda i, s, pt: (i,) + (0,) * len(shape))
    fixed = lambda a: pl.BlockSpec(a.shape, lambda i, s, pt: (0,) * a.ndim, pipeline_mode=pl.Buffered(1))
    grid_spec = pltpu.PrefetchScalarGridSpec(
        num_scalar_prefetch=1,
        grid=(nb, nsteps),
        in_specs=[page_spec(k) for k in range(npg)] + [
            nxt_spec, per_b((CMP_STRIDE, 2 * half)),
            fixed(w["cmp_w1_cat"]), fixed(w["cmp_bias_rows"]), fixed(w["cmp_w2_bd"]),
            per_b((nrow, half)), fixed(slope_col),
            pl.BlockSpec((1, 1, nrow, 1), lambda i, s, pt: (0, i, 0, 0)), fixed(smat)],
        out_specs=[per_b((nrow, NSA_HD)), per_b((NSA_KV * ts, nsbp))],
        scratch_shapes=[pltpu.VMEM((2, half // LANE, (npg * PAGE // CMP_STRIDE + 8) * CHUNK_PITCH, LANE), F32),
                        pltpu.VMEM((nblk, half), BF16),
                        pltpu.VMEM((nblk, half), BF16)],
    )
    return pl.pallas_call(
        kern,
        grid_spec=grid_spec,
        out_shape=[jax.ShapeDtypeStruct((nb, nrow, NSA_HD), F32), jax.ShapeDtypeStruct((nb, NSA_KV * ts, nsbp), F32)],
        compiler_params=_cparams(2, VMEM_BIG),
        name="cmp_select_sample",
    )(page_table, *([cache_t] * npg), cache_t, new_chunk, w["cmp_w1_cat"], w["cmp_bias_rows"], w["cmp_w2_bd"],
      qbd, slope_col, gates, smat)


def _flash_rows(s, ok, pv, m_ref, l_ref, acc_ref):
    s = jnp.where(ok, s, MASK_NEG)
    m_old = m_ref[...]
    m_new = jnp.maximum(m_old, jnp.max(s, -1, keepdims=True))
    alpha = jnp.exp(m_old - m_new)
    e = jnp.where(ok, jnp.exp(s - m_new), 0.0)
    l_ref[...] = alpha * l_ref[...] + jnp.sum(e, -1, keepdims=True)
    acc_ref[...] = alpha * acc_ref[...] + pv(e.astype(BF16))
    m_ref[...] = m_new


def _slc_sample_kernel(pt_ref, *refs, npg, nsteps, ts, pos0):
    pages = refs[:npg]
    (new_ref, q_ref, slope_ref, gate_ref, sel_ref, selt_ref, prev_ref, o_ref, m_ref, l_ref, acc_ref) = refs[npg:]
    s = pl.program_id(1)
    half = NSA_KV * NSA_HD
    nrow = NSA_HEADS * ts
    bps = PAGE // SEL_BLOCK * npg
    trow = pos0 + lax.broadcasted_iota(jnp.int32, (nrow, 1), 0) % ts

    @pl.when(s == 0)
    def _():
        m_ref[...] = jnp.full(m_ref.shape, MASK_NEG, F32)
        l_ref[...] = jnp.zeros(l_ref.shape, F32)
        acc_ref[...] = jnp.zeros(acc_ref.shape, F32)

    kt = jnp.concatenate([_page_rows(pages[k], 0) for k in range(npg)], axis=1).astype(BF16)
    vt = jnp.concatenate([_page_rows(pages[k], 1) for k in range(npg)], axis=1).astype(BF16)
    nk = npg * PAGE
    kpos = s * nk + lax.broadcasted_iota(jnp.int32, (1, nk), 1)
    expand = (lax.broadcasted_iota(jnp.int32, (bps, nk), 0)
              == lax.broadcasted_iota(jnp.int32, (bps, nk), 1) // SEL_BLOCK).astype(BF16)
    unsel = _dot(sel_ref[0, 0], expand)
    dist_i = trow - kpos
    ok = (dist_i >= 0) & (unsel > -0.5)
    sc = _dot(q_ref[0], kt) - slope_ref[...] * dist_i.astype(F32)
    _flash_rows(sc, ok, lambda e: _dot_nt(e, vt), m_ref, l_ref, acc_ref)

    @pl.when(s == nsteps - 1)
    def _():
        new = new_ref[0]
        kpos_n = nsteps * nk + lax.broadcasted_iota(jnp.int32, (1, PAGE), 1)
        dist_n = trow - kpos_n
        ok_n = (dist_n >= 0) & (selt_ref[0] > -0.5) & (kpos_n < nsteps * nk + SEL_BLOCK)
        sc_n = _dot_nt(q_ref[0], new[:, :half].astype(BF16)) - slope_ref[...] * dist_n.astype(F32)
        _flash_rows(sc_n, ok_n, lambda e: _dot(e, new[:, half:].astype(BF16)), m_ref, l_ref, acc_ref)
        o = _diag_heads(acc_ref[...] / jnp.maximum(l_ref[...], 1e-30), NSA_HPG * ts)
        o_ref[0] = prev_ref[0] + o * gate_ref[0, 0]


def _slc_sample(page_table, cache_t, new_page, qbd, slope_col, gates, sel_steps, sel_tail, prev, npg, ts, pos0):
    nb, npages = page_table.shape
    half = NSA_KV * NSA_HD
    nsteps = npages // npg
    nrow = NSA_HEADS * ts
    bps = sel_steps.shape[-1]
    kern = functools.partial(_slc_sample_kernel, npg=npg, nsteps=nsteps, ts=ts, pos0=pos0)
    pshape = (1,) + cache_t.shape[1:]
    page_spec = lambda k: pl.BlockSpec(pshape, lambda i, s, pt: (pt[i, s * npg + k], 0, 0, 0, 0))
    per_b = lambda shape: pl.BlockSpec((1,) + shape, lambda i, s, pt: (i,) + (0,) * len(shape))
    grid_spec = pltpu.PrefetchScalarGridSpec(
        num_scalar_prefetch=1,
        grid=(nb, nsteps),
        in_specs=[page_spec(k) for k in range(npg)] + [
            per_b((PAGE, 2 * half)), per_b((nrow, half)),
            pl.BlockSpec((nrow, 1), lambda i, s, pt: (0, 0)),
            pl.BlockSpec((1, 1, nrow, 1), lambda i, s, pt: (1, i, 0, 0)),
            pl.BlockSpec((1, 1, nrow, bps), lambda i, s, pt: (i, s, 0, 0)),
            per_b((nrow, 1)), per_b((nrow, NSA_HD))],
        out_specs=per_b((nrow, NSA_HD)),
        scratch_shapes=[pltpu.VMEM((nrow, 1), F32), pltpu.VMEM((nrow, 1), F32), pltpu.VMEM((nrow, half), F32)],
    )
    return pl.pallas_call(
        kern,
        grid_spec=grid_spec,
        out_shape=jax.ShapeDtypeStruct((nb, nrow, NSA_HD), F32),
        compiler_params=_cparams(2),
        name="slc_sample",
    )(page_table, *([cache_t] * npg), new_page, qbd, slope_col, gates, sel_steps, sel_tail, prev)


def _win_sample_kernel(past_ref, new_ref, q_ref, slope_ref, gate_ref, prev_ref, o_ref, *, ts, pos0, wb):
    half = NSA_KV * NSA_HD
    nrow = NSA_HEADS * ts
    trow = pos0 + lax.broadcasted_iota(jnp.int32, (nrow, 1), 0) % ts
    q = q_ref[0]
    new = new_ref[0]
    kt = past_ref[0, 0].reshape(half, wb).astype(BF16)
    vt = past_ref[0, 1].reshape(half, wb).astype(BF16)
    dist_p = trow - (pos0 - wb + lax.broadcasted_iota(jnp.int32, (1, wb), 1))
    dist_n = trow - (pos0 + lax.broadcasted_iota(jnp.int32, (1, ts), 1))
    ok_p = (dist_p >= 0) & (dist_p < WINDOW) & (trow - dist_p >= 0)
    ok_n = (dist_n >= 0) & (dist_n < WINDOW)
    s_p = jnp.where(ok_p, _dot(q, kt) - slope_ref[...] * dist_p.astype(F32), MASK_NEG)
    s_n = jnp.where(ok_n, _dot_nt(q, new[:, :half].astype(BF16)) - slope_ref[...] * dist_n.astype(F32), MASK_NEG)
    m = jnp.maximum(jnp.max(s_p, -1, keepdims=True), jnp.max(s_n, -1, keepdims=True))
    e_p = jnp.where(ok_p, jnp.exp(s_p - m), 0.0)
    e_n = jnp.where(ok_n, jnp.exp(s_n - m), 0.0)
    den = jnp.maximum(jnp.sum(e_p, -1, keepdims=True) + jnp.sum(e_n, -1, keepdims=True), 1e-30)
    o = _dot_nt((e_p / den).astype(BF16), vt) + _dot((e_n / den).astype(BF16), new[:, half:].astype(BF16))
    o_ref[0] = prev_ref[0] + _diag_heads(o, NSA_HPG * ts) * gate_ref[0, 0]


def _win_sample(win_t, new_rows, qbd, slope_col, gates, prev, ts, pos0):
    nb = win_t.shape[0]
    wb = win_t.shape[-1]
    nrow = NSA_HEADS * ts
    half = NSA_KV * NSA_HD
    per_b = lambda shape: pl.BlockSpec((1,) + shape, lambda i: (i,) + (0,) * len(shape))
    return pl.pallas_call(
        functools.partial(_win_sample_kernel, ts=ts, pos0=pos0, wb=wb),
        grid=(nb,),
        in_specs=[per_b(win_t.shape[1:]), per_b((ts, 2 * half)), per_b((nrow, half)),
                  pl.BlockSpec((nrow, 1), lambda i: (0, 0)),
                  pl.BlockSpec((1, 1, nrow, 1), lambda i: (2, i, 0, 0)),
                  per_b((nrow, NSA_HD))],
        out_specs=per_b((nrow, NSA_HD)),
        out_shape=jax.ShapeDtypeStruct((nb, nrow, NSA_HD), F32),
        compiler_params=_cparams(1),
        name="win_sample",
    )(win_t, new_rows, qbd, slope_col, gates, prev)


def _importance_matrix(nblk, nsb):
    rs = SEL_BLOCK // CMP_STRIDE
    rc = CMP_LEN // CMP_STRIDE
    s = np.zeros((nsb, nblk), np.float32)
    for j in range(nsb):
        for n in range(rc):
            for r in range(rs):
                m = rs * j + r - n
                if 0 <= m < nblk:
                    s[j, m] += 1.0
    return s


def _pad_heads(a):
    lead = a.shape[:-1]
    a = a.reshape(lead + (NSA_KV, NSA_HD))
    a = jnp.pad(a, ((0, 0),) * len(lead) + ((0, 0), (0, LANE - NSA_HD)))
    return a.reshape(lead + (NSA_KV * LANE,))


def _prep_weights(w_ffn_in, w_ffn_out, ln_g, ln_b, w_in_even, pool_w, pool_scale, gm_ln_g, gm_ln_b, gm_ws, gm_b,
                  w_out_even, w_in_odd, cmp_pe, cmp_w1, cmp_b1, cmp_w2, w_out_odd, ts, nb_s):
    d = w_in_even.shape[0]
    dff = w_ffn_out.shape[2]
    fp = -(-dff // (2 * LANE)) * (2 * LANE)
    w = {}
    w["ffn_g"] = jnp.pad(w_ffn_in[..., :dff], ((0, 0),) * 3 + ((0, fp - dff),)).astype(BF16)
    w["ffn_u"] = jnp.pad(w_ffn_in[..., dff:], ((0, 0),) * 3 + ((0, fp - dff),)).astype(BF16)
    w["ffn_o"] = jnp.pad(w_ffn_out, ((0, 0), (0, 0), (0, fp - dff), (0, 0))).astype(BF16)
    w["ln_g"] = ln_g.reshape(ln_g.shape[0], ln_g.shape[1], 1, d)
    w["ln_b"] = ln_b.reshape(ln_b.shape[0], ln_b.shape[1], 1, d)
    hw = gm_ws.shape[1]
    w["w_in_even"] = w_in_even.astype(BF16)
    w["pool_w"] = pool_w.astype(BF16)
    w["pool_scale"] = pool_scale.reshape(1, -1)
    w["gm_ln_g"] = gm_ln_g.reshape(1, -1)
    w["gm_ln_b"] = gm_ln_b.reshape(1, -1)
    w["gm_ws"] = jnp.tril(gm_ws).astype(BF16)
    w["gm_b_full"] = jnp.repeat(gm_b.T, hw, axis=1)
    cs = min(ts, GMLP_CHUNK)
    ws_s = jnp.tril(gm_ws[:, :cs, :cs])
    w["gm_ws_kron"] = jnp.einsum("ab,hts->hatbs", jnp.eye(nb_s * (ts // cs), dtype=F32), ws_s).reshape(
        gm_ws.shape[0], nb_s * ts, nb_s * ts).astype(BF16)
    w["gm_b_rows"] = jnp.tile(jnp.repeat(gm_b[:, :cs].T, hw, axis=1), (nb_s * (ts // cs), 1))
    w["w_out_even"] = w_out_even.astype(BF16)
    qw = NSA_HEADS * NSA_HD
    half = NSA_KV * NSA_HD
    kvw = 3 * 2 * half
    ng = w_in_odd.shape[1] - qw - kvw
    wq = w_in_odd[:, :qw]
    wkv = w_in_odd[:, qw:qw + kvw]
    wg = jnp.pad(w_in_odd[:, qw + kvw:], ((0, 0), (0, LANE - ng)))
    w["wq"] = wq.astype(BF16)
    w["wkv"] = wkv.astype(BF16)
    w["wg"] = wg.astype(BF16)
    w["wq_t"] = wq.T.astype(BF16)
    w["wkv_t"] = wkv.T.astype(BF16)
    w["wg_t"] = wg.T.astype(BF16)
    w["wkv_cmp"] = wkv[:, :2 * half].astype(BF16)
    w["wk_pad"] = jnp.concatenate([_pad_heads(wkv[:, 2 * half:3 * half]), _pad_heads(wkv[:, 4 * half:5 * half])],
                                  axis=1).astype(BF16)
    w["w_out_odd"] = w_out_odd.astype(BF16)
    rc = CMP_LEN // CMP_STRIDE
    eye = jnp.eye(NSA_KV, dtype=F32)
    pe = cmp_pe.reshape(2, rc, CMP_STRIDE, 1, NSA_HD)
    w["cmp_pe_rows"] = jnp.broadcast_to(pe, (2, rc, CMP_STRIDE, NSA_KV, NSA_HD)).reshape(2, rc, 1, -1)
    w1 = cmp_w1.reshape(2, rc, CMP_STRIDE, NSA_HD, CMP_HIDDEN)
    w["cmp_w1_bd"] = jnp.einsum("krpdf,gh->krpgdhf", w1, eye).reshape(
        2, rc, CMP_STRIDE * NSA_KV * NSA_HD, NSA_KV * CMP_HIDDEN).astype(BF16)
    w["cmp_b1_rows"] = jnp.tile(cmp_b1, (1, NSA_KV)).reshape(2, 1, -1)
    w["cmp_w1_cat"] = w1.transpose(0, 2, 3, 1, 4).reshape(2, CMP_STRIDE * NSA_HD, rc * CMP_HIDDEN).astype(BF16)
    pe_bias = jnp.einsum("krpd,krpdf->kf", cmp_pe.reshape(2, rc, CMP_STRIDE, NSA_HD), w1,
                         precision=lax.Precision.HIGHEST)
    w["cmp_bias_rows"] = jnp.tile(cmp_b1 + pe_bias, (1, NSA_KV)).reshape(2, 1, -1)
    w2_bd = jnp.einsum("kfd,gh->kgfhd", cmp_w2, eye).reshape(2, NSA_KV * CMP_HIDDEN, half)
    w["cmp_w2_bd"] = w2_bd.astype(BF16)
    w["cmp_w2k_pad"] = _pad_heads(w2_bd[0]).astype(BF16)
    slopes = 2.0 ** (-8.0 * (jnp.arange(NSA_HEADS, dtype=F32) + 1.0) / NSA_HEADS)
    w["slopes"] = slopes
    hi = slopes.astype(BF16).astype(F32)
    mid = (slopes - hi).astype(BF16).astype(F32)
    lo = (slopes - hi - mid).astype(BF16).astype(F32)
    pieces = jnp.stack([hi, mid, lo], axis=1)
    rows = jnp.concatenate([pieces * SEL_BLOCK, pieces,
                            jnp.zeros((NSA_HEADS, KEY_SEL_LANE - KEY_POS_LANE - 6), F32)], axis=1)
    w["slope_rows"] = rows[:, :, None]
    return w


def _ffn(x, w, layer, which, tm):
    return _ffn_ln(x, w["ffn_g"][layer, which], w["ffn_u"][layer, which], w["ffn_o"][layer, which],
                   w["ln_g"][layer, which + (which > 0)], w["ln_b"][layer, which + (which > 0)], tm)


def _rows_from_feature_major(a_t):
    b, _, t = a_t.shape
    return a_t.reshape(b, 2, NSA_KV, NSA_HD, t).transpose(0, 4, 1, 2, 3)


def _trunk_prompt(x3, w):
    b, t, d = x3.shape
    m = b * t
    tm = 256
    tf = 512
    tq = WINDOW
    tk = WINDOW // 2
    x = _ffn(x3.reshape(m, d), w, 0, 0, tf)
    x, pool_state = _mixer0_prompt(x.reshape(b, t, d), w, w["ln_g"][0, 1], w["ln_b"][0, 1], tm)
    x = _ffn(x.reshape(m, d), w, 0, 1, tf)
    x = _ffn(x, w, 1, 0, tf)

    qt, gt, ct, st, wt, kvc, kp, vt = _nsa_proj_prompt(x.reshape(b, t, d), w, tk)
    kc, vct = _compress_prompt(kvc, w)
    smat = jnp.asarray(_importance_matrix(kc.shape[1], -(-t // SEL_BLOCK)))
    o = _nsa_prompt(w["slopes"], w["slope_rows"], qt, gt, kc, vct, kp, vt, smat, tq)

    x = _outproj_ln(x, o.reshape(m, -1), w["w_out_odd"], w["ln_g"][1, 1], w["ln_b"][1, 1], tm)
    x = _ffn(x, w, 1, 1, tf)
    win = _rows_from_feature_major(wt[:, :, t - min(WINDOW, t):])
    return x.reshape(b, t, d), pool_state, _rows_from_feature_major(ct), _rows_from_feature_major(st), win


def _trunk_sample(x3, pool_past, cache_cmp, cache_slc, win_past, page_table, w):
    nb, ts, d = x3.shape
    m = nb * ts
    npages = page_table.shape[1]
    pos0 = npages * PAGE
    npg = 16
    x = _ffn(x3.reshape(m, d), w, 0, 0, m)
    x, pool_state, gm_v = _mixer0_sample(x, pool_past, w, w["ln_g"][0, 1], w["ln_b"][0, 1], nb, ts, pos0)
    x = _ffn(x, w, 0, 1, m)
    x = _ffn(x, w, 1, 0, m)

    q, kvc, kvs, kvw, gates = _nsa_proj(x, w["wq"], w["wkv"], w["wg"], m)
    half = NSA_KV * NSA_HD
    roww = 2 * half
    nrow = NSA_HEADS * ts
    qr = q.reshape(nb, ts, NSA_KV, NSA_HPG, NSA_HD).transpose(0, 2, 3, 1, 4).reshape(nb, NSA_KV, NSA_HPG * ts, NSA_HD)
    qbd = jnp.einsum("bgrd,gk->bgrkd", qr, jnp.eye(NSA_KV, dtype=BF16)).reshape(nb, nrow, half)
    slope_col = jnp.repeat(w["slopes"], ts).reshape(nrow, 1)
    gcols = gates[:, :3 * NSA_HEADS].reshape(nb, ts, 3, NSA_HEADS).transpose(2, 0, 3, 1).reshape(3, nb, nrow, 1)
    feature_major = lambda c: c.transpose(0, 2, 3, 4, 1)

    nblk = npages * PAGE // CMP_STRIDE
    nsb = -(-(pos0 + ts) // SEL_BLOCK)
    nsbp = -(-nsb // LANE) * LANE
    smat = jnp.asarray(np.pad(_importance_matrix(nblk, nsb).T, ((0, 0), (0, nsbp - nsb))))
    new_chunk = jnp.pad(kvc.reshape(nb, ts, roww), ((0, 0), (0, CMP_STRIDE - ts), (0, 0)))
    o, sel = _cmp_sample(page_table, feature_major(cache_cmp), new_chunk, qbd, slope_col, gcols, smat, w, npg, ts,
                         pos0)

    bpp = PAGE // SEL_BLOCK
    nsteps = npages // npg
    sel_rows = jnp.broadcast_to(sel.reshape(nb, NSA_KV, 1, ts, nsbp), (nb, NSA_KV, NSA_HPG, ts, nsbp))
    sel_rows = sel_rows.reshape(nb, nrow, nsbp)
    sel_steps = sel_rows[:, :, :npages * bpp].reshape(nb, nrow, nsteps, npg * bpp).transpose(0, 2, 1, 3).astype(BF16)
    sel_tail = sel_rows[:, :, npages * bpp:npages * bpp + 1]
    new_page = jnp.pad(kvs.reshape(nb, ts, roww), ((0, 0), (0, PAGE - ts), (0, 0)))
    o = _slc_sample(page_table, feature_major(cache_slc), new_page, qbd, slope_col, gcols, sel_steps, sel_tail, o,
                    npg, ts, pos0)
    win_t = feature_major(win_past)
    wb = win_t.shape[-1]
    new_win = kvw.reshape(nb, ts, roww)
    o = _win_sample(win_t, new_win, qbd, slope_col, gcols, o, ts, pos0)

    o = o.reshape(nb, NSA_KV, NSA_HPG, ts, NSA_HD).transpose(0, 3, 1, 2, 4).reshape(m, NSA_HEADS * NSA_HD)
    x = _outproj_ln(x, o, w["w_out_odd"], w["ln_g"][1, 1], w["ln_b"][1, 1], m)
    x = _ffn(x, w, 1, 1, m)
    kv5 = lambda a: a.reshape(nb, ts, 2, NSA_KV, NSA_HD)
    new_win_t = feature_major(kv5(kvw))
    win_buf = jnp.concatenate([win_t, new_win_t], axis=-1)[..., -wb:].transpose(0, 4, 1, 2, 3)
    return x.reshape(nb, ts, d), pool_state, gm_v.reshape(nb, ts, -1), kv5(kvc), kv5(kvs), win_buf


def kernel(x_prompt, x_sample, state_l0_pool, cache_l1_cmp_kv, cache_l1_slc_kv, cache_l1_win_kv, page_table,
           w_ffn_in, w_ffn_out, ln_g, ln_b, w_in_even, pool_w, pool_scale, gm_ln_g, gm_ln_b, gm_ws, gm_b,
           w_out_even, w_in_odd, cmp_pe, cmp_w1, cmp_b1, cmp_w2, w_out_odd):
    w = _prep_weights(w_ffn_in, w_ffn_out, ln_g, ln_b, w_in_even, pool_w, pool_scale, gm_ln_g, gm_ln_b, gm_ws,
                      gm_b, w_out_even, w_in_odd, cmp_pe, cmp_w1, cmp_b1, cmp_w2, w_out_odd,
                      x_sample.shape[1], x_sample.shape[0])
    y_p, pool_p, cmp_p, slc_p, win_p = _trunk_prompt(x_prompt, w)
    y_s, pool_s, gmv_s, cmp_s, slc_s, win_s = _trunk_sample(
        x_sample, state_l0_pool, cache_l1_cmp_kv, cache_l1_slc_kv, cache_l1_win_kv, page_table, w)
    return (y_p, y_s, pool_p, pool_s, gmv_s, cmp_p, slc_p, win_p, cmp_s, slc_s, win_s)
```

```python
import functools

import numpy as np
import jax
import jax.numpy as jnp
from jax import lax
from jax.experimental import pallas as pl
from jax.experimental.pallas import tpu as pltpu

F32 = jnp.float32
BF16 = jnp.bfloat16

DEPTH = 2
DN_ALPHA = (2 * DEPTH) ** 0.25
LN_EPS = 1e-5
POOL_WINDOWS = (2, 4, 8, 16)
POOL_STATE = max(POOL_WINDOWS) - 1
POOL_HALO = 16
GMLP_CHUNK = 128
NSA_HEADS = 16
NSA_KV = 4
NSA_HPG = NSA_HEADS // NSA_KV
NSA_HD = 64
CMP_LEN = 32
CMP_STRIDE = 16
CMP_HIDDEN = 128
SEL_BLOCK = 64
N_SEL = 16
WINDOW = 512
PAGE = 128

LANE = 128
MASK_NEG = -1e30
SEL_BIG = 2.0 ** 100
KEY_POS_LANE = NSA_HD
KEY_SEL_LANE = NSA_HD + 16
CHUNK_PITCH = 20
VMEM_BIG = 56 << 20

_NT = (((1,), (1,)), ((), ()))


def _cparams(n_axes, vmem=None, flags=None):
    return pltpu.CompilerParams(dimension_semantics=("arbitrary",) * n_axes, vmem_limit_bytes=vmem, flags=flags)


def _resident(shape):
    nd = len(shape)
    return pl.BlockSpec(shape, lambda *_: (0,) * nd, pipeline_mode=pl.Buffered(1))


def _whole(shape):
    nd = len(shape)
    return pl.BlockSpec(shape, lambda *_: (0,) * nd)


def _dot(a, b):
    return jnp.dot(a, b, preferred_element_type=F32)


def _dot_nt(a, b):
    return lax.dot_general(a, b, _NT, preferred_element_type=F32)


def _ln(x, g, b):
    mu = jnp.mean(x, -1, keepdims=True)
    xc = x - mu
    var = jnp.mean(xc * xc, -1, keepdims=True)
    return xc * lax.rsqrt(var + LN_EPS) * g + b


def _masked_softmax(s, ok, axis):
    s = jnp.where(ok, s, MASK_NEG)
    m = jnp.max(s, axis, keepdims=True)
    e = jnp.where(ok, jnp.exp(s - m), 0.0)
    return e / jnp.maximum(jnp.sum(e, axis, keepdims=True), 1e-30)


def _ffn_kernel(x_ref, wg_ref, wu_ref, wo_ref, g_ref, b_ref, o_ref):
    x = x_ref[...]
    xb = x.astype(BF16)
    gate = _dot(xb, wg_ref[...])
    up = _dot(xb, wu_ref[...])
    act = (gate * jax.nn.sigmoid(gate) * up).astype(BF16)
    y = _dot(act, wo_ref[...])
    o_ref[...] = _ln(DN_ALPHA * x + 0.5 * y, g_ref[...], b_ref[...])


def _ffn_ln(x, wg, wu, wo, g, b, layer, which, ln_idx, tm):
    m, d = x.shape
    fp = wg.shape[-1]
    pick = lambda shape, j: pl.BlockSpec((None, None) + shape, lambda i: (layer, j, 0, 0),
                                         pipeline_mode=pl.Buffered(1))
    return pl.pallas_call(
        _ffn_kernel,
        grid=(m // tm,),
        in_specs=[pl.BlockSpec((tm, d), lambda i: (i, 0)),
                  pick((d, fp), which), pick((d, fp), which), pick((fp, d), which),
                  pick((1, d), ln_idx), pick((1, d), ln_idx)],
        out_specs=pl.BlockSpec((tm, d), lambda i: (i, 0)),
        out_shape=jax.ShapeDtypeStruct((m, d), F32),
        compiler_params=_cparams(1, VMEM_BIG),
        name="ffn_ln",
    )(x, wg, wu, wo, g, b)


def _outproj_kernel(x_ref, a_ref, w_ref, g_ref, b_ref, o_ref):
    y = _dot(a_ref[...].astype(BF16), w_ref[...])
    o_ref[...] = _ln(DN_ALPHA * x_ref[...] + y, g_ref[...], b_ref[...])


def _outproj_ln(x, a, w, g, b, tm):
    m, d = x.shape
    k = a.shape[1]
    return pl.pallas_call(
        _outproj_kernel,
        grid=(m // tm,),
        in_specs=[pl.BlockSpec((tm, d), lambda i: (i, 0)),
                  pl.BlockSpec((tm, k), lambda i: (i, 0)),
                  _resident((k, d)), _resident((1, d)), _resident((1, d))],
        out_specs=pl.BlockSpec((tm, d), lambda i: (i, 0)),
        out_shape=jax.ShapeDtypeStruct((m, d), F32),
        compiler_params=_cparams(1),
        name="outproj_ln",
    )(x, a, w, g, b)


def _pool_groups(read_window, p, cnt, poolw_ref, gw):
    outs = []
    for g, w in enumerate(POOL_WINDOWS):
        lanes = slice(g * gw, (g + 1) * gw)
        tot = p[..., lanes]
        for j in range(1, w):
            tot = tot + read_window(j, lanes)
        d = tot / jnp.minimum(float(w), cnt) - p[..., lanes]
        d2 = d.reshape(-1, gw).astype(BF16)
        outs.append(_dot(d2, poolw_ref[g]))
    return jnp.concatenate(outs, axis=-1)


def _mixer0_prompt_kernel(x_ref, win_ref, poolw_ref, pscale_ref, glng_ref, glnb_ref, ws_ref, gmb_ref,
                          wout_ref, lng_ref, lnb_ref, y_ref, pstate_ref, ext_ref, *, tm, pw, gw, hw):
    t = pl.program_id(1)
    x = x_ref[0]
    z = _dot(x.astype(BF16), win_ref[...])
    p = z[:, :pw]
    u = z[:, pw:pw + 4 * hw]
    v = _ln(z[:, pw + 4 * hw:], glng_ref[...], glnb_ref[...])

    @pl.when(t == 0)
    def _():
        ext_ref[0:POOL_HALO, :] = jnp.zeros((POOL_HALO, pw), F32)

    ext_ref[POOL_HALO:POOL_HALO + tm, :] = p
    cnt = (t * tm + lax.broadcasted_iota(jnp.int32, (tm, 1), 0) + 1).astype(F32)
    a = _pool_groups(lambda j, lanes: ext_ref[POOL_HALO - j:POOL_HALO - j + tm, lanes], p, cnt, poolw_ref, gw)
    a = a * pscale_ref[...]
    tail = ext_ref[tm:tm + POOL_HALO, :]
    pstate_ref[0] = tail
    ext_ref[0:POOL_HALO, :] = tail

    vb = v.astype(BF16)
    rows = []
    for c in range(tm // GMLP_CHUNK):
        r = slice(c * GMLP_CHUNK, (c + 1) * GMLP_CHUNK)
        rows.append(jnp.concatenate(
            [_dot(ws_ref[h], vb[r, h * hw:(h + 1) * hw]) for h in range(4)], axis=-1) + gmb_ref[...])
    gb = u * jnp.concatenate(rows, axis=0)
    y = _dot(a.astype(BF16), wout_ref[0:pw, :]) + _dot(gb.astype(BF16), wout_ref[pw:, :])
    y_ref[0] = _ln(DN_ALPHA * x + y, lng_ref[...], lnb_ref[...])


def _mixer0_prompt(x, w, lng, lnb, tm):
    b, t, d = x.shape
    pw = w["pool_scale"].shape[1]
    gw = pw // len(POOL_WINDOWS)
    hw = w["gm_ws"].shape[1]
    kern = functools.partial(_mixer0_prompt_kernel, tm=tm, pw=pw, gw=gw, hw=hw)
    y, pstate = pl.pallas_call(
        kern,
        grid=(b, t // tm),
        in_specs=[pl.BlockSpec((1, tm, d), lambda i, j: (i, j, 0)),
                  _resident(w["w_in_even"].shape), _resident(w["pool_w"].shape), _resident(w["pool_scale"].shape),
                  _resident(w["gm_ln_g"].shape), _resident(w["gm_ln_b"].shape), _resident(w["gm_ws"].shape),
                  _resident(w["gm_b_full"].shape), _resident(w["w_out_even"].shape),
                  _resident((1, d)), _resident((1, d))],
        out_specs=[pl.BlockSpec((1, tm, d), lambda i, j: (i, j, 0)),
                   pl.BlockSpec((1, POOL_HALO, pw), lambda i, j: (i, 0, 0))],
        out_shape=[jax.ShapeDtypeStruct((b, t, d), F32), jax.ShapeDtypeStruct((b, POOL_HALO, pw), F32)],
        scratch_shapes=[pltpu.VMEM((POOL_HALO + tm, pw), F32)],
        compiler_params=_cparams(2),
        name="mixer0_prompt",
    )(x, w["w_in_even"], w["pool_w"], w["pool_scale"], w["gm_ln_g"], w["gm_ln_b"], w["gm_ws"],
      w["gm_b_full"], w["w_out_even"], lng, lnb)
    return y, pstate[:, POOL_HALO - POOL_STATE:]


def _mixer0_sample_kernel(x_ref, past_ref, win_ref, poolw_ref, pscale_ref, glng_ref, glnb_ref, wk_ref, gmb_ref,
                          wout_ref, lng_ref, lnb_ref, y_ref, pstate_ref, gmv_ref, ext_ref,
                          *, nb, ts, pos0, pw, gw, hw):
    x = x_ref[...]
    z = _dot(x.astype(BF16), win_ref[...])
    p = z[:, :pw]
    u = z[:, pw:pw + 4 * hw]
    v = _ln(z[:, pw + 4 * hw:], glng_ref[...], glnb_ref[...])
    gmv_ref[...] = v

    p3 = p.reshape(nb, ts, pw)
    ext_ref[:, 0:POOL_HALO, :] = past_ref[...]
    ext_ref[:, POOL_HALO:POOL_HALO + ts, :] = p3
    cnt = (pos0 + lax.broadcasted_iota(jnp.int32, (1, ts, 1), 1) + 1).astype(F32)
    a = _pool_groups(lambda j, lanes: ext_ref[:, POOL_HALO - j:POOL_HALO - j + ts, lanes], p3, cnt, poolw_ref, gw)
    a = a * pscale_ref[...]
    pstate_ref[...] = ext_ref[:, ts:ts + POOL_HALO, :]

    vb = v.astype(BF16)
    mix = jnp.concatenate([_dot(wk_ref[h], vb[:, h * hw:(h + 1) * hw]) for h in range(4)], axis=-1) + gmb_ref[...]
    gb = u * mix
    y = _dot(a.astype(BF16), wout_ref[0:pw, :]) + _dot(gb.astype(BF16), wout_ref[pw:, :])
    y_ref[...] = _ln(DN_ALPHA * x + y, lng_ref[...], lnb_ref[...])


def _mixer0_sample(x, past, w, lng, lnb, nb, ts, pos0):
    m, d = x.shape
    pw = w["pool_scale"].shape[1]
    gw = pw // len(POOL_WINDOWS)
    hw = w["gm_ws"].shape[1]
    kern = functools.partial(_mixer0_sample_kernel, nb=nb, ts=ts, pos0=pos0, pw=pw, gw=gw, hw=hw)
    past16 = jnp.pad(past, ((0, 0), (POOL_HALO - POOL_STATE, 0), (0, 0)))
    args = (x, past16, w["w_in_even"], w["pool_w"], w["pool_scale"], w["gm_ln_g"], w["gm_ln_b"],
            w["gm_ws_kron"], w["gm_b_rows"], w["w_out_even"], lng, lnb)
    y, pstate, gmv = pl.pallas_call(
        kern,
        grid=(1,),
        in_specs=[_resident(a.shape) for a in args],
        out_specs=[_whole((m, d)), _whole((nb, POOL_HALO, pw)), _whole((m, 4 * hw))],
        out_shape=[jax.ShapeDtypeStruct((m, d), F32), jax.ShapeDtypeStruct((nb, POOL_HALO, pw), F32),
                   jax.ShapeDtypeStruct((m, 4 * hw), F32)],
        scratch_shapes=[pltpu.VMEM((nb, POOL_HALO + ts, pw), F32)],
        compiler_params=_cparams(1),
        name="mixer0_sample",
    )(*args)
    return y, pstate[:, POOL_HALO - POOL_STATE:], gmv


def _nsa_proj_kernel(x_ref, wq_ref, wkv_ref, wg_ref, q_ref, kc_ref, ks_ref, kw_ref, g_ref, *, kvw):
    xb = x_ref[...].astype(BF16)
    q_ref[...] = (_dot(xb, wq_ref[...]) * (NSA_HD ** -0.5)).astype(BF16)
    kv = _dot(xb, wkv_ref[...])
    kc_ref[...] = kv[:, 0:kvw]
    ks_ref[...] = kv[:, kvw:2 * kvw]
    kw_ref[...] = kv[:, 2 * kvw:3 * kvw]
    g_ref[...] = jax.nn.sigmoid(_dot(xb, wg_ref[...]))


def _nsa_proj(x, wq, wkv, wg, tm):
    m, d = x.shape
    qw = wq.shape[1]
    kvw = wkv.shape[1] // 3
    row = lambda n: pl.BlockSpec((tm, n), lambda i: (i, 0))
    return pl.pallas_call(
        functools.partial(_nsa_proj_kernel, kvw=kvw),
        grid=(m // tm,),
        in_specs=[row(d), _resident(wq.shape), _resident(wkv.shape), _resident(wg.shape)],
        out_specs=[row(qw), row(kvw), row(kvw), row(kvw), row(LANE)],
        out_shape=[jax.ShapeDtypeStruct((m, qw), BF16)] + [jax.ShapeDtypeStruct((m, kvw), F32)] * 3
        + [jax.ShapeDtypeStruct((m, LANE), F32)],
        compiler_params=_cparams(1),
        name="nsa_proj",
    )(x, wq, wkv, wg)


def _nsa_proj_prompt_kernel(x_ref, wqt_ref, wgt_ref, wkvt_ref, wkp_ref,
                            qt_ref, gt_ref, ct_ref, st_ref, wt_ref, kp_ref, vt_ref, *, kvw):
    xb = x_ref[0].astype(BF16)
    qt_ref[0] = (_dot_nt(wqt_ref[...], xb) * (NSA_HD ** -0.5)).astype(BF16)
    gt_ref[0] = jax.nn.sigmoid(_dot_nt(wgt_ref[...], xb))
    kvt = _dot_nt(wkvt_ref[...], xb)
    ct_ref[0] = kvt[0:kvw]
    st_ref[0] = kvt[kvw:2 * kvw]
    wt_ref[0] = kvt[2 * kvw:3 * kvw]
    half = kvw // 2
    vt_ref[0, 0] = jnp.concatenate([kvt[kvw + half:2 * kvw], kvt[2 * kvw + half:3 * kvw]], axis=0).astype(BF16)
    kp = _dot(xb, wkp_ref[...])
    tm = kp.shape[0]
    sub = lax.broadcasted_iota(jnp.int32, kp.shape, 1) % LANE
    grp = lax.broadcasted_iota(jnp.int32, kp.shape, 1) // LANE
    kabs = pl.program_id(1) * tm + lax.broadcasted_iota(jnp.int32, kp.shape, 0)
    blk = kabs // SEL_BLOCK
    pat = jnp.where((sub >= KEY_POS_LANE) & (sub < KEY_POS_LANE + 3), blk,
                    jnp.where((sub >= KEY_POS_LANE + 3) & (sub < KEY_POS_LANE + 6), kabs % SEL_BLOCK, 0))
    pat = jnp.where((grp < NSA_KV) & (sub - KEY_SEL_LANE == blk), 1, pat)
    kp_ref[0] = (kp + pat.astype(F32)).astype(BF16)


def _nsa_proj_prompt(x, w, tm):
    b, t, d = x.shape
    qw = w["wq_t"].shape[0]
    kvw = w["wkv_t"].shape[0] // 3
    kpw = w["wk_pad"].shape[1]
    feat = lambda n: pl.BlockSpec((1, n, tm), lambda i, j: (i, 0, j))
    tok = lambda n: pl.BlockSpec((1, tm, n), lambda i, j: (i, j, 0))
    return pl.pallas_call(
        functools.partial(_nsa_proj_prompt_kernel, kvw=kvw),
        grid=(b, t // tm),
        in_specs=[tok(d), _resident(w["wq_t"].shape), _resident(w["wg_t"].shape), _resident(w["wkv_t"].shape),
                  _resident(w["wk_pad"].shape)],
        out_specs=[feat(qw), feat(LANE), feat(kvw), feat(kvw), feat(kvw), tok(kpw),
                   pl.BlockSpec((1, 1, kvw, tm), lambda i, j: (i, j, 0, 0))],
        out_shape=[jax.ShapeDtypeStruct((b, qw, t), BF16), jax.ShapeDtypeStruct((b, LANE, t), F32)]
        + [jax.ShapeDtypeStruct((b, kvw, t), F32)] * 3
        + [jax.ShapeDtypeStruct((b, t, kpw), BF16), jax.ShapeDtypeStruct((b, t // tm, kvw, tm), BF16)],
        compiler_params=_cparams(2),
        name="nsa_proj_prompt",
    )(x, w["wq_t"], w["wg_t"], w["wkv_t"], w["wk_pad"])


def _put_chunk(xs_ref, kv, i, val):
    for c in range(val.shape[1] // LANE):
        xs_ref[kv, c, i * CHUNK_PITCH:i * CHUNK_PITCH + CMP_STRIDE, :] = val[:, c * LANE:(c + 1) * LANE]


def _put_slab(xs_ref, kv, first_chunk, slab_t):
    rows = slab_t.T
    for i in range(PAGE // CMP_STRIDE):
        _put_chunk(xs_ref, kv, first_chunk + i, rows[i * CMP_STRIDE:(i + 1) * CMP_STRIDE])


def _pad_chunks(xs_ref, kv, first, last):
    for i in range(first, last):
        _put_chunk(xs_ref, kv, i, jnp.zeros((CMP_STRIDE, xs_ref.shape[1] * LANE), F32))


def _compress_hidden(xs_ref, kv, n, w1_ref, bias_ref):
    nc = n + 8
    low_lanes = lax.broadcasted_iota(jnp.int32, (nc, LANE), 1) < NSA_HD
    heads = []
    for plane in range(xs_ref.shape[1]):
        even, odd = [], []
        for j in range(CMP_STRIDE // 2):
            a = xs_ref[kv, plane, pl.ds(2 * j, nc, stride=CHUNK_PITCH), :]
            b = xs_ref[kv, plane, pl.ds(2 * j + 1, nc, stride=CHUNK_PITCH), :]
            even.append(jnp.where(low_lanes, a, pltpu.roll(b, NSA_HD, 1)))
            odd.append(jnp.where(low_lanes, pltpu.roll(a, NSA_HD, 1), b))
        heads += [jnp.concatenate(v, axis=-1).astype(BF16) for v in (even, odd)]
    r = _dot(jnp.concatenate(heads, axis=0), w1_ref[kv])
    hid = jnp.concatenate(
        [r[g * nc:g * nc + n, :CMP_HIDDEN] + r[g * nc + 1:g * nc + n + 1, CMP_HIDDEN:] for g in range(NSA_KV)],
        axis=-1) + bias_ref[kv]
    return jax.nn.gelu(hid, approximate=True).astype(BF16)


def _compress_prompt_kernel(ct_ref, w1_ref, bias_ref, w2k_ref, w2v_ref, kc_ref, vct_ref, xs_ref, *, n):
    half = NSA_KV * NSA_HD
    cpp = PAGE // CMP_STRIDE
    for kv in range(2):
        for k in range(n // cpp):
            _put_slab(xs_ref, kv, k * cpp, ct_ref[0, kv * half:(kv + 1) * half, k * PAGE:(k + 1) * PAGE])
        _pad_chunks(xs_ref, kv, n, n + 8)
    valid = lax.broadcasted_iota(jnp.int32, (n, 1), 0) < n - 1
    kc = _dot(_compress_hidden(xs_ref, 0, n, w1_ref, bias_ref), w2k_ref[...])
    kc_ref[0] = jnp.where(valid, kc, 0.0).astype(BF16)
    vc = _dot(_compress_hidden(xs_ref, 1, n, w1_ref, bias_ref), w2v_ref[...])
    vct_ref[0] = jnp.where(valid, vc, 0.0).T.astype(BF16)


def _compress_prompt(kv_t, w):
    b, roww, t = kv_t.shape
    n = t // CMP_STRIDE
    half = roww // 2
    return pl.pallas_call(
        functools.partial(_compress_prompt_kernel, n=n),
        grid=(b,),
        in_specs=[pl.BlockSpec((1, roww, t), lambda i: (i, 0, 0)),
                  _resident(w["cmp_w1_cat"].shape), _resident(w["cmp_bias_rows"].shape),
                  _resident(w["cmp_w2k_pad"].shape), _resident(w["cmp_w2_bd"].shape[1:])],
        out_specs=[pl.BlockSpec((1, n, NSA_KV * LANE), lambda i: (i, 0, 0)),
                   pl.BlockSpec((1, half, n), lambda i: (i, 0, 0))],
        out_shape=[jax.ShapeDtypeStruct((b, n, NSA_KV * LANE), BF16), jax.ShapeDtypeStruct((b, half, n), BF16)],
        scratch_shapes=[pltpu.VMEM((2, half // LANE, (n + 8) * CHUNK_PITCH, LANE), F32)],
        compiler_params=_cparams(1),
        name="compress_prompt",
    )(kv_t, w["cmp_w1_cat"], w["cmp_bias_rows"], w["cmp_w2k_pad"], w["cmp_w2_bd"][1])


def _select_topk_cols(score, n_sel):
    nblk = score.shape[0]
    jj = lax.broadcasted_iota(jnp.int32, score.shape, 0)
    rank = jnp.zeros(score.shape, F32)
    for jp in range(nblk):
        row = score[jp:jp + 1, :]
        rank = rank + jnp.where(row > score, 1.0, jnp.where((row == score) & (jj > jp), 1.0, 0.0))
    return rank < float(n_sel)


def _nsa_prompt_kernel(slopes_ref, qt_ref, gt_ref, srow_ref, kc_ref, vct_ref, ks_ref, vst_ref, kw_ref, vwt_ref,
                       smat_ref, o_ref, m_ref, l_ref, acc_ref, osum_ref, *, tq, tk, nblk, nsb):
    g = pl.program_id(1)
    t = pl.program_id(2)
    nh = NSA_HPG
    hd = NSA_HD
    slopes = [slopes_ref[g * nh + h] for h in range(nh)]
    gate = lambda br, h: gt_ref[0, pl.ds(br * NSA_HEADS + g * nh + h, 1), :]
    qt = qt_ref[0]
    tcol = t * tq + lax.broadcasted_iota(jnp.int32, (1, tq), 1)

    def stack_queries(extra_rows):
        return jnp.concatenate(
            [jnp.concatenate([qt[h * hd:(h + 1) * hd], extra_rows(h)], axis=0) for h in range(nh)], axis=1)

    sc = _dot(kc_ref[0], stack_queries(lambda h: jnp.zeros((LANE - hd, tq), BF16)))
    cend = lax.broadcasted_iota(jnp.int32, (nblk, 1), 0) * CMP_STRIDE + (CMP_LEN - 1)
    dist_i = tcol - cend
    ok = dist_i >= 0
    dist = dist_i.astype(F32)
    pc = None
    for h in range(nh):
        p = _masked_softmax(sc[:, h * tq:(h + 1) * tq] - slopes[h] * dist, ok, 0)
        osum_ref[h * hd:(h + 1) * hd, :] = _dot(vct_ref[0], p.astype(BF16)) * gate(0, h)
        pc = p if pc is None else pc + p
    imp = jnp.dot(smat_ref[...], pc, precision=lax.Precision.HIGHEST, preferred_element_type=F32)
    jt = tcol // SEL_BLOCK
    jblk = lax.broadcasted_iota(jnp.int32, (nsb, tq), 0)
    forced = (jblk == 0) | (jblk == jt) | (jblk == jt - 1)
    score = jnp.where(forced, jnp.inf, jnp.where(jblk <= jt, imp, -jnp.inf))
    unsel = jnp.where(_select_topk_cols(score, min(N_SEL, nsb)), 0.0, -SEL_BIG).astype(BF16)
    pad_rows = jnp.zeros((LANE - KEY_SEL_LANE - nsb, tq), BF16)
    q4 = stack_queries(lambda h: jnp.concatenate(
        [jnp.broadcast_to(srow_ref[h], (KEY_SEL_LANE - hd, tq)).astype(BF16), unsel, pad_rows], axis=0))

    def reset():
        m_ref[...] = jnp.full(m_ref.shape, MASK_NEG, F32)
        l_ref[...] = jnp.zeros(l_ref.shape, F32)
        acc_ref[...] = jnp.zeros(acc_ref.shape, F32)

    def tile_local(k_ref, vt_ref, kt, mask):
        off = pl.multiple_of(kt * tk, tk)
        s = _dot(k_ref[0, pl.ds(off, tk), :], q4)
        if mask is not None:
            d_i = (t * tq - kt * tk + lax.broadcasted_iota(jnp.int32, (1, tq), 1)
                   - lax.broadcasted_iota(jnp.int32, (tk, 1), 0))
            okm = d_i >= 0 if mask == "causal" else d_i < WINDOW
            s = jnp.where(jnp.concatenate([okm] * nh, axis=1), s, MASK_NEG)
        m_t = jnp.max(s, 0, keepdims=True)
        e = jnp.exp(s - m_t)
        l_t = jnp.sum(e, 0, keepdims=True)
        eb = e.astype(BF16)
        vt = vt_ref[0, kt]
        return m_t, l_t, [_dot(vt, eb[:, h * tq:(h + 1) * tq]) for h in range(nh)]

    def merge(piece):
        m_t, l_t, pv = piece
        m_old = m_ref[...]
        m_new = jnp.maximum(m_old, m_t)
        a_old = jnp.exp(m_old - m_new)
        a_t = jnp.where(m_t > MASK_NEG, jnp.exp(m_t - m_new), 0.0)
        l_ref[...] = a_old * l_ref[...] + a_t * l_t
        m_ref[...] = m_new
        for h in range(nh):
            rows = slice(h * hd, (h + 1) * hd)
            cols = slice(h * tq, (h + 1) * tq)
            acc_ref[rows, :] = a_old[:, cols] * acc_ref[rows, :] + a_t[:, cols] * pv[h]

    def tiles(k_ref, vt_ref, kt0, masks):
        pieces = [tile_local(k_ref, vt_ref, kt0 + i, mk) for i, mk in enumerate(masks)]
        for piece in pieces:
            merge(piece)

    def finish(br):
        for h in range(nh):
            rows = slice(h * hd, (h + 1) * hd)
            o = acc_ref[rows, :] / jnp.maximum(l_ref[:, h * tq:(h + 1) * tq], 1e-30)
            osum_ref[rows, :] = osum_ref[rows, :] + o * gate(br, h)

    nd = tq // tk
    kd = t * nd
    reset()
    tiles(ks_ref, vst_ref, kd, ["causal"] * nd)

    def slc_body(i, c):
        tiles(ks_ref, vst_ref, i * nd, [None] * nd)
        return c

    lax.fori_loop(0, t, slc_body, 0)
    finish(1)
    reset()
    tiles(kw_ref, vwt_ref, kd, ["causal"] * nd)
    nwin = WINDOW // tk

    @pl.when(kd >= nwin)
    def _():
        tiles(kw_ref, vwt_ref, kd - nwin, ["window"] * nwin)

    finish(2)
    o_ref[0] = osum_ref[...].T


def _nsa_prompt(slopes, slope_rows, qt, gt, kc, vct, kp, vt, smat, tq):
    b, qw, t = qt.shape
    nblk = kc.shape[1]
    nsb = smat.shape[0]
    tk = vt.shape[-1]
    ntile = vt.shape[1]
    gw = NSA_HPG * NSA_HD
    assert tq % tk == 0 and WINDOW % tk == 0 and tq >= WINDOW
    assert KEY_SEL_LANE + nsb <= LANE and nsb <= 256
    kern = functools.partial(_nsa_prompt_kernel, tq=tq, tk=tk, nblk=nblk, nsb=nsb)
    keys = lambda br: pl.BlockSpec((1, t, LANE), lambda i, g, j: (i, 0, br * NSA_KV + g))
    vals = lambda br: pl.BlockSpec((1, ntile, NSA_HD, tk), lambda i, g, j: (i, 0, br * NSA_KV + g, 0))
    return pl.pallas_call(
        kern,
        grid=(b, NSA_KV, t // tq),
        in_specs=[pl.BlockSpec(memory_space=pltpu.SMEM),
                  pl.BlockSpec((1, gw, tq), lambda i, g, j: (i, g, j)),
                  pl.BlockSpec((1, LANE, tq), lambda i, g, j: (i, 0, j)),
                  pl.BlockSpec((NSA_HPG,) + slope_rows.shape[1:], lambda i, g, j: (g, 0, 0)),
                  pl.BlockSpec((1, nblk, LANE), lambda i, g, j: (i, 0, g)),
                  pl.BlockSpec((1, NSA_HD, nblk), lambda i, g, j: (i, g, 0)),
                  keys(0), vals(0), keys(1), vals(1),
                  pl.BlockSpec(smat.shape, lambda i, g, j: (0, 0))],
        out_specs=pl.BlockSpec((1, tq, gw), lambda i, g, j: (i, j, g)),
        out_shape=jax.ShapeDtypeStruct((b, t, qw), F32),
        scratch_shapes=[pltpu.VMEM((1, NSA_HPG * tq), F32), pltpu.VMEM((1, NSA_HPG * tq), F32),
                        pltpu.VMEM((gw, tq), F32), pltpu.VMEM((gw, tq), F32)],
        compiler_params=_cparams(3),
        name="nsa_prompt",
    )(slopes, qt, gt, slope_rows, kc, vct, kp, vt, kp, vt, smat)


def _diag_heads(o, rows_per_g):
    return jnp.concatenate(
        [o[g * rows_per_g:(g + 1) * rows_per_g, g * NSA_HD:(g + 1) * NSA_HD] for g in range(NSA_KV)], axis=0)


def _select_topk_rows(score, n_sel):
    jj = lax.broadcasted_iota(jnp.int32, score.shape, 1)
    big = jnp.int32(score.shape[1])
    taken = jnp.zeros(score.shape, jnp.bool_)
    for _ in range(n_sel):
        cur = jnp.where(taken, -jnp.inf, score)
        m = jnp.max(cur, -1, keepdims=True)
        cand = jnp.where(jnp.logical_not(taken) & (cur >= m), jj, big)
        pick = jnp.min(cand, -1, keepdims=True)
        taken = taken | (jj == pick)
    return taken


def _page_rows(page_ref, kv):
    return page_ref[0, kv].reshape(NSA_KV * NSA_HD, PAGE)


def _cmp_sample_kernel(pt_ref, *refs, npg, nsteps, ts, pos0, nblk, nsb, nsbp):
    pages = refs[:npg]
    (nxt_ref, new_ref, w1_ref, bias_ref, w2_ref, q_ref, slope_ref, gate_ref, smat_ref,
     o_ref, sel_ref, xs_ref, kc_ref, vc_ref) = refs[npg:]
    s = pl.program_id(1)
    half = NSA_KV * NSA_HD
    n = npg * PAGE // CMP_STRIDE
    last = s == nsteps - 1
    cpp = PAGE // CMP_STRIDE
    for kv in range(2):
        for k in range(npg):
            _put_slab(xs_ref, kv, k * cpp, _page_rows(pages[k], kv))
        nxt = _page_rows(nxt_ref, kv).T[0:CMP_STRIDE]
        new = new_ref[0][:, kv * half:(kv + 1) * half]
        _put_chunk(xs_ref, kv, n, jnp.where(last, new, nxt))
        _pad_chunks(xs_ref, kv, n + 1, n + 8)

    off = pl.multiple_of(s * n, n)
    for kv, c_ref in ((0, kc_ref), (1, vc_ref)):
        c = _dot(_compress_hidden(xs_ref, kv, n, w1_ref, bias_ref), w2_ref[kv])
        c_ref[pl.ds(off, n), :] = c.astype(BF16)

    @pl.when(last)
    def _():
        nrow = NSA_HEADS * ts
        trow = pos0 + lax.broadcasted_iota(jnp.int32, (nrow, 1), 0) % ts
        cend = lax.broadcasted_iota(jnp.int32, (1, nblk), 1) * CMP_STRIDE + (CMP_LEN - 1)
        dist_i = trow - cend
        sc = _dot_nt(q_ref[0], kc_ref[...]) - slope_ref[...] * dist_i.astype(F32)
        p = _masked_softmax(sc, dist_i >= 0, -1)
        o = _diag_heads(_dot(p.astype(BF16), vc_ref[...]), NSA_HPG * ts)
        o_ref[0] = o * gate_ref[0, 0]
        p4 = p.reshape(NSA_KV, NSA_HPG, ts, nblk)
        pc = p4[:, 0]
        for h in range(1, NSA_HPG):
            pc = pc + p4[:, h]
        imp = jnp.dot(pc.reshape(NSA_KV * ts, nblk), smat_ref[...], precision=lax.Precision.HIGHEST,
                      preferred_element_type=F32)
        tsel = pos0 + lax.broadcasted_iota(jnp.int32, (NSA_KV * ts, 1), 0) % ts
        jt = tsel // SEL_BLOCK
        jblk = lax.broadcasted_iota(jnp.int32, (NSA_KV * ts, nsbp), 1)
        forced = (jblk == 0) | (jblk == jt) | (jblk == jt - 1)
        score = jnp.where(forced, jnp.inf, jnp.where(jblk <= jt, imp, -jnp.inf))
        score = jnp.where(jblk < nsb, score, -jnp.inf)
        sel_ref[0] = jnp.where(_select_topk_rows(score, min(N_SEL, nsb)), 0.0, -1.0)


def _cmp_sample(page_table, cache_t, new_chunk, qbd, slope_col, gates, smat, w, npg, ts, pos0):
    nb, npages = page_table.shape
    half = NSA_KV * NSA_HD
    nsteps = npages // npg
    nblk = npages * PAGE // CMP_STRIDE
    nsbp = smat.shape[1]
    nsb = -(-(npages * PAGE + ts) // SEL_BLOCK)
    nrow = NSA_HEADS * ts
    kern = functools.partial(_cmp_sample_kernel, npg=npg, nsteps=nsteps, ts=ts, pos0=pos0, nblk=nblk, nsb=nsb,
                             nsbp=nsbp)
    pshape = (1,) + cache_t.shape[1:]
    page_spec = lambda k: pl.BlockSpec(pshape, lambda i, s, pt: (pt[i, s * npg + k], 0, 0, 0, 0))
    nxt_spec = pl.BlockSpec(pshape, lambda i, s, pt: (pt[i, jnp.minimum((s + 1) * npg, npages - 1)], 0, 0, 0, 0))
    per_b = lambda shape: pl.BlockSpec((1,) + shape, lambda i, s, pt: (i,) + (0,) * len(shape))
    fixed = lambda a: pl.BlockSpec(a.shape, lambda i, s, pt: (0,) * a.ndim, pipeline_mode=pl.Buffered(1))
    grid_spec = pltpu.PrefetchScalarGridSpec(
        num_scalar_prefetch=1,
        grid=(nb, nsteps),
        in_specs=[page_spec(k) for k in range(npg)] + [
            nxt_spec, per_b((CMP_STRIDE, 2 * half)),
            fixed(w["cmp_w1_cat"]), fixed(w["cmp_bias_rows"]), fixed(w["cmp_w2_bd"]),
            per_b((nrow, half)), fixed(slope_col),
            pl.BlockSpec((1, 1, nrow, 1), lambda i, s, pt: (0, i, 0, 0)), fixed(smat)],
        out_specs=[per_b((nrow, NSA_HD)), per_b((NSA_KV * ts, nsbp))],
        scratch_shapes=[pltpu.VMEM((2, half // LANE, (npg * PAGE // CMP_STRIDE + 8) * CHUNK_PITCH, LANE), F32),
                        pltpu.VMEM((nblk, half), BF16),
                        pltpu.VMEM((nblk, half), BF16)],
    )
    return pl.pallas_call(
        kern,
        grid_spec=grid_spec,
        out_shape=[jax.ShapeDtypeStruct((nb, nrow, NSA_HD), F32), jax.ShapeDtypeStruct((nb, NSA_KV * ts, nsbp), F32)],
        compiler_params=_cparams(2, VMEM_BIG),
        name="cmp_select_sample",
    )(page_table, *([cache_t] * npg), cache_t, new_chunk, w["cmp_w1_cat"], w["cmp_bias_rows"], w["cmp_w2_bd"],
      qbd, slope_col, gates, smat)


def _flash_rows(s, ok, pv, m_ref, l_ref, acc_ref):
    s = jnp.where(ok, s, MASK_NEG)
    m_old = m_ref[...]
    m_new = jnp.maximum(m_old, jnp.max(s, -1, keepdims=True))
    alpha = jnp.exp(m_old - m_new)
    e = jnp.where(ok, jnp.exp(s - m_new), 0.0)
    l_ref[...] = alpha * l_ref[...] + jnp.sum(e, -1, keepdims=True)
    acc_ref[...] = alpha * acc_ref[...] + pv(e.astype(BF16))
    m_ref[...] = m_new


def _slc_sample_kernel(pt_ref, *refs, npg, nsteps, ts, pos0):
    pages = refs[:npg]
    (new_ref, q_ref, slope_ref, gate_ref, auxq_ref, auxk_ref, selt_ref, prev_ref, o_ref, m_ref, l_ref,
     acc_ref) = refs[npg:]
    s = pl.program_id(1)
    half = NSA_KV * NSA_HD
    nrow = NSA_HEADS * ts
    trow = pos0 + lax.broadcasted_iota(jnp.int32, (nrow, 1), 0) % ts

    @pl.when(s == 0)
    def _():
        m_ref[...] = jnp.full(m_ref.shape, MASK_NEG, F32)
        l_ref[...] = jnp.zeros(l_ref.shape, F32)
        acc_ref[...] = jnp.zeros(acc_ref.shape, F32)

    kt = jnp.concatenate([_page_rows(pages[k], 0) for k in range(npg)], axis=1).astype(BF16)
    vt = jnp.concatenate([_page_rows(pages[k], 1) for k in range(npg)], axis=1).astype(BF16)
    nk = npg * PAGE
    s_all = _dot(q_ref[0], kt) + _dot(auxq_ref[0, 0], auxk_ref[...])
    m_old = m_ref[...]
    m_new = jnp.maximum(m_old, jnp.max(s_all, -1, keepdims=True))
    alpha = jnp.exp(m_old - m_new)
    e = jnp.exp(s_all - m_new)
    l_ref[...] = alpha * l_ref[...] + jnp.sum(e, -1, keepdims=True)
    acc_ref[...] = alpha * acc_ref[...] + _dot_nt(e.astype(BF16), vt)
    m_ref[...] = m_new

    @pl.when(s == nsteps - 1)
    def _():
        new = new_ref[0]
        kpos_n = nsteps * nk + lax.broadcasted_iota(jnp.int32, (1, PAGE), 1)
        dist_n = trow - kpos_n
        ok_n = (dist_n >= 0) & (selt_ref[0] > -0.5) & (kpos_n < nsteps * nk + SEL_BLOCK)
        sc_n = (_dot_nt(q_ref[0], new[:, :half].astype(BF16))
                - slope_ref[...] * (dist_n - (trow - pos0)).astype(F32))
        _flash_rows(sc_n, ok_n, lambda e: _dot(e, new[:, half:].astype(BF16)), m_ref, l_ref, acc_ref)
        o = _diag_heads(acc_ref[...] / jnp.maximum(l_ref[...], 1e-30), NSA_HPG * ts)
        o_ref[0] = prev_ref[0] + o * gate_ref[0, 0]


def _slc_sample(page_table, cache_t, new_page, qbd, slope_col, gates, aux_q, aux_k, sel_tail, prev, npg, ts, pos0):
    nb, npages = page_table.shape
    half = NSA_KV * NSA_HD
    nsteps = npages // npg
    nrow = NSA_HEADS * ts
    naux = aux_k.shape[0]
    kern = functools.partial(_slc_sample_kernel, npg=npg, nsteps=nsteps, ts=ts, pos0=pos0)
    pshape = (1,) + cache_t.shape[1:]
    page_spec = lambda k: pl.BlockSpec(pshape, lambda i, s, pt: (pt[i, s * npg + k], 0, 0, 0, 0))
    per_b = lambda shape: pl.BlockSpec((1,) + shape, lambda i, s, pt: (i,) + (0,) * len(shape))
    grid_spec = pltpu.PrefetchScalarGridSpec(
        num_scalar_prefetch=1,
        grid=(nb, nsteps),
        in_specs=[page_spec(k) for k in range(npg)] + [
            per_b((PAGE, 2 * half)), per_b((nrow, half)),
            pl.BlockSpec((nrow, 1), lambda i, s, pt: (0, 0)),
            pl.BlockSpec((1, 1, nrow, 1), lambda i, s, pt: (1, i, 0, 0)),
            pl.BlockSpec((1, 1, nrow, naux), lambda i, s, pt: (i, s, 0, 0)),
            pl.BlockSpec(aux_k.shape, lambda i, s, pt: (0, 0), pipeline_mode=pl.Buffered(1)),
            per_b((nrow, 1)), per_b((nrow, NSA_HD))],
        out_specs=per_b((nrow, NSA_HD)),
        scratch_shapes=[pltpu.VMEM((nrow, 1), F32), pltpu.VMEM((nrow, 1), F32), pltpu.VMEM((nrow, half), F32)],
    )
    return pl.pallas_call(
        kern,
        grid_spec=grid_spec,
        out_shape=jax.ShapeDtypeStruct((nb, nrow, NSA_HD), F32),
        compiler_params=_cparams(2),
        name="slc_sample",
    )(page_table, *([cache_t] * npg), new_page, qbd, slope_col, gates, aux_q, aux_k, sel_tail, prev)


def _win_sample_kernel(past_ref, new_ref, q_ref, slope_ref, gate_ref, prev_ref, o_ref, *, ts, pos0, wb):
    half = NSA_KV * NSA_HD
    nrow = NSA_HEADS * ts
    trow = pos0 + lax.broadcasted_iota(jnp.int32, (nrow, 1), 0) % ts
    q = q_ref[0]
    new = new_ref[0]
    kt = past_ref[0, 0].reshape(half, wb).astype(BF16)
    vt = past_ref[0, 1].reshape(half, wb).astype(BF16)
    dist_p = trow - (pos0 - wb + lax.broadcasted_iota(jnp.int32, (1, wb), 1))
    dist_n = trow - (pos0 + lax.broadcasted_iota(jnp.int32, (1, ts), 1))
    ok_p = (dist_p >= 0) & (dist_p < WINDOW) & (trow - dist_p >= 0)
    ok_n = (dist_n >= 0) & (dist_n < WINDOW)
    s_p = jnp.where(ok_p, _dot(q, kt) - slope_ref[...] * dist_p.astype(F32), MASK_NEG)
    s_n = jnp.where(ok_n, _dot_nt(q, new[:, :half].astype(BF16)) - slope_ref[...] * dist_n.astype(F32), MASK_NEG)
    m = jnp.maximum(jnp.max(s_p, -1, keepdims=True), jnp.max(s_n, -1, keepdims=True))
    e_p = jnp.where(ok_p, jnp.exp(s_p - m), 0.0)
    e_n = jnp.where(ok_n, jnp.exp(s_n - m), 0.0)
    den = jnp.maximum(jnp.sum(e_p, -1, keepdims=True) + jnp.sum(e_n, -1, keepdims=True), 1e-30)
    o = _dot_nt((e_p / den).astype(BF16), vt) + _dot((e_n / den).astype(BF16), new[:, half:].astype(BF16))
    o_ref[0] = prev_ref[0] + _diag_heads(o, NSA_HPG * ts) * gate_ref[0, 0]


def _win_sample(win_t, new_rows, qbd, slope_col, gates, prev, ts, pos0):
    nb = win_t.shape[0]
    wb = win_t.shape[-1]
    nrow = NSA_HEADS * ts
    half = NSA_KV * NSA_HD
    per_b = lambda shape: pl.BlockSpec((1,) + shape, lambda i: (i,) + (0,) * len(shape))
    return pl.pallas_call(
        functools.partial(_win_sample_kernel, ts=ts, pos0=pos0, wb=wb),
        grid=(nb,),
        in_specs=[per_b(win_t.shape[1:]), per_b((ts, 2 * half)), per_b((nrow, half)),
                  pl.BlockSpec((nrow, 1), lambda i: (0, 0)),
                  pl.BlockSpec((1, 1, nrow, 1), lambda i: (2, i, 0, 0)),
                  per_b((nrow, NSA_HD))],
        out_specs=per_b((nrow, NSA_HD)),
        out_shape=jax.ShapeDtypeStruct((nb, nrow, NSA_HD), F32),
        compiler_params=_cparams(1),
        name="win_sample",
    )(win_t, new_rows, qbd, slope_col, gates, prev)


def _importance_matrix(nblk, nsb):
    rs = SEL_BLOCK // CMP_STRIDE
    rc = CMP_LEN // CMP_STRIDE
    s = np.zeros((nsb, nblk), np.float32)
    for j in range(nsb):
        for n in range(rc):
            for r in range(rs):
                m = rs * j + r - n
                if 0 <= m < nblk:
                    s[j, m] += 1.0
    return s


def _pad_heads(a):
    lead = a.shape[:-1]
    a = a.reshape(lead + (NSA_KV, NSA_HD))
    a = jnp.pad(a, ((0, 0),) * len(lead) + ((0, 0), (0, LANE - NSA_HD)))
    return a.reshape(lead + (NSA_KV * LANE,))


def _prep_weights(w_ffn_in, w_ffn_out, ln_g, ln_b, w_in_even, pool_w, pool_scale, gm_ln_g, gm_ln_b, gm_ws, gm_b,
                  w_out_even, w_in_odd, cmp_pe, cmp_w1, cmp_b1, cmp_w2, w_out_odd, ts, nb_s):
    d = w_in_even.shape[0]
    dff = w_ffn_out.shape[2]
    fp = -(-dff // (2 * LANE)) * (2 * LANE)
    w = {}
    w["ffn_g"] = jnp.pad(w_ffn_in[..., :dff], ((0, 0),) * 3 + ((0, fp - dff),)).astype(BF16)
    w["ffn_u"] = jnp.pad(w_ffn_in[..., dff:], ((0, 0),) * 3 + ((0, fp - dff),)).astype(BF16)
    w["ffn_o"] = jnp.pad(w_ffn_out, ((0, 0), (0, 0), (0, fp - dff), (0, 0))).astype(BF16)
    w["ln_g"] = ln_g.reshape(ln_g.shape[0], ln_g.shape[1], 1, d)
    w["ln_b"] = ln_b.reshape(ln_b.shape[0], ln_b.shape[1], 1, d)
    hw = gm_ws.shape[1]
    w["w_in_even"] = w_in_even.astype(BF16)
    w["pool_w"] = pool_w.astype(BF16)
    w["pool_scale"] = pool_scale.reshape(1, -1)
    w["gm_ln_g"] = gm_ln_g.reshape(1, -1)
    w["gm_ln_b"] = gm_ln_b.reshape(1, -1)
    w["gm_ws"] = jnp.tril(gm_ws).astype(BF16)
    w["gm_b_full"] = jnp.repeat(gm_b.T, hw, axis=1)
    cs = min(ts, GMLP_CHUNK)
    ws_s = jnp.tril(gm_ws[:, :cs, :cs])
    w["gm_ws_kron"] = jnp.einsum("ab,hts->hatbs", jnp.eye(nb_s * (ts // cs), dtype=F32), ws_s).reshape(
        gm_ws.shape[0], nb_s * ts, nb_s * ts).astype(BF16)
    w["gm_b_rows"] = jnp.tile(jnp.repeat(gm_b[:, :cs].T, hw, axis=1), (nb_s * (ts // cs), 1))
    w["w_out_even"] = w_out_even.astype(BF16)
    qw = NSA_HEADS * NSA_HD
    half = NSA_KV * NSA_HD
    kvw = 3 * 2 * half
    ng = w_in_odd.shape[1] - qw - kvw
    wq = w_in_odd[:, :qw]
    wkv = w_in_odd[:, qw:qw + kvw]
    wg = jnp.pad(w_in_odd[:, qw + kvw:], ((0, 0), (0, LANE - ng)))
    w["wq"] = wq.astype(BF16)
    w["wkv"] = wkv.astype(BF16)
    w["wg"] = wg.astype(BF16)
    w["wq_t"] = wq.T.astype(BF16)
    w["wkv_t"] = wkv.T.astype(BF16)
    w["wg_t"] = wg.T.astype(BF16)
    w["wk_pad"] = jnp.concatenate([_pad_heads(wkv[:, 2 * half:3 * half]), _pad_heads(wkv[:, 4 * half:5 * half])],
                                  axis=1).astype(BF16)
    w["w_out_odd"] = w_out_odd.astype(BF16)
    rc = CMP_LEN // CMP_STRIDE
    eye = jnp.eye(NSA_KV, dtype=F32)
    w1 = cmp_w1.reshape(2, rc, CMP_STRIDE, NSA_HD, CMP_HIDDEN)
    w["cmp_w1_cat"] = w1.transpose(0, 2, 3, 1, 4).reshape(2, CMP_STRIDE * NSA_HD, rc * CMP_HIDDEN).astype(BF16)
    pe_bias = jnp.einsum("krpd,krpdf->kf", cmp_pe.reshape(2, rc, CMP_STRIDE, NSA_HD), w1,
                         precision=lax.Precision.HIGHEST)
    w["cmp_bias_rows"] = jnp.tile(cmp_b1 + pe_bias, (1, NSA_KV)).reshape(2, 1, -1)
    w2_bd = jnp.einsum("kfd,gh->kgfhd", cmp_w2, eye).reshape(2, NSA_KV * CMP_HIDDEN, half)
    w["cmp_w2_bd"] = w2_bd.astype(BF16)
    w["cmp_w2k_pad"] = _pad_heads(w2_bd[0]).astype(BF16)
    slopes = 2.0 ** (-8.0 * (jnp.arange(NSA_HEADS, dtype=F32) + 1.0) / NSA_HEADS)
    w["slopes"] = slopes
    hi = slopes.astype(BF16).astype(F32)
    mid = (slopes - hi).astype(BF16).astype(F32)
    lo = (slopes - hi - mid).astype(BF16).astype(F32)
    pieces = jnp.stack([hi, mid, lo], axis=1)
    rows = jnp.concatenate([pieces * SEL_BLOCK, pieces,
                            jnp.zeros((NSA_HEADS, KEY_SEL_LANE - KEY_POS_LANE - 6), F32)], axis=1)
    w["slope_rows"] = rows[:, :, None]
    return w


def _ffn(x, w, layer, which, tm):
    return _ffn_ln(x, w["ffn_g"], w["ffn_u"], w["ffn_o"], w["ln_g"], w["ln_b"], layer, which,
                   which + (which > 0), tm)


def _rows_from_feature_major(a_t):
    b, _, t = a_t.shape
    return a_t.reshape(b, 2, NSA_KV, NSA_HD, t).transpose(0, 4, 1, 2, 3)


def _trunk_prompt(x3, w):
    b, t, d = x3.shape
    m = b * t
    tm = 256
    tf = 512
    tq = WINDOW
    tk = WINDOW // 2
    x = _ffn(x3.reshape(m, d), w, 0, 0, tf)
    x, pool_state = _mixer0_prompt(x.reshape(b, t, d), w, w["ln_g"][0, 1], w["ln_b"][0, 1], tm)
    x = _ffn(x.reshape(m, d), w, 0, 1, tf)
    x = _ffn(x, w, 1, 0, tf)

    qt, gt, ct, st, wt, kp, vt = _nsa_proj_prompt(x.reshape(b, t, d), w, tk)
    kc, vct = _compress_prompt(ct, w)
    smat = jnp.asarray(_importance_matrix(kc.shape[1], -(-t // SEL_BLOCK)))
    o = _nsa_prompt(w["slopes"], w["slope_rows"], qt, gt, kc, vct, kp, vt, smat, tq)

    x = _outproj_ln(x, o.reshape(m, -1), w["w_out_odd"], w["ln_g"][1, 1], w["ln_b"][1, 1], tm)
    x = _ffn(x, w, 1, 1, tf)
    win = _rows_from_feature_major(wt[:, :, t - min(WINDOW, t):])
    return x.reshape(b, t, d), pool_state, _rows_from_feature_major(ct), _rows_from_feature_major(st), win


def _split3(x):
    hi = x.astype(BF16).astype(F32)
    mid = (x - hi).astype(BF16).astype(F32)
    return hi, mid, (x - hi - mid).astype(BF16).astype(F32)


def _slc_aux_operands(sel_steps, slope_rows, slope_col, npg, ts, pos0):
    nb, nsteps, nrow, bps = sel_steps.shape
    nk = npg * PAGE
    naux = -(-(bps + 9) // 16) * 16
    pieces = jnp.repeat(slope_rows[:, :6, 0], ts, axis=0)
    start = pos0 - jnp.arange(nsteps, dtype=F32) * nk
    offs = jnp.stack(_split3(-slope_col[None, :, 0] * start[:, None]), axis=-1)
    aux_q = jnp.concatenate([
        sel_steps * SEL_BIG,
        jnp.broadcast_to(pieces, (nb, nsteps, nrow, 6)),
        jnp.broadcast_to(offs[None], (nb, nsteps, nrow, 3)),
        jnp.zeros((nb, nsteps, nrow, naux - bps - 9), F32)], axis=-1).astype(BF16)
    j = np.arange(nk)
    aux_k = np.zeros((naux, nk), np.float32)
    aux_k[j // SEL_BLOCK, j] = 1.0
    aux_k[bps:bps + 3] = j // SEL_BLOCK
    aux_k[bps + 3:bps + 6] = j % SEL_BLOCK
    aux_k[bps + 6:bps + 9] = 1.0
    return aux_q, jnp.asarray(aux_k, dtype=BF16)


def _trunk_sample(x3, pool_past, cache_cmp, cache_slc, win_past, page_table, w):
    nb, ts, d = x3.shape
    m = nb * ts
    npages = page_table.shape[1]
    pos0 = npages * PAGE
    npg = 16
    x = _ffn(x3.reshape(m, d), w, 0, 0, m)
    x, pool_state, gm_v = _mixer0_sample(x, pool_past, w, w["ln_g"][0, 1], w["ln_b"][0, 1], nb, ts, pos0)
    x = _ffn(x, w, 0, 1, m)
    x = _ffn(x, w, 1, 0, m)

    q, kvc, kvs, kvw, gates = _nsa_proj(x, w["wq"], w["wkv"], w["wg"], m)
    half = NSA_KV * NSA_HD
    roww = 2 * half
    nrow = NSA_HEADS * ts
    qr = q.reshape(nb, ts, NSA_KV, NSA_HPG, NSA_HD).transpose(0, 2, 3, 1, 4).reshape(nb, NSA_KV, NSA_HPG * ts, NSA_HD)
    qbd = jnp.einsum("bgrd,gk->bgrkd", qr, jnp.eye(NSA_KV, dtype=BF16)).reshape(nb, nrow, half)
    slope_col = jnp.repeat(w["slopes"], ts).reshape(nrow, 1)
    gcols = gates[:, :3 * NSA_HEADS].reshape(nb, ts, 3, NSA_HEADS).transpose(2, 0, 3, 1).reshape(3, nb, nrow, 1)
    feature_major = lambda c: c.transpose(0, 2, 3, 4, 1)

    nblk = npages * PAGE // CMP_STRIDE
    nsb = -(-(pos0 + ts) // SEL_BLOCK)
    nsbp = -(-nsb // LANE) * LANE
    smat = jnp.asarray(np.pad(_importance_matrix(nblk, nsb).T, ((0, 0), (0, nsbp - nsb))))
    new_chunk = jnp.pad(kvc.reshape(nb, ts, roww), ((0, 0), (0, CMP_STRIDE - ts), (0, 0)))
    o, sel = _cmp_sample(page_table, feature_major(cache_cmp), new_chunk, qbd, slope_col, gcols, smat, w, npg, ts,
                         pos0)

    bpp = PAGE // SEL_BLOCK
    nsteps = npages // npg
    sel_rows = jnp.broadcast_to(sel.reshape(nb, NSA_KV, 1, ts, nsbp), (nb, NSA_KV, NSA_HPG, ts, nsbp))
    sel_rows = sel_rows.reshape(nb, nrow, nsbp)
    sel_steps = sel_rows[:, :, :npages * bpp].reshape(nb, nrow, nsteps, npg * bpp).transpose(0, 2, 1, 3)
    sel_tail = sel_rows[:, :, npages * bpp:npages * bpp + 1]
    new_page = jnp.pad(kvs.reshape(nb, ts, roww), ((0, 0), (0, PAGE - ts), (0, 0)))
    aux_q, aux_k = _slc_aux_operands(sel_steps, w["slope_rows"], slope_col, npg, ts, pos0)
    o = _slc_sample(page_table, feature_major(cache_slc), new_page, qbd, slope_col, gcols, aux_q, aux_k, sel_tail, o,
                    npg, ts, pos0)
    win_t = feature_major(win_past)
    wb = win_t.shape[-1]
    new_win = kvw.reshape(nb, ts, roww)
    o = _win_sample(win_t, new_win, qbd, slope_col, gcols, o, ts, pos0)

    o = o.reshape(nb, NSA_KV, NSA_HPG, ts, NSA_HD).transpose(0, 3, 1, 2, 4).reshape(m, NSA_HEADS * NSA_HD)
    x = _outproj_ln(x, o, w["w_out_odd"], w["ln_g"][1, 1], w["ln_b"][1, 1], m)
    x = _ffn(x, w, 1, 1, m)
    kv5 = lambda a: a.reshape(nb, ts, 2, NSA_KV, NSA_HD)
    new_win_t = feature_major(kv5(kvw))
    win_buf = jnp.concatenate([win_t, new_win_t], axis=-1)[..., -wb:].transpose(0, 4, 1, 2, 3)
    return x.reshape(nb, ts, d), pool_state, gm_v.reshape(nb, ts, -1), kv5(kvc), kv5(kvs), win_buf


def kernel(x_prompt, x_sample, state_l0_pool, cache_l1_cmp_kv, cache_l1_slc_kv, cache_l1_win_kv, page_table,
           w_ffn_in, w_ffn_out, ln_g, ln_b, w_in_even, pool_w, pool_scale, gm_ln_g, gm_ln_b, gm_ws, gm_b,
           w_out_even, w_in_odd, cmp_pe, cmp_w1, cmp_b1, cmp_w2, w_out_odd):
    w = _prep_weights(w_ffn_in, w_ffn_out, ln_g, ln_b, w_in_even, pool_w, pool_scale, gm_ln_g, gm_ln_b, gm_ws,
                      gm_b, w_out_even, w_in_odd, cmp_pe, cmp_w1, cmp_b1, cmp_w2, w_out_odd,
                      x_sample.shape[1], x_sample.shape[0])
    y_p, pool_p, cmp_p, slc_p, win_p = _trunk_prompt(x_prompt, w)
    y_s, pool_s, gmv_s, cmp_s, slc_s, win_s = _trunk_sample(
        x_sample, state_l0_pool, cache_l1_cmp_kv, cache_l1_slc_kv, cache_l1_win_kv, page_table, w)
    return (y_p, y_s, pool_p, pool_s, gmv_s, cmp_p, slc_p, win_p, cmp_s, slc_s, win_s)
```

```python
import functools

import numpy as np
import jax
import jax.numpy as jnp
from jax import lax
from jax.experimental import pallas as pl
from jax.experimental.pallas import tpu as pltpu

F32 = jnp.float32
BF16 = jnp.bfloat16

DEPTH = 2
DN_ALPHA = (2 * DEPTH) ** 0.25
LN_EPS = 1e-5
POOL_WINDOWS = (2, 4, 8, 16)
POOL_STATE = max(POOL_WINDOWS) - 1
POOL_HALO = 16
GMLP_CHUNK = 128
NSA_HEADS = 16
NSA_KV = 4
NSA_HPG = NSA_HEADS // NSA_KV
NSA_HD = 64
CMP_LEN = 32
CMP_STRIDE = 16
CMP_HIDDEN = 128
SEL_BLOCK = 64
N_SEL = 16
WINDOW = 512
PAGE = 128

LANE = 128
MASK_NEG = -1e30
SEL_BIG = 2.0 ** 100
KEY_POS_LANE = NSA_HD
KEY_SEL_LANE = NSA_HD + 16
CHUNK_PITCH = 20
VMEM_BIG = 56 << 20

_NT = (((1,), (1,)), ((), ()))


def _cparams(n_axes, vmem=None, flags=None):
    return pltpu.CompilerParams(dimension_semantics=("arbitrary",) * n_axes, vmem_limit_bytes=vmem, flags=flags)


def _resident(shape):
    nd = len(shape)
    return pl.BlockSpec(shape, lambda *_: (0,) * nd, pipeline_mode=pl.Buffered(1))


def _whole(shape):
    nd = len(shape)
    return pl.BlockSpec(shape, lambda *_: (0,) * nd)


def _dot(a, b):
    return jnp.dot(a, b, preferred_element_type=F32)


def _dot_nt(a, b):
    return lax.dot_general(a, b, _NT, preferred_element_type=F32)


def _ln(x, g, b):
    mu = jnp.mean(x, -1, keepdims=True)
    xc = x - mu
    var = jnp.mean(xc * xc, -1, keepdims=True)
    return xc * lax.rsqrt(var + LN_EPS) * g + b


def _masked_softmax(s, ok, axis):
    s = jnp.where(ok, s, MASK_NEG)
    m = jnp.max(s, axis, keepdims=True)
    e = jnp.where(ok, jnp.exp(s - m), 0.0)
    return e / jnp.maximum(jnp.sum(e, axis, keepdims=True), 1e-30)


def _ffn_math(x, wg_ref, wu_ref, wo_ref, g_ref, b_ref):
    xb = x.astype(BF16)
    gate = _dot(xb, wg_ref[...])
    up = _dot(xb, wu_ref[...])
    act = (gate * jax.nn.sigmoid(gate) * up).astype(BF16)
    return _ln(DN_ALPHA * x + 0.5 * _dot(act, wo_ref[...]), g_ref[...], b_ref[...])


def _ffn_kernel(x_ref, wg_ref, wu_ref, wo_ref, g_ref, b_ref, o_ref):
    o_ref[...] = _ffn_math(x_ref[...], wg_ref, wu_ref, wo_ref, g_ref, b_ref)


def _ffn_specs(w, layer, which):
    d, fp = w["ffn_g"].shape[-2:]
    ln_idx = which + (which > 0)
    pick = lambda shape, j: pl.BlockSpec((None, None) + shape, lambda *_: (layer, j, 0, 0),
                                         pipeline_mode=pl.Buffered(1))
    specs = [pick((d, fp), which), pick((d, fp), which), pick((fp, d), which), pick((1, d), ln_idx),
             pick((1, d), ln_idx)]
    return specs, (w["ffn_g"], w["ffn_u"], w["ffn_o"], w["ln_g"], w["ln_b"])


def _ffn(x, w, layer, which, tm):
    m, d = x.shape
    specs, ops = _ffn_specs(w, layer, which)
    return pl.pallas_call(
        _ffn_kernel,
        grid=(m // tm,),
        in_specs=[pl.BlockSpec((tm, d), lambda i: (i, 0))] + specs,
        out_specs=pl.BlockSpec((tm, d), lambda i: (i, 0)),
        out_shape=jax.ShapeDtypeStruct((m, d), F32),
        compiler_params=_cparams(1, VMEM_BIG),
        name="ffn_ln",
    )(x, *ops)


def _outproj_ffn_kernel(x_ref, a_ref, w_ref, g_ref, b_ref, wg_ref, wu_ref, wo_ref, g2_ref, b2_ref, o_ref):
    y = _ln(DN_ALPHA * x_ref[...] + _dot(a_ref[...].astype(BF16), w_ref[...]), g_ref[...], b_ref[...])
    o_ref[...] = _ffn_math(y, wg_ref, wu_ref, wo_ref, g2_ref, b2_ref)


def _outproj_ffn(x, a, w_out, w, layer, tm):
    m, d = x.shape
    k = a.shape[1]
    specs, ops = _ffn_specs(w, layer, 1)
    ln = lambda: pl.BlockSpec((None, None, 1, d), lambda i: (layer, 1, 0, 0), pipeline_mode=pl.Buffered(1))
    return pl.pallas_call(
        _outproj_ffn_kernel,
        grid=(m // tm,),
        in_specs=[pl.BlockSpec((tm, d), lambda i: (i, 0)), pl.BlockSpec((tm, k), lambda i: (i, 0)),
                  _resident((k, d)), ln(), ln()] + specs,
        out_specs=pl.BlockSpec((tm, d), lambda i: (i, 0)),
        out_shape=jax.ShapeDtypeStruct((m, d), F32),
        compiler_params=_cparams(1, VMEM_BIG),
        name="outproj_ffn",
    )(x, a, w_out, w["ln_g"], w["ln_b"], *ops)


def _pool_groups(read_window, p, cnt, poolw_ref, gw):
    outs = []
    for g, w in enumerate(POOL_WINDOWS):
        lanes = slice(g * gw, (g + 1) * gw)
        tot = p[..., lanes]
        for j in range(1, w):
            tot = tot + read_window(j, lanes)
        d = tot / jnp.minimum(float(w), cnt) - p[..., lanes]
        d2 = d.reshape(-1, gw).astype(BF16)
        outs.append(_dot(d2, poolw_ref[g]))
    return jnp.concatenate(outs, axis=-1)


def _mixer0_prompt_kernel(x_ref, win_ref, poolw_ref, pscale_ref, glng_ref, glnb_ref, ws_ref, gmb_ref,
                          wout_ref, lng_ref, lnb_ref, wg_ref, wu_ref, wo_ref, g2_ref, b2_ref,
                          y_ref, pstate_ref, ext_ref, *, tm, pw, gw, hw):
    t = pl.program_id(1)
    x = x_ref[0]
    z = _dot(x.astype(BF16), win_ref[...])
    p = z[:, :pw]
    u = z[:, pw:pw + 4 * hw]
    v = _ln(z[:, pw + 4 * hw:], glng_ref[...], glnb_ref[...])

    @pl.when(t == 0)
    def _():
        ext_ref[0:POOL_HALO, :] = jnp.zeros((POOL_HALO, pw), F32)

    ext_ref[POOL_HALO:POOL_HALO + tm, :] = p
    cnt = (t * tm + lax.broadcasted_iota(jnp.int32, (tm, 1), 0) + 1).astype(F32)
    a = _pool_groups(lambda j, lanes: ext_ref[POOL_HALO - j:POOL_HALO - j + tm, lanes], p, cnt, poolw_ref, gw)
    a = a * pscale_ref[...]
    tail = ext_ref[tm:tm + POOL_HALO, :]
    pstate_ref[0] = tail
    ext_ref[0:POOL_HALO, :] = tail

    vb = v.astype(BF16)
    rows = []
    for c in range(tm // GMLP_CHUNK):
        r = slice(c * GMLP_CHUNK, (c + 1) * GMLP_CHUNK)
        rows.append(jnp.concatenate(
            [_dot(ws_ref[h], vb[r, h * hw:(h + 1) * hw]) for h in range(4)], axis=-1) + gmb_ref[...])
    gb = u * jnp.concatenate(rows, axis=0)
    y = _dot(a.astype(BF16), wout_ref[0:pw, :]) + _dot(gb.astype(BF16), wout_ref[pw:, :])
    y = _ln(DN_ALPHA * x + y, lng_ref[...], lnb_ref[...])
    y_ref[0] = _ffn_math(y, wg_ref, wu_ref, wo_ref, g2_ref, b2_ref)


def _mixer0_prompt(x, w, tm):
    b, t, d = x.shape
    pw = w["pool_scale"].shape[1]
    gw = pw // len(POOL_WINDOWS)
    hw = w["gm_ws"].shape[1]
    kern = functools.partial(_mixer0_prompt_kernel, tm=tm, pw=pw, gw=gw, hw=hw)
    ffn_specs, ffn_ops = _ffn_specs(w, 0, 1)
    ln = lambda: pl.BlockSpec((None, None, 1, d), lambda i, j: (0, 1, 0, 0), pipeline_mode=pl.Buffered(1))
    y, pstate = pl.pallas_call(
        kern,
        grid=(b, t // tm),
        in_specs=[pl.BlockSpec((1, tm, d), lambda i, j: (i, j, 0)),
                  _resident(w["w_in_even"].shape), _resident(w["pool_w"].shape), _resident(w["pool_scale"].shape),
                  _resident(w["gm_ln_g"].shape), _resident(w["gm_ln_b"].shape), _resident(w["gm_ws"].shape),
                  _resident(w["gm_b_full"].shape), _resident(w["w_out_even"].shape), ln(), ln()] + ffn_specs,
        out_specs=[pl.BlockSpec((1, tm, d), lambda i, j: (i, j, 0)),
                   pl.BlockSpec((1, POOL_HALO, pw), lambda i, j: (i, 0, 0))],
        out_shape=[jax.ShapeDtypeStruct((b, t, d), F32), jax.ShapeDtypeStruct((b, POOL_HALO, pw), F32)],
        scratch_shapes=[pltpu.VMEM((POOL_HALO + tm, pw), F32)],
        compiler_params=_cparams(2, VMEM_BIG),
        name="mixer0_ffn_prompt",
    )(x, w["w_in_even"], w["pool_w"], w["pool_scale"], w["gm_ln_g"], w["gm_ln_b"], w["gm_ws"],
      w["gm_b_full"], w["w_out_even"], w["ln_g"], w["ln_b"], *ffn_ops)
    return y, pstate[:, POOL_HALO - POOL_STATE:]


def _mixer0_sample_kernel(x_ref, past_ref, win_ref, poolw_ref, pscale_ref, glng_ref, glnb_ref, wk_ref, gmb_ref,
                          wout_ref, lng_ref, lnb_ref, y_ref, pstate_ref, gmv_ref, ext_ref,
                          *, nb, ts, pos0, pw, gw, hw):
    x = x_ref[...]
    z = _dot(x.astype(BF16), win_ref[...])
    p = z[:, :pw]
    u = z[:, pw:pw + 4 * hw]
    v = _ln(z[:, pw + 4 * hw:], glng_ref[...], glnb_ref[...])
    gmv_ref[...] = v

    p3 = p.reshape(nb, ts, pw)
    ext_ref[:, 0:POOL_HALO, :] = past_ref[...]
    ext_ref[:, POOL_HALO:POOL_HALO + ts, :] = p3
    cnt = (pos0 + lax.broadcasted_iota(jnp.int32, (1, ts, 1), 1) + 1).astype(F32)
    a = _pool_groups(lambda j, lanes: ext_ref[:, POOL_HALO - j:POOL_HALO - j + ts, lanes], p3, cnt, poolw_ref, gw)
    a = a * pscale_ref[...]
    pstate_ref[...] = ext_ref[:, ts:ts + POOL_HALO, :]

    vb = v.astype(BF16)
    mix = jnp.concatenate([_dot(wk_ref[h], vb[:, h * hw:(h + 1) * hw]) for h in range(4)], axis=-1) + gmb_ref[...]
    gb = u * mix
    y = _dot(a.astype(BF16), wout_ref[0:pw, :]) + _dot(gb.astype(BF16), wout_ref[pw:, :])
    y_ref[...] = _ln(DN_ALPHA * x + y, lng_ref[...], lnb_ref[...])


def _mixer0_sample(x, past, w, lng, lnb, nb, ts, pos0):
    m, d = x.shape
    pw = w["pool_scale"].shape[1]
    gw = pw // len(POOL_WINDOWS)
    hw = w["gm_ws"].shape[1]
    kern = functools.partial(_mixer0_sample_kernel, nb=nb, ts=ts, pos0=pos0, pw=pw, gw=gw, hw=hw)
    past16 = jnp.pad(past, ((0, 0), (POOL_HALO - POOL_STATE, 0), (0, 0)))
    args = (x, past16, w["w_in_even"], w["pool_w"], w["pool_scale"], w["gm_ln_g"], w["gm_ln_b"],
            w["gm_ws_kron"], w["gm_b_rows"], w["w_out_even"], lng, lnb)
    y, pstate, gmv = pl.pallas_call(
        kern,
        grid=(1,),
        in_specs=[_resident(a.shape) for a in args],
        out_specs=[_whole((m, d)), _whole((nb, POOL_HALO, pw)), _whole((m, 4 * hw))],
        out_shape=[jax.ShapeDtypeStruct((m, d), F32), jax.ShapeDtypeStruct((nb, POOL_HALO, pw), F32),
                   jax.ShapeDtypeStruct((m, 4 * hw), F32)],
        scratch_shapes=[pltpu.VMEM((nb, POOL_HALO + ts, pw), F32)],
        compiler_params=_cparams(1),
        name="mixer0_sample",
    )(*args)
    return y, pstate[:, POOL_HALO - POOL_STATE:], gmv


def _nsa_proj_kernel(x_ref, wq_ref, wkv_ref, wg_ref, q_ref, kc_ref, ks_ref, kw_ref, g_ref, *, kvw):
    xb = x_ref[...].astype(BF16)
    q_ref[...] = (_dot(xb, wq_ref[...]) * (NSA_HD ** -0.5)).astype(BF16)
    kv = _dot(xb, wkv_ref[...])
    kc_ref[...] = kv[:, 0:kvw]
    ks_ref[...] = kv[:, kvw:2 * kvw]
    kw_ref[...] = kv[:, 2 * kvw:3 * kvw]
    g_ref[...] = jax.nn.sigmoid(_dot(xb, wg_ref[...]))


def _nsa_proj(x, wq, wkv, wg, tm):
    m, d = x.shape
    qw = wq.shape[1]
    kvw = wkv.shape[1] // 3
    row = lambda n: pl.BlockSpec((tm, n), lambda i: (i, 0))
    return pl.pallas_call(
        functools.partial(_nsa_proj_kernel, kvw=kvw),
        grid=(m // tm,),
        in_specs=[row(d), _resident(wq.shape), _resident(wkv.shape), _resident(wg.shape)],
        out_specs=[row(qw), row(kvw), row(kvw), row(kvw), row(LANE)],
        out_shape=[jax.ShapeDtypeStruct((m, qw), BF16)] + [jax.ShapeDtypeStruct((m, kvw), F32)] * 3
        + [jax.ShapeDtypeStruct((m, LANE), F32)],
        compiler_params=_cparams(1),
        name="nsa_proj",
    )(x, wq, wkv, wg)


def _nsa_proj_prompt_kernel(x_ref, wqt_ref, wgt_ref, wkvt_ref, wkp_ref,
                            qt_ref, gt_ref, ct_ref, st_ref, wt_ref, kp_ref, vt_ref, *, kvw):
    xb = x_ref[0].astype(BF16)
    qt_ref[0] = (_dot_nt(wqt_ref[...], xb) * (NSA_HD ** -0.5)).astype(BF16)
    gt_ref[0] = jax.nn.sigmoid(_dot_nt(wgt_ref[...], xb))
    kvt = _dot_nt(wkvt_ref[...], xb)
    ct_ref[0] = kvt[0:kvw]
    st_ref[0] = kvt[kvw:2 * kvw]
    wt_ref[0] = kvt[2 * kvw:3 * kvw]
    half = kvw // 2
    vt_ref[0, 0] = jnp.concatenate([kvt[kvw + half:2 * kvw], kvt[2 * kvw + half:3 * kvw]], axis=0).astype(BF16)
    kp = _dot(xb, wkp_ref[...])
    tm = kp.shape[0]
    sub = lax.broadcasted_iota(jnp.int32, kp.shape, 1) % LANE
    grp = lax.broadcasted_iota(jnp.int32, kp.shape, 1) // LANE
    kabs = pl.program_id(1) * tm + lax.broadcasted_iota(jnp.int32, kp.shape, 0)
    blk = kabs // SEL_BLOCK
    pat = jnp.where((sub >= KEY_POS_LANE) & (sub < KEY_POS_LANE + 3), blk,
                    jnp.where((sub >= KEY_POS_LANE + 3) & (sub < KEY_POS_LANE + 6), kabs % SEL_BLOCK, 0))
    pat = jnp.where((grp < NSA_KV) & (sub - KEY_SEL_LANE == blk), 1, pat)
    kp_ref[0] = (kp + pat.astype(F32)).astype(BF16)


def _nsa_proj_prompt(x, w, tm):
    b, t, d = x.shape
    qw = w["wq_t"].shape[0]
    kvw = w["wkv_t"].shape[0] // 3
    kpw = w["wk_pad"].shape[1]
    feat = lambda n: pl.BlockSpec((1, n, tm), lambda i, j: (i, 0, j))
    tok = lambda n: pl.BlockSpec((1, tm, n), lambda i, j: (i, j, 0))
    return pl.pallas_call(
        functools.partial(_nsa_proj_prompt_kernel, kvw=kvw),
        grid=(b, t // tm),
        in_specs=[tok(d), _resident(w["wq_t"].shape), _resident(w["wg_t"].shape), _resident(w["wkv_t"].shape),
                  _resident(w["wk_pad"].shape)],
        out_specs=[feat(qw), feat(LANE), feat(kvw), feat(kvw), feat(kvw), tok(kpw),
                   pl.BlockSpec((1, 1, kvw, tm), lambda i, j: (i, j, 0, 0))],
        out_shape=[jax.ShapeDtypeStruct((b, qw, t), BF16), jax.ShapeDtypeStruct((b, LANE, t), F32)]
        + [jax.ShapeDtypeStruct((b, kvw, t), F32)] * 3
        + [jax.ShapeDtypeStruct((b, t, kpw), BF16), jax.ShapeDtypeStruct((b, t // tm, kvw, tm), BF16)],
        compiler_params=_cparams(2),
        name="nsa_proj_prompt",
    )(x, w["wq_t"], w["wg_t"], w["wkv_t"], w["wk_pad"])


def _put_chunk(xs_ref, kv, i, val):
    for c in range(val.shape[1] // LANE):
        xs_ref[kv, c, i * CHUNK_PITCH:i * CHUNK_PITCH + CMP_STRIDE, :] = val[:, c * LANE:(c + 1) * LANE]


def _put_slab(xs_ref, kv, first_chunk, slab_t):
    rows = slab_t.T
    for i in range(PAGE // CMP_STRIDE):
        _put_chunk(xs_ref, kv, first_chunk + i, rows[i * CMP_STRIDE:(i + 1) * CMP_STRIDE])


def _pad_chunks(xs_ref, kv, first, last):
    for i in range(first, last):
        _put_chunk(xs_ref, kv, i, jnp.zeros((CMP_STRIDE, xs_ref.shape[1] * LANE), F32))


def _compress_hidden(xs_ref, kv, n, w1_ref, bias_ref):
    nc = n + 8
    low_lanes = lax.broadcasted_iota(jnp.int32, (nc, LANE), 1) < NSA_HD
    heads = []
    for plane in range(xs_ref.shape[1]):
        even, odd = [], []
        for j in range(CMP_STRIDE // 2):
            a = xs_ref[kv, plane, pl.ds(2 * j, nc, stride=CHUNK_PITCH), :]
            b = xs_ref[kv, plane, pl.ds(2 * j + 1, nc, stride=CHUNK_PITCH), :]
            even.append(jnp.where(low_lanes, a, pltpu.roll(b, NSA_HD, 1)))
            odd.append(jnp.where(low_lanes, pltpu.roll(a, NSA_HD, 1), b))
        heads += [jnp.concatenate(v, axis=-1).astype(BF16) for v in (even, odd)]
    r = _dot(jnp.concatenate(heads, axis=0), w1_ref[kv])
    hid = jnp.concatenate(
        [r[g * nc:g * nc + n, :CMP_HIDDEN] + r[g * nc + 1:g * nc + n + 1, CMP_HIDDEN:] for g in range(NSA_KV)],
        axis=-1) + bias_ref[kv]
    return jax.nn.gelu(hid, approximate=True).astype(BF16)


def _compress_prompt_kernel(ct_ref, w1_ref, bias_ref, w2k_ref, w2v_ref, kc_ref, vct_ref, xs_ref, *, n):
    half = NSA_KV * NSA_HD
    cpp = PAGE // CMP_STRIDE
    for kv in range(2):
        for k in range(n // cpp):
            _put_slab(xs_ref, kv, k * cpp, ct_ref[0, kv * half:(kv + 1) * half, k * PAGE:(k + 1) * PAGE])
        _pad_chunks(xs_ref, kv, n, n + 8)
    valid = lax.broadcasted_iota(jnp.int32, (n, 1), 0) < n - 1
    kc = _dot(_compress_hidden(xs_ref, 0, n, w1_ref, bias_ref), w2k_ref[...])
    kc_ref[0] = jnp.where(valid, kc, 0.0).astype(BF16)
    vc = _dot(_compress_hidden(xs_ref, 1, n, w1_ref, bias_ref), w2v_ref[...])
    vct_ref[0] = jnp.where(valid, vc, 0.0).T.astype(BF16)


def _compress_prompt(kv_t, w):
    b, roww, t = kv_t.shape
    n = t // CMP_STRIDE
    half = roww // 2
    return pl.pallas_call(
        functools.partial(_compress_prompt_kernel, n=n),
        grid=(b,),
        in_specs=[pl.BlockSpec((1, roww, t), lambda i: (i, 0, 0)),
                  _resident(w["cmp_w1_cat"].shape), _resident(w["cmp_bias_rows"].shape),
                  _resident(w["cmp_w2k_pad"].shape), _resident(w["cmp_w2_bd"].shape[1:])],
        out_specs=[pl.BlockSpec((1, n, NSA_KV * LANE), lambda i: (i, 0, 0)),
                   pl.BlockSpec((1, half, n), lambda i: (i, 0, 0))],
        out_shape=[jax.ShapeDtypeStruct((b, n, NSA_KV * LANE), BF16), jax.ShapeDtypeStruct((b, half, n), BF16)],
        scratch_shapes=[pltpu.VMEM((2, half // LANE, (n + 8) * CHUNK_PITCH, LANE), F32)],
        compiler_params=_cparams(1),
        name="compress_prompt",
    )(kv_t, w["cmp_w1_cat"], w["cmp_bias_rows"], w["cmp_w2k_pad"], w["cmp_w2_bd"][1])


def _select_topk_cols(score, n_sel):
    nblk = score.shape[0]
    jj = lax.broadcasted_iota(jnp.int32, score.shape, 0)
    rank = jnp.zeros(score.shape, F32)
    for jp in range(nblk):
        row = score[jp:jp + 1, :]
        rank = rank + jnp.where(row > score, 1.0, jnp.where((row == score) & (jj > jp), 1.0, 0.0))
    return rank < float(n_sel)


def _nsa_prompt_kernel(slopes_ref, qt_ref, gt_ref, srow_ref, kc_ref, vct_ref, ks_ref, vst_ref, kw_ref, vwt_ref,
                       smat_ref, o_ref, m_ref, l_ref, acc_ref, osum_ref, *, tq, tk, nblk, nsb):
    g = pl.program_id(1)
    t = pl.program_id(2)
    nh = NSA_HPG
    hd = NSA_HD
    slopes = [slopes_ref[g * nh + h] for h in range(nh)]
    gate = lambda br, h: gt_ref[0, pl.ds(br * NSA_HEADS + g * nh + h, 1), :]
    qt = qt_ref[0]
    tcol = t * tq + lax.broadcasted_iota(jnp.int32, (1, tq), 1)

    def stack_queries(extra_rows):
        return jnp.concatenate(
            [jnp.concatenate([qt[h * hd:(h + 1) * hd], extra_rows(h)], axis=0) for h in range(nh)], axis=1)

    sc = _dot(kc_ref[0], stack_queries(lambda h: jnp.zeros((LANE - hd, tq), BF16)))
    cend = lax.broadcasted_iota(jnp.int32, (nblk, 1), 0) * CMP_STRIDE + (CMP_LEN - 1)
    dist_i = tcol - cend
    ok = dist_i >= 0
    dist = dist_i.astype(F32)
    pc = None
    for h in range(nh):
        p = _masked_softmax(sc[:, h * tq:(h + 1) * tq] - slopes[h] * dist, ok, 0)
        osum_ref[h * hd:(h + 1) * hd, :] = _dot(vct_ref[0], p.astype(BF16)) * gate(0, h)
        pc = p if pc is None else pc + p
    imp = jnp.dot(smat_ref[...], pc, precision=lax.Precision.HIGHEST, preferred_element_type=F32)
    jt = tcol // SEL_BLOCK
    jblk = lax.broadcasted_iota(jnp.int32, (nsb, tq), 0)
    forced = (jblk == 0) | (jblk == jt) | (jblk == jt - 1)
    score = jnp.where(forced, jnp.inf, jnp.where(jblk <= jt, imp, -jnp.inf))
    unsel = jnp.where(_select_topk_cols(score, min(N_SEL, nsb)), 0.0, -SEL_BIG).astype(BF16)
    pad_rows = jnp.zeros((LANE - KEY_SEL_LANE - nsb, tq), BF16)
    q4 = stack_queries(lambda h: jnp.concatenate(
        [jnp.broadcast_to(srow_ref[h], (KEY_SEL_LANE - hd, tq)).astype(BF16), unsel, pad_rows], axis=0))

    def reset():
        m_ref[...] = jnp.full(m_ref.shape, MASK_NEG, F32)
        l_ref[...] = jnp.zeros(l_ref.shape, F32)
        acc_ref[...] = jnp.zeros(acc_ref.shape, F32)

    def tile_local(k_ref, vt_ref, kt, mask):
        off = pl.multiple_of(kt * tk, tk)
        s = _dot(k_ref[0, pl.ds(off, tk), :], q4)
        if mask is not None:
            d_i = (t * tq - kt * tk + lax.broadcasted_iota(jnp.int32, (1, tq), 1)
                   - lax.broadcasted_iota(jnp.int32, (tk, 1), 0))
            okm = d_i >= 0 if mask == "causal" else d_i < WINDOW
            s = jnp.where(jnp.concatenate([okm] * nh, axis=1), s, MASK_NEG)
        m_t = jnp.max(s, 0, keepdims=True)
        eb = jnp.exp(s - m_t).astype(BF16)
        vt = jnp.concatenate([vt_ref[0, kt], jnp.ones((16, tk), BF16)], axis=0)
        pv = [_dot(vt, eb[:, h * tq:(h + 1) * tq]) for h in range(nh)]
        l_t = jnp.concatenate([r[hd:hd + 1] for r in pv], axis=1)
        return m_t, l_t, [r[:hd] for r in pv]

    def merge(piece):
        m_t, l_t, pv = piece
        m_old = m_ref[...]
        m_new = jnp.maximum(m_old, m_t)
        a_old = jnp.exp(m_old - m_new)
        a_t = jnp.where(m_t > MASK_NEG, jnp.exp(m_t - m_new), 0.0)
        l_ref[...] = a_old * l_ref[...] + a_t * l_t
        m_ref[...] = m_new
        for h in range(nh):
            rows = slice(h * hd, (h + 1) * hd)
            cols = slice(h * tq, (h + 1) * tq)
            acc_ref[rows, :] = a_old[:, cols] * acc_ref[rows, :] + a_t[:, cols] * pv[h]

    def tiles(k_ref, vt_ref, kt0, masks):
        pieces = [tile_local(k_ref, vt_ref, kt0 + i, mk) for i, mk in enumerate(masks)]
        for piece in pieces:
            merge(piece)

    def finish(br):
        for h in range(nh):
            rows = slice(h * hd, (h + 1) * hd)
            o = acc_ref[rows, :] / jnp.maximum(l_ref[:, h * tq:(h + 1) * tq], 1e-30)
            osum_ref[rows, :] = osum_ref[rows, :] + o * gate(br, h)

    nd = tq // tk
    kd = t * nd
    reset()
    tiles(ks_ref, vst_ref, kd, ["causal"] * nd)

    def slc_body(i, c):
        tiles(ks_ref, vst_ref, i * nd, [None] * nd)
        return c

    lax.fori_loop(0, t, slc_body, 0)
    finish(1)
    reset()
    tiles(kw_ref, vwt_ref, kd, ["causal"] * nd)
    nwin = WINDOW // tk

    @pl.when(kd >= nwin)
    def _():
        tiles(kw_ref, vwt_ref, kd - nwin, ["window"] * nwin)

    finish(2)
    o_ref[0] = osum_ref[...].T


def _nsa_prompt(slopes, slope_rows, qt, gt, kc, vct, kp, vt, smat, tq):
    b, qw, t = qt.shape
    nblk = kc.shape[1]
    nsb = smat.shape[0]
    tk = vt.shape[-1]
    ntile = vt.shape[1]
    gw = NSA_HPG * NSA_HD
    assert tq % tk == 0 and WINDOW % tk == 0 and tq >= WINDOW
    assert KEY_SEL_LANE + nsb <= LANE and nsb <= 256
    kern = functools.partial(_nsa_prompt_kernel, tq=tq, tk=tk, nblk=nblk, nsb=nsb)
    keys = lambda br: pl.BlockSpec((1, t, LANE), lambda i, g, j: (i, 0, br * NSA_KV + g))
    vals = lambda br: pl.BlockSpec((1, ntile, NSA_HD, tk), lambda i, g, j: (i, 0, br * NSA_KV + g, 0))
    return pl.pallas_call(
        kern,
        grid=(b, NSA_KV, t // tq),
        in_specs=[pl.BlockSpec(memory_space=pltpu.SMEM),
                  pl.BlockSpec((1, gw, tq), lambda i, g, j: (i, g, j)),
                  pl.BlockSpec((1, LANE, tq), lambda i, g, j: (i, 0, j)),
                  pl.BlockSpec((NSA_HPG,) + slope_rows.shape[1:], lambda i, g, j: (g, 0, 0)),
                  pl.BlockSpec((1, nblk, LANE), lambda i, g, j: (i, 0, g)),
                  pl.BlockSpec((1, NSA_HD, nblk), lambda i, g, j: (i, g, 0)),
                  keys(0), vals(0), keys(1), vals(1),
                  pl.BlockSpec(smat.shape, lambda i, g, j: (0, 0))],
        out_specs=pl.BlockSpec((1, tq, gw), lambda i, g, j: (i, j, g)),
        out_shape=jax.ShapeDtypeStruct((b, t, qw), F32),
        scratch_shapes=[pltpu.VMEM((1, NSA_HPG * tq), F32), pltpu.VMEM((1, NSA_HPG * tq), F32),
                        pltpu.VMEM((gw, tq), F32), pltpu.VMEM((gw, tq), F32)],
        compiler_params=_cparams(3),
        name="nsa_prompt",
    )(slopes, qt, gt, slope_rows, kc, vct, kp, vt, kp, vt, smat)


def _diag_heads(o, rows_per_g):
    return jnp.concatenate(
        [o[g * rows_per_g:(g + 1) * rows_per_g, g * NSA_HD:(g + 1) * NSA_HD] for g in range(NSA_KV)], axis=0)


def _select_topk_rows(score, n_sel):
    jj = lax.broadcasted_iota(jnp.int32, score.shape, 1)
    big = jnp.int32(score.shape[1])
    taken = jnp.zeros(score.shape, jnp.bool_)
    for _ in range(n_sel):
        cur = jnp.where(taken, -jnp.inf, score)
        m = jnp.max(cur, -1, keepdims=True)
        cand = jnp.where(jnp.logical_not(taken) & (cur >= m), jj, big)
        pick = jnp.min(cand, -1, keepdims=True)
        taken = taken | (jj == pick)
    return taken


def _page_rows(page_ref, kv):
    return page_ref[0, kv].reshape(NSA_KV * NSA_HD, PAGE)


def _cmp_sample_kernel(pt_ref, *refs, npg, nsteps, ts, pos0, nblk, nsb, nsbp):
    pages = refs[:npg]
    (nxt_ref, new_ref, w1_ref, bias_ref, w2_ref, q_ref, slope_ref, gate_ref, smat_ref,
     o_ref, sel_ref, xs_ref, kc_ref, vc_ref) = refs[npg:]
    s = pl.program_id(1)
    half = NSA_KV * NSA_HD
    n = npg * PAGE // CMP_STRIDE
    last = s == nsteps - 1
    cpp = PAGE // CMP_STRIDE
    for kv in range(2):
        for k in range(npg):
            _put_slab(xs_ref, kv, k * cpp, _page_rows(pages[k], kv))
        nxt = _page_rows(nxt_ref, kv).T[0:CMP_STRIDE]
        new = new_ref[0][:, kv * half:(kv + 1) * half]
        _put_chunk(xs_ref, kv, n, jnp.where(last, new, nxt))
        _pad_chunks(xs_ref, kv, n + 1, n + 8)

    off = pl.multiple_of(s * n, n)
    for kv, c_ref in ((0, kc_ref), (1, vc_ref)):
        c = _dot(_compress_hidden(xs_ref, kv, n, w1_ref, bias_ref), w2_ref[kv])
        c_ref[pl.ds(off, n), :] = c.astype(BF16)

    @pl.when(last)
    def _():
        nrow = NSA_HEADS * ts
        trow = pos0 + lax.broadcasted_iota(jnp.int32, (nrow, 1), 0) % ts
        cend = lax.broadcasted_iota(jnp.int32, (1, nblk), 1) * CMP_STRIDE + (CMP_LEN - 1)
        dist_i = trow - cend
        sc = _dot_nt(q_ref[0], kc_ref[...]) - slope_ref[...] * dist_i.astype(F32)
        p = _masked_softmax(sc, dist_i >= 0, -1)
        o = _diag_heads(_dot(p.astype(BF16), vc_ref[...]), NSA_HPG * ts)
        o_ref[0] = o * gate_ref[0, 0]
        p4 = p.reshape(NSA_KV, NSA_HPG, ts, nblk)
        pc = p4[:, 0]
        for h in range(1, NSA_HPG):
            pc = pc + p4[:, h]
        imp = jnp.dot(pc.reshape(NSA_KV * ts, nblk), smat_ref[...], precision=lax.Precision.HIGHEST,
                      preferred_element_type=F32)
        tsel = pos0 + lax.broadcasted_iota(jnp.int32, (NSA_KV * ts, 1), 0) % ts
        jt = tsel // SEL_BLOCK
        jblk = lax.broadcasted_iota(jnp.int32, (NSA_KV * ts, nsbp), 1)
        forced = (jblk == 0) | (jblk == jt) | (jblk == jt - 1)
        score = jnp.where(forced, jnp.inf, jnp.where(jblk <= jt, imp, -jnp.inf))
        score = jnp.where(jblk < nsb, score, -jnp.inf)
        sel_ref[0] = jnp.where(_select_topk_rows(score, min(N_SEL, nsb)), 0.0, -1.0)


def _cmp_sample(page_table, cache_t, new_chunk, qbd, slope_col, gates, smat, w, npg, ts, pos0):
    nb, npages = page_table.shape
    half = NSA_KV * NSA_HD
    nsteps = npages // npg
    nblk = npages * PAGE // CMP_STRIDE
    nsbp = smat.shape[1]
    nsb = -(-(npages * PAGE + ts) // SEL_BLOCK)
    nrow = NSA_HEADS * ts
    kern = functools.partial(_cmp_sample_kernel, npg=npg, nsteps=nsteps, ts=ts, pos0=pos0, nblk=nblk, nsb=nsb,
                             nsbp=nsbp)
    pshape = (1,) + cache_t.shape[1:]
    page_spec = lambda k: pl.BlockSpec(pshape, lambda i, s, pt: (pt[i, s * npg + k], 0, 0, 0, 0))
    nxt_spec = pl.BlockSpec(pshape, lambda i, s, pt: (pt[i, jnp.minimum((s + 1) * npg, npages - 1)], 0, 0, 0, 0))
    per_b = lambda shape: pl.BlockSpec((1,) + shape, lambda i, s, pt: (i,) + (0,) * len(shape))
    fixed = lambda a: pl.BlockSpec(a.shape, lambda i, s, pt: (0,) * a.ndim, pipeline_mode=pl.Buffered(1))
    grid_spec = pltpu.PrefetchScalarGridSpec(
        num_scalar_prefetch=1,
        grid=(nb, nsteps),
        in_specs=[page_spec(k) for k in range(npg)] + [
            nxt_spec, per_b((CMP_STRIDE, 2 * half)),
            fixed(w["cmp_w1_cat"]), fixed(w["cmp_bias_rows"]), fixed(w["cmp_w2_bd"]),
            per_b((nrow, half)), fixed(slope_col),
            pl.BlockSpec((1, 1, nrow, 1), lambda i, s, pt: (0, i, 0, 0)), fixed(smat)],
        out_specs=[per_b((nrow, NSA_HD)), per_b((NSA_KV * ts, nsbp))],
        scratch_shapes=[pltpu.VMEM((2, half // LANE, (npg * PAGE // CMP_STRIDE + 8) * CHUNK_PITCH, LANE), F32),
                        pltpu.VMEM((nblk, half), BF16),
                        pltpu.VMEM((nblk, half), BF16)],
    )
    return pl.pallas_call(
        kern,
        grid_spec=grid_spec,
        out_shape=[jax.ShapeDtypeStruct((nb, nrow, NSA_HD), F32), jax.ShapeDtypeStruct((nb, NSA_KV * ts, nsbp), F32)],
        compiler_params=_cparams(2, VMEM_BIG),
        name="cmp_select_sample",
    )(page_table, *([cache_t] * npg), cache_t, new_chunk, w["cmp_w1_cat"], w["cmp_bias_rows"], w["cmp_w2_bd"],
      qbd, slope_col, gates, smat)


def _flash_rows(s, ok, pv, m_ref, l_ref, acc_ref):
    s = jnp.where(ok, s, MASK_NEG)
    m_old = m_ref[...]
    m_new = jnp.maximum(m_old, jnp.max(s, -1, keepdims=True))
    alpha = jnp.exp(m_old - m_new)
    e = jnp.where(ok, jnp.exp(s - m_new), 0.0)
    l_ref[...] = alpha * l_ref[...] + jnp.sum(e, -1, keepdims=True)
    acc_ref[...] = alpha * acc_ref[...] + pv(e.astype(BF16))
    m_ref[...] = m_new


def _slc_sample_kernel(pt_ref, *refs, npg, nsteps, ts, pos0):
    pages = refs[:npg]
    (new_ref, q_ref, slope_ref, gate_ref, auxq_ref, auxk_ref, selt_ref, prev_ref, o_ref, m_ref, l_ref,
     acc_ref) = refs[npg:]
    s = pl.program_id(1)
    half = NSA_KV * NSA_HD
    nrow = NSA_HEADS * ts
    trow = pos0 + lax.broadcasted_iota(jnp.int32, (nrow, 1), 0) % ts

    @pl.when(s == 0)
    def _():
        m_ref[...] = jnp.full(m_ref.shape, MASK_NEG, F32)
        l_ref[...] = jnp.zeros(l_ref.shape, F32)
        acc_ref[...] = jnp.zeros(acc_ref.shape, F32)

    kt = jnp.concatenate([_page_rows(pages[k], 0) for k in range(npg)], axis=1).astype(BF16)
    vt = jnp.concatenate([_page_rows(pages[k], 1) for k in range(npg)], axis=1).astype(BF16)
    nk = npg * PAGE
    s_all = _dot(q_ref[0], kt) + _dot(auxq_ref[0, 0], auxk_ref[...])
    m_old = m_ref[...]
    m_new = jnp.maximum(m_old, jnp.max(s_all, -1, keepdims=True))
    alpha = jnp.exp(m_old - m_new)
    e = jnp.exp(s_all - m_new)
    l_ref[...] = alpha * l_ref[...] + jnp.sum(e, -1, keepdims=True)
    acc_ref[...] = alpha * acc_ref[...] + _dot_nt(e.astype(BF16), vt)
    m_ref[...] = m_new

    @pl.when(s == nsteps - 1)
    def _():
        new = new_ref[0]
        kpos_n = nsteps * nk + lax.broadcasted_iota(jnp.int32, (1, PAGE), 1)
        dist_n = trow - kpos_n
        ok_n = (dist_n >= 0) & (selt_ref[0] > -0.5) & (kpos_n < nsteps * nk + SEL_BLOCK)
        sc_n = (_dot_nt(q_ref[0], new[:, :half].astype(BF16))
                - slope_ref[...] * (dist_n - (trow - pos0)).astype(F32))
        _flash_rows(sc_n, ok_n, lambda e: _dot(e, new[:, half:].astype(BF16)), m_ref, l_ref, acc_ref)
        o = _diag_heads(acc_ref[...] / jnp.maximum(l_ref[...], 1e-30), NSA_HPG * ts)
        o_ref[0] = prev_ref[0] + o * gate_ref[0, 0]


def _slc_sample(page_table, cache_t, new_page, qbd, slope_col, gates, aux_q, aux_k, sel_tail, prev, npg, ts, pos0):
    nb, npages = page_table.shape
    half = NSA_KV * NSA_HD
    nsteps = npages // npg
    nrow = NSA_HEADS * ts
    naux = aux_k.shape[0]
    kern = functools.partial(_slc_sample_kernel, npg=npg, nsteps=nsteps, ts=ts, pos0=pos0)
    pshape = (1,) + cache_t.shape[1:]
    page_spec = lambda k: pl.BlockSpec(pshape, lambda i, s, pt: (pt[i, s * npg + k], 0, 0, 0, 0))
    per_b = lambda shape: pl.BlockSpec((1,) + shape, lambda i, s, pt: (i,) + (0,) * len(shape))
    grid_spec = pltpu.PrefetchScalarGridSpec(
        num_scalar_prefetch=1,
        grid=(nb, nsteps),
        in_specs=[page_spec(k) for k in range(npg)] + [
            per_b((PAGE, 2 * half)), per_b((nrow, half)),
            pl.BlockSpec((nrow, 1), lambda i, s, pt: (0, 0)),
            pl.BlockSpec((1, 1, nrow, 1), lambda i, s, pt: (1, i, 0, 0)),
            pl.BlockSpec((1, 1, nrow, naux), lambda i, s, pt: (i, s, 0, 0)),
            pl.BlockSpec(aux_k.shape, lambda i, s, pt: (0, 0), pipeline_mode=pl.Buffered(1)),
            per_b((nrow, 1)), per_b((nrow, NSA_HD))],
        out_specs=per_b((nrow, NSA_HD)),
        scratch_shapes=[pltpu.VMEM((nrow, 1), F32), pltpu.VMEM((nrow, 1), F32), pltpu.VMEM((nrow, half), F32)],
    )
    return pl.pallas_call(
        kern,
        grid_spec=grid_spec,
        out_shape=jax.ShapeDtypeStruct((nb, nrow, NSA_HD), F32),
        compiler_params=_cparams(2),
        name="slc_sample",
    )(page_table, *([cache_t] * npg), new_page, qbd, slope_col, gates, aux_q, aux_k, sel_tail, prev)


def _win_sample_kernel(past_ref, new_ref, q_ref, slope_ref, gate_ref, prev_ref, o_ref, *, ts, pos0, wb):
    half = NSA_KV * NSA_HD
    nrow = NSA_HEADS * ts
    trow = pos0 + lax.broadcasted_iota(jnp.int32, (nrow, 1), 0) % ts
    q = q_ref[0]
    new = new_ref[0]
    kt = past_ref[0, 0].reshape(half, wb).astype(BF16)
    vt = past_ref[0, 1].reshape(half, wb).astype(BF16)
    dist_p = trow - (pos0 - wb + lax.broadcasted_iota(jnp.int32, (1, wb), 1))
    dist_n = trow - (pos0 + lax.broadcasted_iota(jnp.int32, (1, ts), 1))
    ok_p = (dist_p >= 0) & (dist_p < WINDOW) & (trow - dist_p >= 0)
    ok_n = (dist_n >= 0) & (dist_n < WINDOW)
    s_p = jnp.where(ok_p, _dot(q, kt) - slope_ref[...] * dist_p.astype(F32), MASK_NEG)
    s_n = jnp.where(ok_n, _dot_nt(q, new[:, :half].astype(BF16)) - slope_ref[...] * dist_n.astype(F32), MASK_NEG)
    m = jnp.maximum(jnp.max(s_p, -1, keepdims=True), jnp.max(s_n, -1, keepdims=True))
    e_p = jnp.where(ok_p, jnp.exp(s_p - m), 0.0)
    e_n = jnp.where(ok_n, jnp.exp(s_n - m), 0.0)
    den = jnp.maximum(jnp.sum(e_p, -1, keepdims=True) + jnp.sum(e_n, -1, keepdims=True), 1e-30)
    o = _dot_nt((e_p / den).astype(BF16), vt) + _dot((e_n / den).astype(BF16), new[:, half:].astype(BF16))
    o_ref[0] = prev_ref[0] + _diag_heads(o, NSA_HPG * ts) * gate_ref[0, 0]


def _win_sample(win_t, new_rows, qbd, slope_col, gates, prev, ts, pos0):
    nb = win_t.shape[0]
    wb = win_t.shape[-1]
    nrow = NSA_HEADS * ts
    half = NSA_KV * NSA_HD
    per_b = lambda shape: pl.BlockSpec((1,) + shape, lambda i: (i,) + (0,) * len(shape))
    return pl.pallas_call(
        functools.partial(_win_sample_kernel, ts=ts, pos0=pos0, wb=wb),
        grid=(nb,),
        in_specs=[per_b(win_t.shape[1:]), per_b((ts, 2 * half)), per_b((nrow, half)),
                  pl.BlockSpec((nrow, 1), lambda i: (0, 0)),
                  pl.BlockSpec((1, 1, nrow, 1), lambda i: (2, i, 0, 0)),
                  per_b((nrow, NSA_HD))],
        out_specs=per_b((nrow, NSA_HD)),
        out_shape=jax.ShapeDtypeStruct((nb, nrow, NSA_HD), F32),
        compiler_params=_cparams(1),
        name="win_sample",
    )(win_t, new_rows, qbd, slope_col, gates, prev)


def _importance_matrix(nblk, nsb):
    rs = SEL_BLOCK // CMP_STRIDE
    rc = CMP_LEN // CMP_STRIDE
    s = np.zeros((nsb, nblk), np.float32)
    for j in range(nsb):
        for n in range(rc):
            for r in range(rs):
                m = rs * j + r - n
                if 0 <= m < nblk:
                    s[j, m] += 1.0
    return s


def _pad_heads(a):
    lead = a.shape[:-1]
    a = a.reshape(lead + (NSA_KV, NSA_HD))
    a = jnp.pad(a, ((0, 0),) * len(lead) + ((0, 0), (0, LANE - NSA_HD)))
    return a.reshape(lead + (NSA_KV * LANE,))


def _prep_weights(w_ffn_in, w_ffn_out, ln_g, ln_b, w_in_even, pool_w, pool_scale, gm_ln_g, gm_ln_b, gm_ws, gm_b,
                  w_out_even, w_in_odd, cmp_pe, cmp_w1, cmp_b1, cmp_w2, w_out_odd, ts, nb_s):
    d = w_in_even.shape[0]
    dff = w_ffn_out.shape[2]
    fp = -(-dff // (2 * LANE)) * (2 * LANE)
    w = {}
    w["ffn_g"] = jnp.pad(w_ffn_in[..., :dff], ((0, 0),) * 3 + ((0, fp - dff),)).astype(BF16)
    w["ffn_u"] = jnp.pad(w_ffn_in[..., dff:], ((0, 0),) * 3 + ((0, fp - dff),)).astype(BF16)
    w["ffn_o"] = jnp.pad(w_ffn_out, ((0, 0), (0, 0), (0, fp - dff), (0, 0))).astype(BF16)
    w["ln_g"] = ln_g.reshape(ln_g.shape[0], ln_g.shape[1], 1, d)
    w["ln_b"] = ln_b.reshape(ln_b.shape[0], ln_b.shape[1], 1, d)
    hw = gm_ws.shape[1]
    w["w_in_even"] = w_in_even.astype(BF16)
    w["pool_w"] = pool_w.astype(BF16)
    w["pool_scale"] = pool_scale.reshape(1, -1)
    w["gm_ln_g"] = gm_ln_g.reshape(1, -1)
    w["gm_ln_b"] = gm_ln_b.reshape(1, -1)
    w["gm_ws"] = jnp.tril(gm_ws).astype(BF16)
    w["gm_b_full"] = jnp.repeat(gm_b.T, hw, axis=1)
    cs = min(ts, GMLP_CHUNK)
    ws_s = jnp.tril(gm_ws[:, :cs, :cs])
    w["gm_ws_kron"] = jnp.einsum("ab,hts->hatbs", jnp.eye(nb_s * (ts // cs), dtype=F32), ws_s).reshape(
        gm_ws.shape[0], nb_s * ts, nb_s * ts).astype(BF16)
    w["gm_b_rows"] = jnp.tile(jnp.repeat(gm_b[:, :cs].T, hw, axis=1), (nb_s * (ts // cs), 1))
    w["w_out_even"] = w_out_even.astype(BF16)
    qw = NSA_HEADS * NSA_HD
    half = NSA_KV * NSA_HD
    kvw = 3 * 2 * half
    ng = w_in_odd.shape[1] - qw - kvw
    wq = w_in_odd[:, :qw]
    wkv = w_in_odd[:, qw:qw + kvw]
    wg = jnp.pad(w_in_odd[:, qw + kvw:], ((0, 0), (0, LANE - ng)))
    w["wq"] = wq.astype(BF16)
    w["wkv"] = wkv.astype(BF16)
    w["wg"] = wg.astype(BF16)
    w["wq_t"] = wq.T.astype(BF16)
    w["wkv_t"] = wkv.T.astype(BF16)
    w["wg_t"] = wg.T.astype(BF16)
    w["wk_pad"] = jnp.concatenate([_pad_heads(wkv[:, 2 * half:3 * half]), _pad_heads(wkv[:, 4 * half:5 * half])],
                                  axis=1).astype(BF16)
    w["w_out_odd"] = w_out_odd.astype(BF16)
    rc = CMP_LEN // CMP_STRIDE
    eye = jnp.eye(NSA_KV, dtype=F32)
    w1 = cmp_w1.reshape(2, rc, CMP_STRIDE, NSA_HD, CMP_HIDDEN)
    w["cmp_w1_cat"] = w1.transpose(0, 2, 3, 1, 4).reshape(2, CMP_STRIDE * NSA_HD, rc * CMP_HIDDEN).astype(BF16)
    pe_bias = jnp.einsum("krpd,krpdf->kf", cmp_pe.reshape(2, rc, CMP_STRIDE, NSA_HD), w1,
                         precision=lax.Precision.HIGHEST)
    w["cmp_bias_rows"] = jnp.tile(cmp_b1 + pe_bias, (1, NSA_KV)).reshape(2, 1, -1)
    w2_bd = jnp.einsum("kfd,gh->kgfhd", cmp_w2, eye).reshape(2, NSA_KV * CMP_HIDDEN, half)
    w["cmp_w2_bd"] = w2_bd.astype(BF16)
    w["cmp_w2k_pad"] = _pad_heads(w2_bd[0]).astype(BF16)
    slopes = 2.0 ** (-8.0 * (jnp.arange(NSA_HEADS, dtype=F32) + 1.0) / NSA_HEADS)
    w["slopes"] = slopes
    hi = slopes.astype(BF16).astype(F32)
    mid = (slopes - hi).astype(BF16).astype(F32)
    lo = (slopes - hi - mid).astype(BF16).astype(F32)
    pieces = jnp.stack([hi, mid, lo], axis=1)
    rows = jnp.concatenate([pieces * SEL_BLOCK, pieces,
                            jnp.zeros((NSA_HEADS, KEY_SEL_LANE - KEY_POS_LANE - 6), F32)], axis=1)
    w["slope_rows"] = rows[:, :, None]
    return w


def _rows_from_feature_major(a_t):
    b, _, t = a_t.shape
    return a_t.reshape(b, 2, NSA_KV, NSA_HD, t).transpose(0, 4, 1, 2, 3)


def _trunk_prompt(x3, w):
    b, t, d = x3.shape
    m = b * t
    tm = 256
    tf = 512
    tq = WINDOW
    tk = WINDOW // 2
    x = _ffn(x3.reshape(m, d), w, 0, 0, tf)
    x, pool_state = _mixer0_prompt(x.reshape(b, t, d), w, tm)
    x = _ffn(x.reshape(m, d), w, 1, 0, tf)

    qt, gt, ct, st, wt, kp, vt = _nsa_proj_prompt(x.reshape(b, t, d), w, tk)
    kc, vct = _compress_prompt(ct, w)
    smat = jnp.asarray(_importance_matrix(kc.shape[1], -(-t // SEL_BLOCK)))
    o = _nsa_prompt(w["slopes"], w["slope_rows"], qt, gt, kc, vct, kp, vt, smat, tq)

    x = _outproj_ffn(x, o.reshape(m, -1), w["w_out_odd"], w, 1, tf)
    win = _rows_from_feature_major(wt[:, :, t - min(WINDOW, t):])
    return x.reshape(b, t, d), pool_state, _rows_from_feature_major(ct), _rows_from_feature_major(st), win


def _split3(x):
    hi = x.astype(BF16).astype(F32)
    mid = (x - hi).astype(BF16).astype(F32)
    return hi, mid, (x - hi - mid).astype(BF16).astype(F32)


def _slc_aux_operands(sel_steps, slope_rows, slope_col, npg, ts, pos0):
    nb, nsteps, nrow, bps = sel_steps.shape
    nk = npg * PAGE
    naux = -(-(bps + 9) // 16) * 16
    pieces = jnp.repeat(slope_rows[:, :6, 0], ts, axis=0)
    start = pos0 - jnp.arange(nsteps, dtype=F32) * nk
    offs = jnp.stack(_split3(-slope_col[None, :, 0] * start[:, None]), axis=-1)
    aux_q = jnp.concatenate([
        sel_steps * SEL_BIG,
        jnp.broadcast_to(pieces, (nb, nsteps, nrow, 6)),
        jnp.broadcast_to(offs[None], (nb, nsteps, nrow, 3)),
        jnp.zeros((nb, nsteps, nrow, naux - bps - 9), F32)], axis=-1).astype(BF16)
    j = np.arange(nk)
    aux_k = np.zeros((naux, nk), np.float32)
    aux_k[j // SEL_BLOCK, j] = 1.0
    aux_k[bps:bps + 3] = j // SEL_BLOCK
    aux_k[bps + 3:bps + 6] = j % SEL_BLOCK
    aux_k[bps + 6:bps + 9] = 1.0
    return aux_q, jnp.asarray(aux_k, dtype=BF16)


def _trunk_sample(x3, pool_past, cache_cmp, cache_slc, win_past, page_table, w):
    nb, ts, d = x3.shape
    m = nb * ts
    npages = page_table.shape[1]
    pos0 = npages * PAGE
    npg = 16
    x = _ffn(x3.reshape(m, d), w, 0, 0, m)
    x, pool_state, gm_v = _mixer0_sample(x, pool_past, w, w["ln_g"][0, 1], w["ln_b"][0, 1], nb, ts, pos0)
    x = _ffn(x, w, 0, 1, m)
    x = _ffn(x, w, 1, 0, m)

    q, kvc, kvs, kvw, gates = _nsa_proj(x, w["wq"], w["wkv"], w["wg"], m)
    half = NSA_KV * NSA_HD
    roww = 2 * half
    nrow = NSA_HEADS * ts
    qr = q.reshape(nb, ts, NSA_KV, NSA_HPG, NSA_HD).transpose(0, 2, 3, 1, 4).reshape(nb, NSA_KV, NSA_HPG * ts, NSA_HD)
    qbd = jnp.einsum("bgrd,gk->bgrkd", qr, jnp.eye(NSA_KV, dtype=BF16)).reshape(nb, nrow, half)
    slope_col = jnp.repeat(w["slopes"], ts).reshape(nrow, 1)
    gcols = gates[:, :3 * NSA_HEADS].reshape(nb, ts, 3, NSA_HEADS).transpose(2, 0, 3, 1).reshape(3, nb, nrow, 1)
    feature_major = lambda c: c.transpose(0, 2, 3, 4, 1)

    nblk = npages * PAGE // CMP_STRIDE
    nsb = -(-(pos0 + ts) // SEL_BLOCK)
    nsbp = -(-nsb // LANE) * LANE
    smat = jnp.asarray(np.pad(_importance_matrix(nblk, nsb).T, ((0, 0), (0, nsbp - nsb))))
    new_chunk = jnp.pad(kvc.reshape(nb, ts, roww), ((0, 0), (0, CMP_STRIDE - ts), (0, 0)))
    o, sel = _cmp_sample(page_table, feature_major(cache_cmp), new_chunk, qbd, slope_col, gcols, smat, w, npg, ts,
                         pos0)

    bpp = PAGE // SEL_BLOCK
    nsteps = npages // npg
    sel_rows = jnp.broadcast_to(sel.reshape(nb, NSA_KV, 1, ts, nsbp), (nb, NSA_KV, NSA_HPG, ts, nsbp))
    sel_rows = sel_rows.reshape(nb, nrow, nsbp)
    sel_steps = sel_rows[:, :, :npages * bpp].reshape(nb, nrow, nsteps, npg * bpp).transpose(0, 2, 1, 3)
    sel_tail = sel_rows[:, :, npages * bpp:npages * bpp + 1]
    new_page = jnp.pad(kvs.reshape(nb, ts, roww), ((0, 0), (0, PAGE - ts), (0, 0)))
    aux_q, aux_k = _slc_aux_operands(sel_steps, w["slope_rows"], slope_col, npg, ts, pos0)
    o = _slc_sample(page_table, feature_major(cache_slc), new_page, qbd, slope_col, gcols, aux_q, aux_k, sel_tail, o,
                    npg, ts, pos0)
    win_t = feature_major(win_past)
    wb = win_t.shape[-1]
    new_win = kvw.reshape(nb, ts, roww)
    o = _win_sample(win_t, new_win, qbd, slope_col, gcols, o, ts, pos0)

    o = o.reshape(nb, NSA_KV, NSA_HPG, ts, NSA_HD).transpose(0, 3, 1, 2, 4).reshape(m, NSA_HEADS * NSA_HD)
    x = _outproj_ffn(x, o, w["w_out_odd"], w, 1, m)
    kv5 = lambda a: a.reshape(nb, ts, 2, NSA_KV, NSA_HD)
    new_win_t = feature_major(kv5(kvw))
    win_buf = jnp.concatenate([win_t, new_win_t], axis=-1)[..., -wb:].transpose(0, 4, 1, 2, 3)
    return x.reshape(nb, ts, d), pool_state, gm_v.reshape(nb, ts, -1), kv5(kvc), kv5(kvs), win_buf


def kernel(x_prompt, x_sample, state_l0_pool, cache_l1_cmp_kv, cache_l1_slc_kv, cache_l1_win_kv, page_table,
           w_ffn_in, w_ffn_out, ln_g, ln_b, w_in_even, pool_w, pool_scale, gm_ln_g, gm_ln_b, gm_ws, gm_b,
           w_out_even, w_in_odd, cmp_pe, cmp_w1, cmp_b1, cmp_w2, w_out_odd):
    w = _prep_weights(w_ffn_in, w_ffn_out, ln_g, ln_b, w_in_even, pool_w, pool_scale, gm_ln_g, gm_ln_b, gm_ws,
                      gm_b, w_out_even, w_in_odd, cmp_pe, cmp_w1, cmp_b1, cmp_w2, w_out_odd,
                      x_sample.shape[1], x_sample.shape[0])
    y_p, pool_p, cmp_p, slc_p, win_p = _trunk_prompt(x_prompt, w)
    y_s, pool_s, gmv_s, cmp_s, slc_s, win_s = _trunk_sample(
        x_sample, state_l0_pool, cache_l1_cmp_kv, cache_l1_slc_kv, cache_l1_win_kv, page_table, w)
    return (y_p, y_s, pool_p, pool_s, gmv_s, cmp_p, slc_p, win_p, cmp_s, slc_s, win_s)
```

```python
import functools

import numpy as np
import jax
import jax.numpy as jnp
from jax import lax
from jax.experimental import pallas as pl
from jax.experimental.pallas import tpu as pltpu

F32 = jnp.float32
BF16 = jnp.bfloat16

DEPTH = 2
DN_ALPHA = (2 * DEPTH) ** 0.25
LN_EPS = 1e-5
POOL_WINDOWS = (2, 4, 8, 16)
POOL_STATE = max(POOL_WINDOWS) - 1
POOL_HALO = 16
GMLP_CHUNK = 128
NSA_HEADS = 16
NSA_KV = 4
NSA_HPG = NSA_HEADS // NSA_KV
NSA_HD = 64
CMP_LEN = 32
CMP_STRIDE = 16
CMP_HIDDEN = 128
SEL_BLOCK = 64
N_SEL = 16
WINDOW = 512
PAGE = 128

LANE = 128
MASK_NEG = -1e30
SEL_BIG = 2.0 ** 100
KEY_POS_LANE = NSA_HD
KEY_SEL_LANE = NSA_HD + 16
CHUNK_PITCH = 20
VMEM_BIG = 56 << 20

_NT = (((1,), (1,)), ((), ()))


def _cparams(n_axes, vmem=None, flags=None):
    return pltpu.CompilerParams(dimension_semantics=("arbitrary",) * n_axes, vmem_limit_bytes=vmem, flags=flags)


def _resident(shape):
    nd = len(shape)
    return pl.BlockSpec(shape, lambda *_: (0,) * nd, pipeline_mode=pl.Buffered(1))


def _whole(shape):
    nd = len(shape)
    return pl.BlockSpec(shape, lambda *_: (0,) * nd)


def _dot(a, b):
    return jnp.dot(a, b, preferred_element_type=F32)


def _dot_nt(a, b):
    return lax.dot_general(a, b, _NT, preferred_element_type=F32)


def _ln(x, g, b):
    mu = jnp.mean(x, -1, keepdims=True)
    xc = x - mu
    var = jnp.mean(xc * xc, -1, keepdims=True)
    return xc * lax.rsqrt(var + LN_EPS) * g + b


def _masked_softmax(s, ok, axis):
    s = jnp.where(ok, s, MASK_NEG)
    m = jnp.max(s, axis, keepdims=True)
    e = jnp.where(ok, jnp.exp(s - m), 0.0)
    return e / jnp.maximum(jnp.sum(e, axis, keepdims=True), 1e-30)


def _ffn_math(x, wg_ref, wu_ref, wo_ref, g_ref, b_ref):
    xb = x.astype(BF16)
    gate = _dot(xb, wg_ref[...])
    up = _dot(xb, wu_ref[...])
    act = (gate * jax.nn.sigmoid(gate) * up).astype(BF16)
    return _ln(DN_ALPHA * x + 0.5 * _dot(act, wo_ref[...]), g_ref[...], b_ref[...])


def _ffn_kernel(x_ref, wg_ref, wu_ref, wo_ref, g_ref, b_ref, o_ref):
    o_ref[...] = _ffn_math(x_ref[...], wg_ref, wu_ref, wo_ref, g_ref, b_ref)


def _ffn_specs(w, layer, which):
    d, fp = w["ffn_g"].shape[-2:]
    ln_idx = which + (which > 0)
    pick = lambda shape, j: pl.BlockSpec((None, None) + shape, lambda *_: (layer, j, 0, 0),
                                         pipeline_mode=pl.Buffered(1))
    specs = [pick((d, fp), which), pick((d, fp), which), pick((fp, d), which), pick((1, d), ln_idx),
             pick((1, d), ln_idx)]
    return specs, (w["ffn_g"], w["ffn_u"], w["ffn_o"], w["ln_g"], w["ln_b"])


def _ffn(x, w, layer, which, tm):
    m, d = x.shape
    specs, ops = _ffn_specs(w, layer, which)
    return pl.pallas_call(
        _ffn_kernel,
        grid=(m // tm,),
        in_specs=[pl.BlockSpec((tm, d), lambda i: (i, 0))] + specs,
        out_specs=pl.BlockSpec((tm, d), lambda i: (i, 0)),
        out_shape=jax.ShapeDtypeStruct((m, d), F32),
        compiler_params=_cparams(1, VMEM_BIG),
        name="ffn_ln",
    )(x, *ops)


def _outproj_ffn_kernel(x_ref, a_ref, w_ref, g_ref, b_ref, wg_ref, wu_ref, wo_ref, g2_ref, b2_ref, o_ref):
    y = _ln(DN_ALPHA * x_ref[...] + _dot(a_ref[...].astype(BF16), w_ref[...]), g_ref[...], b_ref[...])
    o_ref[...] = _ffn_math(y, wg_ref, wu_ref, wo_ref, g2_ref, b2_ref)


def _outproj_ffn(x, a, w_out, w, layer, tm):
    m, d = x.shape
    k = a.shape[1]
    specs, ops = _ffn_specs(w, layer, 1)
    ln = lambda: pl.BlockSpec((None, None, 1, d), lambda i: (layer, 1, 0, 0), pipeline_mode=pl.Buffered(1))
    return pl.pallas_call(
        _outproj_ffn_kernel,
        grid=(m // tm,),
        in_specs=[pl.BlockSpec((tm, d), lambda i: (i, 0)), pl.BlockSpec((tm, k), lambda i: (i, 0)),
                  _resident((k, d)), ln(), ln()] + specs,
        out_specs=pl.BlockSpec((tm, d), lambda i: (i, 0)),
        out_shape=jax.ShapeDtypeStruct((m, d), F32),
        compiler_params=_cparams(1, VMEM_BIG),
        name="outproj_ffn",
    )(x, a, w_out, w["ln_g"], w["ln_b"], *ops)


def _pool_groups(read_window, p, cnt, poolw_ref, gw):
    outs = []
    for g, w in enumerate(POOL_WINDOWS):
        lanes = slice(g * gw, (g + 1) * gw)
        tot = p[..., lanes]
        for j in range(1, w):
            tot = tot + read_window(j, lanes)
        d = tot / jnp.minimum(float(w), cnt) - p[..., lanes]
        d2 = d.reshape(-1, gw).astype(BF16)
        outs.append(_dot(d2, poolw_ref[g]))
    return jnp.concatenate(outs, axis=-1)


def _mixer0_prompt_kernel(x_ref, win_ref, poolw_ref, pscale_ref, glng_ref, glnb_ref, ws_ref, gmb_ref,
                          wout_ref, lng_ref, lnb_ref, wg_ref, wu_ref, wo_ref, g2_ref, b2_ref,
                          y_ref, pstate_ref, ext_ref, *, tm, pw, gw, hw):
    t = pl.program_id(1)
    x = x_ref[0]
    z = _dot(x.astype(BF16), win_ref[...])
    p = z[:, :pw]
    u = z[:, pw:pw + 4 * hw]
    v = _ln(z[:, pw + 4 * hw:], glng_ref[...], glnb_ref[...])

    @pl.when(t == 0)
    def _():
        ext_ref[0:POOL_HALO, :] = jnp.zeros((POOL_HALO, pw), F32)

    ext_ref[POOL_HALO:POOL_HALO + tm, :] = p
    cnt = (t * tm + lax.broadcasted_iota(jnp.int32, (tm, 1), 0) + 1).astype(F32)
    a = _pool_groups(lambda j, lanes: ext_ref[POOL_HALO - j:POOL_HALO - j + tm, lanes], p, cnt, poolw_ref, gw)
    a = a * pscale_ref[...]
    tail = ext_ref[tm:tm + POOL_HALO, :]
    pstate_ref[0] = tail
    ext_ref[0:POOL_HALO, :] = tail

    vb = v.astype(BF16)
    rows = []
    for c in range(tm // GMLP_CHUNK):
        r = slice(c * GMLP_CHUNK, (c + 1) * GMLP_CHUNK)
        rows.append(jnp.concatenate(
            [_dot(ws_ref[h], vb[r, h * hw:(h + 1) * hw]) for h in range(4)], axis=-1) + gmb_ref[...])
    gb = u * jnp.concatenate(rows, axis=0)
    y = _dot(a.astype(BF16), wout_ref[0:pw, :]) + _dot(gb.astype(BF16), wout_ref[pw:, :])
    y = _ln(DN_ALPHA * x + y, lng_ref[...], lnb_ref[...])
    y_ref[0] = _ffn_math(y, wg_ref, wu_ref, wo_ref, g2_ref, b2_ref)


def _mixer0_prompt(x, w, tm):
    b, t, d = x.shape
    pw = w["pool_scale"].shape[1]
    gw = pw // len(POOL_WINDOWS)
    hw = w["gm_ws"].shape[1]
    kern = functools.partial(_mixer0_prompt_kernel, tm=tm, pw=pw, gw=gw, hw=hw)
    ffn_specs, ffn_ops = _ffn_specs(w, 0, 1)
    ln = lambda: pl.BlockSpec((None, None, 1, d), lambda i, j: (0, 1, 0, 0), pipeline_mode=pl.Buffered(1))
    y, pstate = pl.pallas_call(
        kern,
        grid=(b, t // tm),
        in_specs=[pl.BlockSpec((1, tm, d), lambda i, j: (i, j, 0)),
                  _resident(w["w_in_even"].shape), _resident(w["pool_w"].shape), _resident(w["pool_scale"].shape),
                  _resident(w["gm_ln_g"].shape), _resident(w["gm_ln_b"].shape), _resident(w["gm_ws"].shape),
                  _resident(w["gm_b_full"].shape), _resident(w["w_out_even"].shape), ln(), ln()] + ffn_specs,
        out_specs=[pl.BlockSpec((1, tm, d), lambda i, j: (i, j, 0)),
                   pl.BlockSpec((1, POOL_HALO, pw), lambda i, j: (i, 0, 0))],
        out_shape=[jax.ShapeDtypeStruct((b, t, d), F32), jax.ShapeDtypeStruct((b, POOL_HALO, pw), F32)],
        scratch_shapes=[pltpu.VMEM((POOL_HALO + tm, pw), F32)],
        compiler_params=_cparams(2, VMEM_BIG),
        name="mixer0_ffn_prompt",
    )(x, w["w_in_even"], w["pool_w"], w["pool_scale"], w["gm_ln_g"], w["gm_ln_b"], w["gm_ws"],
      w["gm_b_full"], w["w_out_even"], w["ln_g"], w["ln_b"], *ffn_ops)
    return y, pstate[:, POOL_HALO - POOL_STATE:]


def _mixer0_sample_kernel(x_ref, past_ref, win_ref, poolw_ref, pscale_ref, glng_ref, glnb_ref, wk_ref, gmb_ref,
                          wout_ref, lng_ref, lnb_ref, y_ref, pstate_ref, gmv_ref, ext_ref,
                          *, nb, ts, pos0, pw, gw, hw):
    x = x_ref[...]
    z = _dot(x.astype(BF16), win_ref[...])
    p = z[:, :pw]
    u = z[:, pw:pw + 4 * hw]
    v = _ln(z[:, pw + 4 * hw:], glng_ref[...], glnb_ref[...])
    gmv_ref[...] = v

    p3 = p.reshape(nb, ts, pw)
    ext_ref[:, 0:POOL_HALO, :] = past_ref[...]
    ext_ref[:, POOL_HALO:POOL_HALO + ts, :] = p3
    cnt = (pos0 + lax.broadcasted_iota(jnp.int32, (1, ts, 1), 1) + 1).astype(F32)
    a = _pool_groups(lambda j, lanes: ext_ref[:, POOL_HALO - j:POOL_HALO - j + ts, lanes], p3, cnt, poolw_ref, gw)
    a = a * pscale_ref[...]
    pstate_ref[...] = ext_ref[:, ts:ts + POOL_HALO, :]

    vb = v.astype(BF16)
    mix = jnp.concatenate([_dot(wk_ref[h], vb[:, h * hw:(h + 1) * hw]) for h in range(4)], axis=-1) + gmb_ref[...]
    gb = u * mix
    y = _dot(a.astype(BF16), wout_ref[0:pw, :]) + _dot(gb.astype(BF16), wout_ref[pw:, :])
    y_ref[...] = _ln(DN_ALPHA * x + y, lng_ref[...], lnb_ref[...])


def _mixer0_sample(x, past, w, lng, lnb, nb, ts, pos0):
    m, d = x.shape
    pw = w["pool_scale"].shape[1]
    gw = pw // len(POOL_WINDOWS)
    hw = w["gm_ws"].shape[1]
    kern = functools.partial(_mixer0_sample_kernel, nb=nb, ts=ts, pos0=pos0, pw=pw, gw=gw, hw=hw)
    past16 = jnp.pad(past, ((0, 0), (POOL_HALO - POOL_STATE, 0), (0, 0)))
    args = (x, past16, w["w_in_even"], w["pool_w"], w["pool_scale"], w["gm_ln_g"], w["gm_ln_b"],
            w["gm_ws_kron"], w["gm_b_rows"], w["w_out_even"], lng, lnb)
    y, pstate, gmv = pl.pallas_call(
        kern,
        grid=(1,),
        in_specs=[_resident(a.shape) for a in args],
        out_specs=[_whole((m, d)), _whole((nb, POOL_HALO, pw)), _whole((m, 4 * hw))],
        out_shape=[jax.ShapeDtypeStruct((m, d), F32), jax.ShapeDtypeStruct((nb, POOL_HALO, pw), F32),
                   jax.ShapeDtypeStruct((m, 4 * hw), F32)],
        scratch_shapes=[pltpu.VMEM((nb, POOL_HALO + ts, pw), F32)],
        compiler_params=_cparams(1),
        name="mixer0_sample",
    )(*args)
    return y, pstate[:, POOL_HALO - POOL_STATE:], gmv


def _nsa_proj_kernel(x_ref, wq_ref, wkv_ref, wg_ref, q_ref, kc_ref, ks_ref, kw_ref, g_ref, *, kvw):
    xb = x_ref[...].astype(BF16)
    q_ref[...] = (_dot(xb, wq_ref[...]) * (NSA_HD ** -0.5)).astype(BF16)
    kv = _dot(xb, wkv_ref[...])
    kc_ref[...] = kv[:, 0:kvw]
    ks_ref[...] = kv[:, kvw:2 * kvw]
    kw_ref[...] = kv[:, 2 * kvw:3 * kvw]
    g_ref[...] = jax.nn.sigmoid(_dot(xb, wg_ref[...]))


def _nsa_proj(x, wq, wkv, wg, tm):
    m, d = x.shape
    qw = wq.shape[1]
    kvw = wkv.shape[1] // 3
    row = lambda n: pl.BlockSpec((tm, n), lambda i: (i, 0))
    return pl.pallas_call(
        functools.partial(_nsa_proj_kernel, kvw=kvw),
        grid=(m // tm,),
        in_specs=[row(d), _resident(wq.shape), _resident(wkv.shape), _resident(wg.shape)],
        out_specs=[row(qw), row(kvw), row(kvw), row(kvw), row(LANE)],
        out_shape=[jax.ShapeDtypeStruct((m, qw), BF16)] + [jax.ShapeDtypeStruct((m, kvw), F32)] * 3
        + [jax.ShapeDtypeStruct((m, LANE), F32)],
        compiler_params=_cparams(1),
        name="nsa_proj",
    )(x, wq, wkv, wg)


def _nsa_proj_prompt_kernel(x_ref, wqt_ref, wgt_ref, wkvt_ref, wkp_ref,
                            qt_ref, gt_ref, ct_ref, st_ref, wt_ref, kp_ref, vt_ref, *, kvw):
    xb = x_ref[0].astype(BF16)
    qt_ref[0] = (_dot_nt(wqt_ref[...], xb) * (NSA_HD ** -0.5)).astype(BF16)
    gt_ref[0] = jax.nn.sigmoid(_dot_nt(wgt_ref[...], xb))
    kvt = _dot_nt(wkvt_ref[...], xb)
    ct_ref[0] = kvt[0:kvw]
    st_ref[0] = kvt[kvw:2 * kvw]
    wt_ref[0] = kvt[2 * kvw:3 * kvw]
    half = kvw // 2
    vt_ref[0, 0] = jnp.concatenate([kvt[kvw + half:2 * kvw], kvt[2 * kvw + half:3 * kvw]], axis=0).astype(BF16)
    kp = _dot(xb, wkp_ref[...])
    tm = kp.shape[0]
    sub = lax.broadcasted_iota(jnp.int32, kp.shape, 1) % LANE
    grp = lax.broadcasted_iota(jnp.int32, kp.shape, 1) // LANE
    kabs = pl.program_id(1) * tm + lax.broadcasted_iota(jnp.int32, kp.shape, 0)
    blk = kabs // SEL_BLOCK
    pat = jnp.where((sub >= KEY_POS_LANE) & (sub < KEY_POS_LANE + 3), blk,
                    jnp.where((sub >= KEY_POS_LANE + 3) & (sub < KEY_POS_LANE + 6), kabs % SEL_BLOCK, 0))
    pat = jnp.where((grp < NSA_KV) & (sub - KEY_SEL_LANE == blk), 1, pat)
    kp_ref[0] = (kp + pat.astype(F32)).astype(BF16)


def _nsa_proj_prompt(x, w, tm):
    b, t, d = x.shape
    qw = w["wq_t"].shape[0]
    kvw = w["wkv_t"].shape[0] // 3
    kpw = w["wk_pad"].shape[1]
    feat = lambda n: pl.BlockSpec((1, n, tm), lambda i, j: (i, 0, j))
    tok = lambda n: pl.BlockSpec((1, tm, n), lambda i, j: (i, j, 0))
    return pl.pallas_call(
        functools.partial(_nsa_proj_prompt_kernel, kvw=kvw),
        grid=(b, t // tm),
        in_specs=[tok(d), _resident(w["wq_t"].shape), _resident(w["wg_t"].shape), _resident(w["wkv_t"].shape),
                  _resident(w["wk_pad"].shape)],
        out_specs=[feat(qw), feat(LANE), feat(kvw), feat(kvw), feat(kvw), tok(kpw),
                   pl.BlockSpec((1, 1, kvw, tm), lambda i, j: (i, j, 0, 0))],
        out_shape=[jax.ShapeDtypeStruct((b, qw, t), BF16), jax.ShapeDtypeStruct((b, LANE, t), F32)]
        + [jax.ShapeDtypeStruct((b, kvw, t), F32)] * 3
        + [jax.ShapeDtypeStruct((b, t, kpw), BF16), jax.ShapeDtypeStruct((b, t // tm, kvw, tm), BF16)],
        compiler_params=_cparams(2),
        name="nsa_proj_prompt",
    )(x, w["wq_t"], w["wg_t"], w["wkv_t"], w["wk_pad"])


def _put_chunk(xs_ref, kv, i, val):
    for c in range(val.shape[1] // LANE):
        xs_ref[kv, c, i * CHUNK_PITCH:i * CHUNK_PITCH + CMP_STRIDE, :] = val[:, c * LANE:(c + 1) * LANE]


def _put_slab(xs_ref, kv, first_chunk, slab_t):
    rows = slab_t.astype(BF16).T.astype(F32)
    for i in range(PAGE // CMP_STRIDE):
        _put_chunk(xs_ref, kv, first_chunk + i, rows[i * CMP_STRIDE:(i + 1) * CMP_STRIDE])


def _pad_chunks(xs_ref, kv, first, last):
    for i in range(first, last):
        _put_chunk(xs_ref, kv, i, jnp.zeros((CMP_STRIDE, xs_ref.shape[1] * LANE), F32))


def _compress_hidden(xs_ref, kv, n, w1_ref, bias_ref):
    nc = n + 8
    low_lanes = lax.broadcasted_iota(jnp.int32, (nc, LANE), 1) < NSA_HD
    heads = []
    for plane in range(xs_ref.shape[1]):
        even, odd = [], []
        for j in range(CMP_STRIDE // 2):
            a = xs_ref[kv, plane, pl.ds(2 * j, nc, stride=CHUNK_PITCH), :]
            b = xs_ref[kv, plane, pl.ds(2 * j + 1, nc, stride=CHUNK_PITCH), :]
            even.append(jnp.where(low_lanes, a, pltpu.roll(b, NSA_HD, 1)))
            odd.append(jnp.where(low_lanes, pltpu.roll(a, NSA_HD, 1), b))
        heads += [jnp.concatenate(v, axis=-1).astype(BF16) for v in (even, odd)]
    r = _dot(jnp.concatenate(heads, axis=0), w1_ref[kv])
    hid = jnp.concatenate(
        [r[g * nc:g * nc + n, :CMP_HIDDEN] + r[g * nc + 1:g * nc + n + 1, CMP_HIDDEN:] for g in range(NSA_KV)],
        axis=-1) + bias_ref[kv]
    return jax.nn.gelu(hid, approximate=True).astype(BF16)


def _compress_prompt_kernel(ct_ref, w1_ref, bias_ref, w2k_ref, w2v_ref, kc_ref, vct_ref, xs_ref, *, n):
    half = NSA_KV * NSA_HD
    cpp = PAGE // CMP_STRIDE
    for kv in range(2):
        for k in range(n // cpp):
            _put_slab(xs_ref, kv, k * cpp, ct_ref[0, kv * half:(kv + 1) * half, k * PAGE:(k + 1) * PAGE])
        _pad_chunks(xs_ref, kv, n, n + 8)
    valid = lax.broadcasted_iota(jnp.int32, (n, 1), 0) < n - 1
    kc = _dot(_compress_hidden(xs_ref, 0, n, w1_ref, bias_ref), w2k_ref[...])
    kc_ref[0] = jnp.where(valid, kc, 0.0).astype(BF16)
    vc = _dot(_compress_hidden(xs_ref, 1, n, w1_ref, bias_ref), w2v_ref[...])
    vct_ref[0] = jnp.where(valid, vc, 0.0).T.astype(BF16)


def _compress_prompt(kv_t, w):
    b, roww, t = kv_t.shape
    n = t // CMP_STRIDE
    half = roww // 2
    return pl.pallas_call(
        functools.partial(_compress_prompt_kernel, n=n),
        grid=(b,),
        in_specs=[pl.BlockSpec((1, roww, t), lambda i: (i, 0, 0)),
                  _resident(w["cmp_w1_cat"].shape), _resident(w["cmp_bias_rows"].shape),
                  _resident(w["cmp_w2k_pad"].shape), _resident(w["cmp_w2_bd"].shape[1:])],
        out_specs=[pl.BlockSpec((1, n, NSA_KV * LANE), lambda i: (i, 0, 0)),
                   pl.BlockSpec((1, half, n), lambda i: (i, 0, 0))],
        out_shape=[jax.ShapeDtypeStruct((b, n, NSA_KV * LANE), BF16), jax.ShapeDtypeStruct((b, half, n), BF16)],
        scratch_shapes=[pltpu.VMEM((2, half // LANE, (n + 8) * CHUNK_PITCH, LANE), F32)],
        compiler_params=_cparams(1),
        name="compress_prompt",
    )(kv_t, w["cmp_w1_cat"], w["cmp_bias_rows"], w["cmp_w2k_pad"], w["cmp_w2_bd"][1])


def _select_topk_cols(score, n_sel):
    nblk = score.shape[0]
    jj = lax.broadcasted_iota(jnp.int32, score.shape, 0)
    rank = jnp.zeros(score.shape, F32)
    for jp in range(nblk):
        row = score[jp:jp + 1, :]
        rank = rank + jnp.where(row > score, 1.0, jnp.where((row == score) & (jj > jp), 1.0, 0.0))
    return rank < float(n_sel)


def _nsa_prompt_kernel(slopes_ref, qt_ref, gt_ref, srow_ref, kc_ref, vct_ref, ks_ref, vst_ref, kw_ref, vwt_ref,
                       smat_ref, o_ref, m_ref, l_ref, acc_ref, osum_ref, *, tq, tk, nblk, nsb):
    g = pl.program_id(1)
    t = pl.program_id(2)
    nh = NSA_HPG
    hd = NSA_HD
    slopes = [slopes_ref[g * nh + h] for h in range(nh)]
    gate = lambda br, h: gt_ref[0, pl.ds(br * NSA_HEADS + g * nh + h, 1), :]
    qt = qt_ref[0]
    tcol = t * tq + lax.broadcasted_iota(jnp.int32, (1, tq), 1)

    def stack_queries(extra_rows):
        return jnp.concatenate(
            [jnp.concatenate([qt[h * hd:(h + 1) * hd], extra_rows(h)], axis=0) for h in range(nh)], axis=1)

    sc = _dot(kc_ref[0], stack_queries(lambda h: jnp.zeros((LANE - hd, tq), BF16)))
    cend = lax.broadcasted_iota(jnp.int32, (nblk, 1), 0) * CMP_STRIDE + (CMP_LEN - 1)
    dist_i = tcol - cend
    ok = dist_i >= 0
    dist = dist_i.astype(F32)
    pc = None
    for h in range(nh):
        p = _masked_softmax(sc[:, h * tq:(h + 1) * tq] - slopes[h] * dist, ok, 0)
        osum_ref[h * hd:(h + 1) * hd, :] = _dot(vct_ref[0], p.astype(BF16)) * gate(0, h)
        pc = p if pc is None else pc + p
    imp = jnp.dot(smat_ref[...], pc, precision=lax.Precision.HIGHEST, preferred_element_type=F32)
    jt = tcol // SEL_BLOCK
    jblk = lax.broadcasted_iota(jnp.int32, (nsb, tq), 0)
    forced = (jblk == 0) | (jblk == jt) | (jblk == jt - 1)
    score = jnp.where(forced, jnp.inf, jnp.where(jblk <= jt, imp, -jnp.inf))
    unsel = jnp.where(_select_topk_cols(score, min(N_SEL, nsb)), 0.0, -SEL_BIG).astype(BF16)
    pad_rows = jnp.zeros((LANE - KEY_SEL_LANE - nsb, tq), BF16)
    q4 = stack_queries(lambda h: jnp.concatenate(
        [jnp.broadcast_to(srow_ref[h], (KEY_SEL_LANE - hd, tq)).astype(BF16), unsel, pad_rows], axis=0))

    def reset():
        m_ref[...] = jnp.full(m_ref.shape, MASK_NEG, F32)
        l_ref[...] = jnp.zeros(l_ref.shape, F32)
        acc_ref[...] = jnp.zeros(acc_ref.shape, F32)

    def tile_local(k_ref, vt_ref, kt, mask):
        off = pl.multiple_of(kt * tk, tk)
        s = _dot(k_ref[0, pl.ds(off, tk), :], q4)
        if mask is not None:
            d_i = (t * tq - kt * tk + lax.broadcasted_iota(jnp.int32, (1, tq), 1)
                   - lax.broadcasted_iota(jnp.int32, (tk, 1), 0))
            okm = d_i >= 0 if mask == "causal" else d_i < WINDOW
            s = jnp.where(jnp.concatenate([okm] * nh, axis=1), s, MASK_NEG)
        m_t = jnp.max(s, 0, keepdims=True)
        eb = jnp.exp(s - m_t).astype(BF16)
        vt = jnp.concatenate([vt_ref[0, kt], jnp.ones((16, tk), BF16)], axis=0)
        pv = [_dot(vt, eb[:, h * tq:(h + 1) * tq]) for h in range(nh)]
        l_t = jnp.concatenate([r[hd:hd + 1] for r in pv], axis=1)
        return m_t, l_t, [r[:hd] for r in pv]

    def merge(piece):
        m_t, l_t, pv = piece
        m_old = m_ref[...]
        m_new = jnp.maximum(m_old, m_t)
        a_old = jnp.exp(m_old - m_new)
        a_t = jnp.where(m_t > MASK_NEG, jnp.exp(m_t - m_new), 0.0)
        l_ref[...] = a_old * l_ref[...] + a_t * l_t
        m_ref[...] = m_new
        for h in range(nh):
            rows = slice(h * hd, (h + 1) * hd)
            cols = slice(h * tq, (h + 1) * tq)
            acc_ref[rows, :] = a_old[:, cols] * acc_ref[rows, :] + a_t[:, cols] * pv[h]

    def tiles(k_ref, vt_ref, kt0, masks):
        pieces = [tile_local(k_ref, vt_ref, kt0 + i, mk) for i, mk in enumerate(masks)]
        for piece in pieces:
            merge(piece)

    def finish(br):
        for h in range(nh):
            rows = slice(h * hd, (h + 1) * hd)
            o = acc_ref[rows, :] / jnp.maximum(l_ref[:, h * tq:(h + 1) * tq], 1e-30)
            osum_ref[rows, :] = osum_ref[rows, :] + o * gate(br, h)

    nd = tq // tk
    kd = t * nd
    reset()
    tiles(ks_ref, vst_ref, kd, ["causal"] * nd)

    def slc_body(i, c):
        tiles(ks_ref, vst_ref, i * nd, [None] * nd)
        return c

    lax.fori_loop(0, t, slc_body, 0)
    finish(1)
    reset()
    tiles(kw_ref, vwt_ref, kd, ["causal"] * nd)
    nwin = WINDOW // tk

    @pl.when(kd >= nwin)
    def _():
        tiles(kw_ref, vwt_ref, kd - nwin, ["window"] * nwin)

    finish(2)
    o_ref[0] = osum_ref[...].T


def _nsa_prompt(slopes, slope_rows, qt, gt, kc, vct, kp, vt, smat, tq):
    b, qw, t = qt.shape
    nblk = kc.shape[1]
    nsb = smat.shape[0]
    tk = vt.shape[-1]
    ntile = vt.shape[1]
    gw = NSA_HPG * NSA_HD
    assert tq % tk == 0 and WINDOW % tk == 0 and tq >= WINDOW
    assert KEY_SEL_LANE + nsb <= LANE and nsb <= 256
    kern = functools.partial(_nsa_prompt_kernel, tq=tq, tk=tk, nblk=nblk, nsb=nsb)
    keys = lambda br: pl.BlockSpec((1, t, LANE), lambda i, g, j: (i, 0, br * NSA_KV + g))
    vals = lambda br: pl.BlockSpec((1, ntile, NSA_HD, tk), lambda i, g, j: (i, 0, br * NSA_KV + g, 0))
    return pl.pallas_call(
        kern,
        grid=(b, NSA_KV, t // tq),
        in_specs=[pl.BlockSpec(memory_space=pltpu.SMEM),
                  pl.BlockSpec((1, gw, tq), lambda i, g, j: (i, g, j)),
                  pl.BlockSpec((1, LANE, tq), lambda i, g, j: (i, 0, j)),
                  pl.BlockSpec((NSA_HPG,) + slope_rows.shape[1:], lambda i, g, j: (g, 0, 0)),
                  pl.BlockSpec((1, nblk, LANE), lambda i, g, j: (i, 0, g)),
                  pl.BlockSpec((1, NSA_HD, nblk), lambda i, g, j: (i, g, 0)),
                  keys(0), vals(0), keys(1), vals(1),
                  pl.BlockSpec(smat.shape, lambda i, g, j: (0, 0))],
        out_specs=pl.BlockSpec((1, tq, gw), lambda i, g, j: (i, j, g)),
        out_shape=jax.ShapeDtypeStruct((b, t, qw), F32),
        scratch_shapes=[pltpu.VMEM((1, NSA_HPG * tq), F32), pltpu.VMEM((1, NSA_HPG * tq), F32),
                        pltpu.VMEM((gw, tq), F32), pltpu.VMEM((gw, tq), F32)],
        compiler_params=_cparams(3),
        name="nsa_prompt",
    )(slopes, qt, gt, slope_rows, kc, vct, kp, vt, kp, vt, smat)


def _diag_heads(o, rows_per_g):
    return jnp.concatenate(
        [o[g * rows_per_g:(g + 1) * rows_per_g, g * NSA_HD:(g + 1) * NSA_HD] for g in range(NSA_KV)], axis=0)


def _select_topk_rows(score, n_sel):
    jj = lax.broadcasted_iota(jnp.int32, score.shape, 1)
    big = jnp.int32(score.shape[1])
    taken = jnp.zeros(score.shape, jnp.bool_)
    for _ in range(n_sel):
        cur = jnp.where(taken, -jnp.inf, score)
        m = jnp.max(cur, -1, keepdims=True)
        cand = jnp.where(jnp.logical_not(taken) & (cur >= m), jj, big)
        pick = jnp.min(cand, -1, keepdims=True)
        taken = taken | (jj == pick)
    return taken


def _page_rows(page_ref, kv):
    return page_ref[0, kv].reshape(NSA_KV * NSA_HD, PAGE)


def _cmp_sample_kernel(pt_ref, *refs, npg, nsteps, ts, pos0, nblk, nsb, nsbp):
    pages = refs[:npg]
    (nxt_ref, new_ref, w1_ref, bias_ref, w2_ref, q_ref, slope_ref, gate_ref, smat_ref,
     o_ref, sel_ref, xs_ref, kc_ref, vc_ref) = refs[npg:]
    s = pl.program_id(1)
    half = NSA_KV * NSA_HD
    n = npg * PAGE // CMP_STRIDE
    last = s == nsteps - 1
    cpp = PAGE // CMP_STRIDE
    for kv in range(2):
        for k in range(npg):
            _put_slab(xs_ref, kv, k * cpp, _page_rows(pages[k], kv))
        nxt = _page_rows(nxt_ref, kv).T[0:CMP_STRIDE]
        new = new_ref[0][:, kv * half:(kv + 1) * half]
        _put_chunk(xs_ref, kv, n, jnp.where(last, new, nxt))
        _pad_chunks(xs_ref, kv, n + 1, n + 8)

    off = pl.multiple_of(s * n, n)
    for kv, c_ref in ((0, kc_ref), (1, vc_ref)):
        c = _dot(_compress_hidden(xs_ref, kv, n, w1_ref, bias_ref), w2_ref[kv])
        c_ref[pl.ds(off, n), :] = c.astype(BF16)

    @pl.when(last)
    def _():
        nrow = NSA_HEADS * ts
        trow = pos0 + lax.broadcasted_iota(jnp.int32, (nrow, 1), 0) % ts
        cend = lax.broadcasted_iota(jnp.int32, (1, nblk), 1) * CMP_STRIDE + (CMP_LEN - 1)
        dist_i = trow - cend
        sc = _dot_nt(q_ref[0], kc_ref[...]) - slope_ref[...] * dist_i.astype(F32)
        p = _masked_softmax(sc, dist_i >= 0, -1)
        o = _diag_heads(_dot(p.astype(BF16), vc_ref[...]), NSA_HPG * ts)
        o_ref[0] = o * gate_ref[0, 0]
        p4 = p.reshape(NSA_KV, NSA_HPG, ts, nblk)
        pc = p4[:, 0]
        for h in range(1, NSA_HPG):
            pc = pc + p4[:, h]
        imp = jnp.dot(pc.reshape(NSA_KV * ts, nblk), smat_ref[...], precision=lax.Precision.HIGHEST,
                      preferred_element_type=F32)
        tsel = pos0 + lax.broadcasted_iota(jnp.int32, (NSA_KV * ts, 1), 0) % ts
        jt = tsel // SEL_BLOCK
        jblk = lax.broadcasted_iota(jnp.int32, (NSA_KV * ts, nsbp), 1)
        forced = (jblk == 0) | (jblk == jt) | (jblk == jt - 1)
        score = jnp.where(forced, jnp.inf, jnp.where(jblk <= jt, imp, -jnp.inf))
        score = jnp.where(jblk < nsb, score, -jnp.inf)
        sel_ref[0] = jnp.where(_select_topk_rows(score, min(N_SEL, nsb)), 0.0, -1.0)


def _cmp_sample(page_table, cache_t, new_chunk, qbd, slope_col, gates, smat, w, npg, ts, pos0):
    nb, npages = page_table.shape
    half = NSA_KV * NSA_HD
    nsteps = npages // npg
    nblk = npages * PAGE // CMP_STRIDE
    nsbp = smat.shape[1]
    nsb = -(-(npages * PAGE + ts) // SEL_BLOCK)
    nrow = NSA_HEADS * ts
    kern = functools.partial(_cmp_sample_kernel, npg=npg, nsteps=nsteps, ts=ts, pos0=pos0, nblk=nblk, nsb=nsb,
                             nsbp=nsbp)
    pshape = (1,) + cache_t.shape[1:]
    page_spec = lambda k: pl.BlockSpec(pshape, lambda i, s, pt: (pt[i, s * npg + k], 0, 0, 0, 0))
    nxt_spec = pl.BlockSpec(pshape, lambda i, s, pt: (pt[i, jnp.minimum((s + 1) * npg, npages - 1)], 0, 0, 0, 0))
    per_b = lambda shape: pl.BlockSpec((1,) + shape, lambda i, s, pt: (i,) + (0,) * len(shape))
    fixed = lambda a: pl.BlockSpec(a.shape, lambda i, s, pt: (0,) * a.ndim, pipeline_mode=pl.Buffered(1))
    grid_spec = pltpu.PrefetchScalarGridSpec(
        num_scalar_prefetch=1,
        grid=(nb, nsteps),
        in_specs=[page_spec(k) for k in range(npg)] + [
            nxt_spec, per_b((CMP_STRIDE, 2 * half)),
            fixed(w["cmp_w1_cat"]), fixed(w["cmp_bias_rows"]), fixed(w["cmp_w2_bd"]),
            per_b((nrow, half)), fixed(slope_col),
            pl.BlockSpec((1, 1, nrow, 1), lambda i, s, pt: (0, i, 0, 0)), fixed(smat)],
        out_specs=[per_b((nrow, NSA_HD)), per_b((NSA_KV * ts, nsbp))],
        scratch_shapes=[pltpu.VMEM((2, half // LANE, (npg * PAGE // CMP_STRIDE + 8) * CHUNK_PITCH, LANE), F32),
                        pltpu.VMEM((nblk, half), BF16),
                        pltpu.VMEM((nblk, half), BF16)],
    )
    return pl.pallas_call(
        kern,
        grid_spec=grid_spec,
        out_shape=[jax.ShapeDtypeStruct((nb, nrow, NSA_HD), F32), jax.ShapeDtypeStruct((nb, NSA_KV * ts, nsbp), F32)],
        compiler_params=_cparams(2, VMEM_BIG),
        name="cmp_select_sample",
    )(page_table, *([cache_t] * npg), cache_t, new_chunk, w["cmp_w1_cat"], w["cmp_bias_rows"], w["cmp_w2_bd"],
      qbd, slope_col, gates, smat)


def _flash_rows(s, ok, pv, m_ref, l_ref, acc_ref):
    s = jnp.where(ok, s, MASK_NEG)
    m_old = m_ref[...]
    m_new = jnp.maximum(m_old, jnp.max(s, -1, keepdims=True))
    alpha = jnp.exp(m_old - m_new)
    e = jnp.where(ok, jnp.exp(s - m_new), 0.0)
    l_ref[...] = alpha * l_ref[...] + jnp.sum(e, -1, keepdims=True)
    acc_ref[...] = alpha * acc_ref[...] + pv(e.astype(BF16))
    m_ref[...] = m_new


def _slc_sample_kernel(pt_ref, *refs, npg, nsteps, ts, pos0):
    pages = refs[:npg]
    (new_ref, q_ref, slope_ref, gate_ref, auxq_ref, auxk_ref, selt_ref, prev_ref, o_ref, m_ref, l_ref,
     acc_ref) = refs[npg:]
    s = pl.program_id(1)
    half = NSA_KV * NSA_HD
    nrow = NSA_HEADS * ts
    trow = pos0 + lax.broadcasted_iota(jnp.int32, (nrow, 1), 0) % ts

    @pl.when(s == 0)
    def _():
        m_ref[...] = jnp.full(m_ref.shape, MASK_NEG, F32)
        l_ref[...] = jnp.zeros(l_ref.shape, F32)
        acc_ref[...] = jnp.zeros(acc_ref.shape, F32)

    kt = jnp.concatenate([_page_rows(pages[k], 0) for k in range(npg)], axis=1).astype(BF16)
    vt = jnp.concatenate([_page_rows(pages[k], 1) for k in range(npg)], axis=1).astype(BF16)
    nk = npg * PAGE
    s_all = _dot(q_ref[0], kt) + _dot(auxq_ref[0, 0], auxk_ref[...])
    m_old = m_ref[...]
    m_new = jnp.maximum(m_old, jnp.max(s_all, -1, keepdims=True))
    alpha = jnp.exp(m_old - m_new)
    e = jnp.exp(s_all - m_new)
    l_ref[...] = alpha * l_ref[...] + jnp.sum(e, -1, keepdims=True)
    acc_ref[...] = alpha * acc_ref[...] + _dot_nt(e.astype(BF16), vt)
    m_ref[...] = m_new

    @pl.when(s == nsteps - 1)
    def _():
        new = new_ref[0]
        kpos_n = nsteps * nk + lax.broadcasted_iota(jnp.int32, (1, PAGE), 1)
        dist_n = trow - kpos_n
        ok_n = (dist_n >= 0) & (selt_ref[0] > -0.5) & (kpos_n < nsteps * nk + SEL_BLOCK)
        sc_n = (_dot_nt(q_ref[0], new[:, :half].astype(BF16))
                - slope_ref[...] * (dist_n - (trow - pos0)).astype(F32))
        _flash_rows(sc_n, ok_n, lambda e: _dot(e, new[:, half:].astype(BF16)), m_ref, l_ref, acc_ref)
        o = _diag_heads(acc_ref[...] / jnp.maximum(l_ref[...], 1e-30), NSA_HPG * ts)
        o_ref[0] = prev_ref[0] + o * gate_ref[0, 0]


def _slc_sample(page_table, cache_t, new_page, qbd, slope_col, gates, aux_q, aux_k, sel_tail, prev, npg, ts, pos0):
    nb, npages = page_table.shape
    half = NSA_KV * NSA_HD
    nsteps = npages // npg
    nrow = NSA_HEADS * ts
    naux = aux_k.shape[0]
    kern = functools.partial(_slc_sample_kernel, npg=npg, nsteps=nsteps, ts=ts, pos0=pos0)
    pshape = (1,) + cache_t.shape[1:]
    page_spec = lambda k: pl.BlockSpec(pshape, lambda i, s, pt: (pt[i, s * npg + k], 0, 0, 0, 0))
    per_b = lambda shape: pl.BlockSpec((1,) + shape, lambda i, s, pt: (i,) + (0,) * len(shape))
    grid_spec = pltpu.PrefetchScalarGridSpec(
        num_scalar_prefetch=1,
        grid=(nb, nsteps),
        in_specs=[page_spec(k) for k in range(npg)] + [
            per_b((PAGE, 2 * half)), per_b((nrow, half)),
            pl.BlockSpec((nrow, 1), lambda i, s, pt: (0, 0)),
            pl.BlockSpec((1, 1, nrow, 1), lambda i, s, pt: (1, i, 0, 0)),
            pl.BlockSpec((1, 1, nrow, naux), lambda i, s, pt: (i, s, 0, 0)),
            pl.BlockSpec(aux_k.shape, lambda i, s, pt: (0, 0), pipeline_mode=pl.Buffered(1)),
            per_b((nrow, 1)), per_b((nrow, NSA_HD))],
        out_specs=per_b((nrow, NSA_HD)),
        scratch_shapes=[pltpu.VMEM((nrow, 1), F32), pltpu.VMEM((nrow, 1), F32), pltpu.VMEM((nrow, half), F32)],
    )
    return pl.pallas_call(
        kern,
        grid_spec=grid_spec,
        out_shape=jax.ShapeDtypeStruct((nb, nrow, NSA_HD), F32),
        compiler_params=_cparams(2),
        name="slc_sample",
    )(page_table, *([cache_t] * npg), new_page, qbd, slope_col, gates, aux_q, aux_k, sel_tail, prev)


def _win_sample_kernel(past_ref, new_ref, q_ref, slope_ref, gate_ref, prev_ref, o_ref, *, ts, pos0, wb):
    half = NSA_KV * NSA_HD
    nrow = NSA_HEADS * ts
    trow = pos0 + lax.broadcasted_iota(jnp.int32, (nrow, 1), 0) % ts
    q = q_ref[0]
    new = new_ref[0]
    kt = past_ref[0, 0].reshape(half, wb).astype(BF16)
    vt = past_ref[0, 1].reshape(half, wb).astype(BF16)
    dist_p = trow - (pos0 - wb + lax.broadcasted_iota(jnp.int32, (1, wb), 1))
    dist_n = trow - (pos0 + lax.broadcasted_iota(jnp.int32, (1, ts), 1))
    ok_p = (dist_p >= 0) & (dist_p < WINDOW) & (trow - dist_p >= 0)
    ok_n = (dist_n >= 0) & (dist_n < WINDOW)
    s_p = jnp.where(ok_p, _dot(q, kt) - slope_ref[...] * dist_p.astype(F32), MASK_NEG)
    s_n = jnp.where(ok_n, _dot_nt(q, new[:, :half].astype(BF16)) - slope_ref[...] * dist_n.astype(F32), MASK_NEG)
    m = jnp.maximum(jnp.max(s_p, -1, keepdims=True), jnp.max(s_n, -1, keepdims=True))
    e_p = jnp.where(ok_p, jnp.exp(s_p - m), 0.0)
    e_n = jnp.where(ok_n, jnp.exp(s_n - m), 0.0)
    den = jnp.maximum(jnp.sum(e_p, -1, keepdims=True) + jnp.sum(e_n, -1, keepdims=True), 1e-30)
    o = _dot_nt((e_p / den).astype(BF16), vt) + _dot((e_n / den).astype(BF16), new[:, half:].astype(BF16))
    o_ref[0] = prev_ref[0] + _diag_heads(o, NSA_HPG * ts) * gate_ref[0, 0]


def _win_sample(win_t, new_rows, qbd, slope_col, gates, prev, ts, pos0):
    nb = win_t.shape[0]
    wb = win_t.shape[-1]
    nrow = NSA_HEADS * ts
    half = NSA_KV * NSA_HD
    per_b = lambda shape: pl.BlockSpec((1,) + shape, lambda i: (i,) + (0,) * len(shape))
    return pl.pallas_call(
        functools.partial(_win_sample_kernel, ts=ts, pos0=pos0, wb=wb),
        grid=(nb,),
        in_specs=[per_b(win_t.shape[1:]), per_b((ts, 2 * half)), per_b((nrow, half)),
                  pl.BlockSpec((nrow, 1), lambda i: (0, 0)),
                  pl.BlockSpec((1, 1, nrow, 1), lambda i: (2, i, 0, 0)),
                  per_b((nrow, NSA_HD))],
        out_specs=per_b((nrow, NSA_HD)),
        out_shape=jax.ShapeDtypeStruct((nb, nrow, NSA_HD), F32),
        compiler_params=_cparams(1),
        name="win_sample",
    )(win_t, new_rows, qbd, slope_col, gates, prev)


def _importance_matrix(nblk, nsb):
    rs = SEL_BLOCK // CMP_STRIDE
    rc = CMP_LEN // CMP_STRIDE
    s = np.zeros((nsb, nblk), np.float32)
    for j in range(nsb):
        for n in range(rc):
            for r in range(rs):
                m = rs * j + r - n
                if 0 <= m < nblk:
                    s[j, m] += 1.0
    return s


def _pad_heads(a):
    lead = a.shape[:-1]
    a = a.reshape(lead + (NSA_KV, NSA_HD))
    a = jnp.pad(a, ((0, 0),) * len(lead) + ((0, 0), (0, LANE - NSA_HD)))
    return a.reshape(lead + (NSA_KV * LANE,))


def _prep_weights(w_ffn_in, w_ffn_out, ln_g, ln_b, w_in_even, pool_w, pool_scale, gm_ln_g, gm_ln_b, gm_ws, gm_b,
                  w_out_even, w_in_odd, cmp_pe, cmp_w1, cmp_b1, cmp_w2, w_out_odd, ts, nb_s):
    d = w_in_even.shape[0]
    dff = w_ffn_out.shape[2]
    fp = -(-dff // (2 * LANE)) * (2 * LANE)
    w = {}
    w["ffn_g"] = jnp.pad(w_ffn_in[..., :dff], ((0, 0),) * 3 + ((0, fp - dff),)).astype(BF16)
    w["ffn_u"] = jnp.pad(w_ffn_in[..., dff:], ((0, 0),) * 3 + ((0, fp - dff),)).astype(BF16)
    w["ffn_o"] = jnp.pad(w_ffn_out, ((0, 0), (0, 0), (0, fp - dff), (0, 0))).astype(BF16)
    w["ln_g"] = ln_g.reshape(ln_g.shape[0], ln_g.shape[1], 1, d)
    w["ln_b"] = ln_b.reshape(ln_b.shape[0], ln_b.shape[1], 1, d)
    hw = gm_ws.shape[1]
    w["w_in_even"] = w_in_even.astype(BF16)
    w["pool_w"] = pool_w.astype(BF16)
    w["pool_scale"] = pool_scale.reshape(1, -1)
    w["gm_ln_g"] = gm_ln_g.reshape(1, -1)
    w["gm_ln_b"] = gm_ln_b.reshape(1, -1)
    w["gm_ws"] = jnp.tril(gm_ws).astype(BF16)
    w["gm_b_full"] = jnp.repeat(gm_b.T, hw, axis=1)
    cs = min(ts, GMLP_CHUNK)
    ws_s = jnp.tril(gm_ws[:, :cs, :cs])
    w["gm_ws_kron"] = jnp.einsum("ab,hts->hatbs", jnp.eye(nb_s * (ts // cs), dtype=F32), ws_s).reshape(
        gm_ws.shape[0], nb_s * ts, nb_s * ts).astype(BF16)
    w["gm_b_rows"] = jnp.tile(jnp.repeat(gm_b[:, :cs].T, hw, axis=1), (nb_s * (ts // cs), 1))
    w["w_out_even"] = w_out_even.astype(BF16)
    qw = NSA_HEADS * NSA_HD
    half = NSA_KV * NSA_HD
    kvw = 3 * 2 * half
    ng = w_in_odd.shape[1] - qw - kvw
    wq = w_in_odd[:, :qw]
    wkv = w_in_odd[:, qw:qw + kvw]
    wg = jnp.pad(w_in_odd[:, qw + kvw:], ((0, 0), (0, LANE - ng)))
    w["wq"] = wq.astype(BF16)
    w["wkv"] = wkv.astype(BF16)
    w["wg"] = wg.astype(BF16)
    w["wq_t"] = wq.T.astype(BF16)
    w["wkv_t"] = wkv.T.astype(BF16)
    w["wg_t"] = wg.T.astype(BF16)
    w["wk_pad"] = jnp.concatenate([_pad_heads(wkv[:, 2 * half:3 * half]), _pad_heads(wkv[:, 4 * half:5 * half])],
                                  axis=1).astype(BF16)
    w["w_out_odd"] = w_out_odd.astype(BF16)
    rc = CMP_LEN // CMP_STRIDE
    eye = jnp.eye(NSA_KV, dtype=F32)
    w1 = cmp_w1.reshape(2, rc, CMP_STRIDE, NSA_HD, CMP_HIDDEN)
    w["cmp_w1_cat"] = w1.transpose(0, 2, 3, 1, 4).reshape(2, CMP_STRIDE * NSA_HD, rc * CMP_HIDDEN).astype(BF16)
    pe_bias = jnp.einsum("krpd,krpdf->kf", cmp_pe.reshape(2, rc, CMP_STRIDE, NSA_HD), w1,
                         precision=lax.Precision.HIGHEST)
    w["cmp_bias_rows"] = jnp.tile(cmp_b1 + pe_bias, (1, NSA_KV)).reshape(2, 1, -1)
    w2_bd = jnp.einsum("kfd,gh->kgfhd", cmp_w2, eye).reshape(2, NSA_KV * CMP_HIDDEN, half)
    w["cmp_w2_bd"] = w2_bd.astype(BF16)
    w["cmp_w2k_pad"] = _pad_heads(w2_bd[0]).astype(BF16)
    slopes = 2.0 ** (-8.0 * (jnp.arange(NSA_HEADS, dtype=F32) + 1.0) / NSA_HEADS)
    w["slopes"] = slopes
    hi = slopes.astype(BF16).astype(F32)
    mid = (slopes - hi).astype(BF16).astype(F32)
    lo = (slopes - hi - mid).astype(BF16).astype(F32)
    pieces = jnp.stack([hi, mid, lo], axis=1)
    rows = jnp.concatenate([pieces * SEL_BLOCK, pieces,
                            jnp.zeros((NSA_HEADS, KEY_SEL_LANE - KEY_POS_LANE - 6), F32)], axis=1)
    w["slope_rows"] = rows[:, :, None]
    return w


def _rows_from_feature_major(a_t):
    b, _, t = a_t.shape
    return a_t.reshape(b, 2, NSA_KV, NSA_HD, t).transpose(0, 4, 1, 2, 3)


def _trunk_prompt(x3, w):
    b, t, d = x3.shape
    m = b * t
    tm = 256
    tf = 512
    tq = WINDOW
    tk = WINDOW // 2
    x = _ffn(x3.reshape(m, d), w, 0, 0, tf)
    x, pool_state = _mixer0_prompt(x.reshape(b, t, d), w, tm)
    x = _ffn(x.reshape(m, d), w, 1, 0, tf)

    qt, gt, ct, st, wt, kp, vt = _nsa_proj_prompt(x.reshape(b, t, d), w, tk)
    kc, vct = _compress_prompt(ct, w)
    smat = jnp.asarray(_importance_matrix(kc.shape[1], -(-t // SEL_BLOCK)))
    o = _nsa_prompt(w["slopes"], w["slope_rows"], qt, gt, kc, vct, kp, vt, smat, tq)

    x = _outproj_ffn(x, o.reshape(m, -1), w["w_out_odd"], w, 1, tf)
    win = _rows_from_feature_major(wt[:, :, t - min(WINDOW, t):])
    return x.reshape(b, t, d), pool_state, _rows_from_feature_major(ct), _rows_from_feature_major(st), win


def _split3(x):
    hi = x.astype(BF16).astype(F32)
    mid = (x - hi).astype(BF16).astype(F32)
    return hi, mid, (x - hi - mid).astype(BF16).astype(F32)


def _slc_aux_operands(sel_steps, slope_rows, slope_col, npg, ts, pos0):
    nb, nsteps, nrow, bps = sel_steps.shape
    nk = npg * PAGE
    naux = -(-(bps + 9) // 16) * 16
    pieces = jnp.repeat(slope_rows[:, :6, 0], ts, axis=0)
    start = pos0 - jnp.arange(nsteps, dtype=F32) * nk
    offs = jnp.stack(_split3(-slope_col[None, :, 0] * start[:, None]), axis=-1)
    aux_q = jnp.concatenate([
        sel_steps * SEL_BIG,
        jnp.broadcast_to(pieces, (nb, nsteps, nrow, 6)),
        jnp.broadcast_to(offs[None], (nb, nsteps, nrow, 3)),
        jnp.zeros((nb, nsteps, nrow, naux - bps - 9), F32)], axis=-1).astype(BF16)
    j = np.arange(nk)
    aux_k = np.zeros((naux, nk), np.float32)
    aux_k[j // SEL_BLOCK, j] = 1.0
    aux_k[bps:bps + 3] = j // SEL_BLOCK
    aux_k[bps + 3:bps + 6] = j % SEL_BLOCK
    aux_k[bps + 6:bps + 9] = 1.0
    return aux_q, jnp.asarray(aux_k, dtype=BF16)


def _trunk_sample(x3, pool_past, cache_cmp, cache_slc, win_past, page_table, w):
    nb, ts, d = x3.shape
    m = nb * ts
    npages = page_table.shape[1]
    pos0 = npages * PAGE
    npg = 16
    x = _ffn(x3.reshape(m, d), w, 0, 0, m)
    x, pool_state, gm_v = _mixer0_sample(x, pool_past, w, w["ln_g"][0, 1], w["ln_b"][0, 1], nb, ts, pos0)
    x = _ffn(x, w, 0, 1, m)
    x = _ffn(x, w, 1, 0, m)

    q, kvc, kvs, kvw, gates = _nsa_proj(x, w["wq"], w["wkv"], w["wg"], m)
    half = NSA_KV * NSA_HD
    roww = 2 * half
    nrow = NSA_HEADS * ts
    qr = q.reshape(nb, ts, NSA_KV, NSA_HPG, NSA_HD).transpose(0, 2, 3, 1, 4).reshape(nb, NSA_KV, NSA_HPG * ts, NSA_HD)
    qbd = jnp.einsum("bgrd,gk->bgrkd", qr, jnp.eye(NSA_KV, dtype=BF16)).reshape(nb, nrow, half)
    slope_col = jnp.repeat(w["slopes"], ts).reshape(nrow, 1)
    gcols = gates[:, :3 * NSA_HEADS].reshape(nb, ts, 3, NSA_HEADS).transpose(2, 0, 3, 1).reshape(3, nb, nrow, 1)
    feature_major = lambda c: c.transpose(0, 2, 3, 4, 1)

    nblk = npages * PAGE // CMP_STRIDE
    nsb = -(-(pos0 + ts) // SEL_BLOCK)
    nsbp = -(-nsb // LANE) * LANE
    smat = jnp.asarray(np.pad(_importance_matrix(nblk, nsb).T, ((0, 0), (0, nsbp - nsb))))
    new_chunk = jnp.pad(kvc.reshape(nb, ts, roww), ((0, 0), (0, CMP_STRIDE - ts), (0, 0)))
    o, sel = _cmp_sample(page_table, feature_major(cache_cmp), new_chunk, qbd, slope_col, gcols, smat, w, npg, ts,
                         pos0)

    bpp = PAGE // SEL_BLOCK
    nsteps = npages // npg
    sel_rows = jnp.broadcast_to(sel.reshape(nb, NSA_KV, 1, ts, nsbp), (nb, NSA_KV, NSA_HPG, ts, nsbp))
    sel_rows = sel_rows.reshape(nb, nrow, nsbp)
    sel_steps = sel_rows[:, :, :npages * bpp].reshape(nb, nrow, nsteps, npg * bpp).transpose(0, 2, 1, 3)
    sel_tail = sel_rows[:, :, npages * bpp:npages * bpp + 1]
    new_page = jnp.pad(kvs.reshape(nb, ts, roww), ((0, 0), (0, PAGE - ts), (0, 0)))
    aux_q, aux_k = _slc_aux_operands(sel_steps, w["slope_rows"], slope_col, npg, ts, pos0)
    o = _slc_sample(page_table, feature_major(cache_slc), new_page, qbd, slope_col, gcols, aux_q, aux_k, sel_tail, o,
                    npg, ts, pos0)
    win_t = feature_major(win_past)
    wb = win_t.shape[-1]
    new_win = kvw.reshape(nb, ts, roww)
    o = _win_sample(win_t, new_win, qbd, slope_col, gcols, o, ts, pos0)

    o = o.reshape(nb, NSA_KV, NSA_HPG, ts, NSA_HD).transpose(0, 3, 1, 2, 4).reshape(m, NSA_HEADS * NSA_HD)
    x = _outproj_ffn(x, o, w["w_out_odd"], w, 1, m)
    kv5 = lambda a: a.reshape(nb, ts, 2, NSA_KV, NSA_HD)
    new_win_t = feature_major(kv5(kvw))
    win_buf = jnp.concatenate([win_t, new_win_t], axis=-1)[..., -wb:].transpose(0, 4, 1, 2, 3)
    return x.reshape(nb, ts, d), pool_state, gm_v.reshape(nb, ts, -1), kv5(kvc), kv5(kvs), win_buf


def kernel(x_prompt, x_sample, state_l0_pool, cache_l1_cmp_kv, cache_l1_slc_kv, cache_l1_win_kv, page_table,
           w_ffn_in, w_ffn_out, ln_g, ln_b, w_in_even, pool_w, pool_scale, gm_ln_g, gm_ln_b, gm_ws, gm_b,
           w_out_even, w_in_odd, cmp_pe, cmp_w1, cmp_b1, cmp_w2, w_out_odd):
    w = _prep_weights(w_ffn_in, w_ffn_out, ln_g, ln_b, w_in_even, pool_w, pool_scale, gm_ln_g, gm_ln_b, gm_ws,
                      gm_b, w_out_even, w_in_odd, cmp_pe, cmp_w1, cmp_b1, cmp_w2, w_out_odd,
                      x_sample.shape[1], x_sample.shape[0])
    y_p, pool_p, cmp_p, slc_p, win_p = _trunk_prompt(x_prompt, w)
    y_s, pool_s, gmv_s, cmp_s, slc_s, win_s = _trunk_sample(
        x_sample, state_l0_pool, cache_l1_cmp_kv, cache_l1_slc_kv, cache_l1_win_kv, page_table, w)
    return (y_p, y_s, pool_p, pool_s, gmv_s, cmp_p, slc_p, win_p, cmp_s, slc_s, win_s)
```

```python
import functools

import numpy as np
import jax
import jax.numpy as jnp
from jax import lax
from jax.experimental import pallas as pl
from jax.experimental.pallas import tpu as pltpu

F32 = jnp.float32
BF16 = jnp.bfloat16

DEPTH = 2
DN_ALPHA = (2 * DEPTH) ** 0.25
LN_EPS = 1e-5
POOL_WINDOWS = (2, 4, 8, 16)
POOL_STATE = max(POOL_WINDOWS) - 1
POOL_HALO = 16
GMLP_CHUNK = 128
NSA_HEADS = 16
NSA_KV = 4
NSA_HPG = NSA_HEADS // NSA_KV
NSA_HD = 64
CMP_LEN = 32
CMP_STRIDE = 16
CMP_HIDDEN = 128
SEL_BLOCK = 64
N_SEL = 16
WINDOW = 512
PAGE = 128

LANE = 128
MASK_NEG = -1e30
SEL_BIG = 2.0 ** 100
KEY_POS_LANE = NSA_HD
KEY_SEL_LANE = NSA_HD + 16
CHUNK_PITCH = 20
VMEM_BIG = 56 << 20

_NT = (((1,), (1,)), ((), ()))


def _cparams(n_axes, vmem=None, flags=None):
    return pltpu.CompilerParams(dimension_semantics=("arbitrary",) * n_axes, vmem_limit_bytes=vmem, flags=flags)


def _resident(shape):
    nd = len(shape)
    return pl.BlockSpec(shape, lambda *_: (0,) * nd, pipeline_mode=pl.Buffered(1))


def _whole(shape):
    nd = len(shape)
    return pl.BlockSpec(shape, lambda *_: (0,) * nd)


def _dot(a, b):
    return jnp.dot(a, b, preferred_element_type=F32)


def _dot_nt(a, b):
    return lax.dot_general(a, b, _NT, preferred_element_type=F32)


def _ln(x, g, b):
    mu = jnp.mean(x, -1, keepdims=True)
    xc = x - mu
    var = jnp.mean(xc * xc, -1, keepdims=True)
    return xc * lax.rsqrt(var + LN_EPS) * g + b


def _masked_softmax(s, ok, axis):
    s = jnp.where(ok, s, MASK_NEG)
    m = jnp.max(s, axis, keepdims=True)
    e = jnp.where(ok, jnp.exp(s - m), 0.0)
    return e / jnp.maximum(jnp.sum(e, axis, keepdims=True), 1e-30)


def _ffn_math(x, wg_ref, wu_ref, wo_ref, g_ref, b_ref):
    xb = x.astype(BF16)
    gate = _dot(xb, wg_ref[...])
    up = _dot(xb, wu_ref[...])
    act = (gate * jax.nn.sigmoid(gate) * up).astype(BF16)
    return _ln(DN_ALPHA * x + 0.5 * _dot(act, wo_ref[...]), g_ref[...], b_ref[...])


def _ffn_kernel(x_ref, wg_ref, wu_ref, wo_ref, g_ref, b_ref, o_ref):
    o_ref[...] = _ffn_math(x_ref[...], wg_ref, wu_ref, wo_ref, g_ref, b_ref)


def _ffn_specs(w, layer, which):
    d, fp = w["ffn_g"].shape[-2:]
    ln_idx = which + (which > 0)
    pick = lambda shape, j: pl.BlockSpec((None, None) + shape, lambda *_: (layer, j, 0, 0),
                                         pipeline_mode=pl.Buffered(1))
    specs = [pick((d, fp), which), pick((d, fp), which), pick((fp, d), which), pick((1, d), ln_idx),
             pick((1, d), ln_idx)]
    return specs, (w["ffn_g"], w["ffn_u"], w["ffn_o"], w["ln_g"], w["ln_b"])


def _ffn(x, w, layer, which, tm):
    m, d = x.shape
    specs, ops = _ffn_specs(w, layer, which)
    return pl.pallas_call(
        _ffn_kernel,
        grid=(m // tm,),
        in_specs=[pl.BlockSpec((tm, d), lambda i: (i, 0))] + specs,
        out_specs=pl.BlockSpec((tm, d), lambda i: (i, 0)),
        out_shape=jax.ShapeDtypeStruct((m, d), F32),
        compiler_params=_cparams(1, VMEM_BIG),
        name="ffn_ln",
    )(x, *ops)


def _outproj_ffn_kernel(x_ref, a_ref, w_ref, g_ref, b_ref, wg_ref, wu_ref, wo_ref, g2_ref, b2_ref, o_ref):
    y = _ln(DN_ALPHA * x_ref[...] + _dot(a_ref[...].astype(BF16), w_ref[...]), g_ref[...], b_ref[...])
    o_ref[...] = _ffn_math(y, wg_ref, wu_ref, wo_ref, g2_ref, b2_ref)


def _outproj_ffn(x, a, w_out, w, layer, tm):
    m, d = x.shape
    k = a.shape[1]
    specs, ops = _ffn_specs(w, layer, 1)
    ln = lambda: pl.BlockSpec((None, None, 1, d), lambda i: (layer, 1, 0, 0), pipeline_mode=pl.Buffered(1))
    return pl.pallas_call(
        _outproj_ffn_kernel,
        grid=(m // tm,),
        in_specs=[pl.BlockSpec((tm, d), lambda i: (i, 0)), pl.BlockSpec((tm, k), lambda i: (i, 0)),
                  _resident((k, d)), ln(), ln()] + specs,
        out_specs=pl.BlockSpec((tm, d), lambda i: (i, 0)),
        out_shape=jax.ShapeDtypeStruct((m, d), F32),
        compiler_params=_cparams(1, VMEM_BIG),
        name="outproj_ffn",
    )(x, a, w_out, w["ln_g"], w["ln_b"], *ops)


def _pool_groups(read_window, p, cnt, poolw_ref, gw):
    outs = []
    for g, w in enumerate(POOL_WINDOWS):
        lanes = slice(g * gw, (g + 1) * gw)
        tot = p[..., lanes]
        for j in range(1, w):
            tot = tot + read_window(j, lanes)
        d = tot / jnp.minimum(float(w), cnt) - p[..., lanes]
        d2 = d.reshape(-1, gw).astype(BF16)
        outs.append(_dot(d2, poolw_ref[g]))
    return jnp.concatenate(outs, axis=-1)


def _mixer0_prompt_kernel(x_ref, win_ref, poolw_ref, pscale_ref, glng_ref, glnb_ref, ws_ref, gmb_ref,
                          wout_ref, lng_ref, lnb_ref, wg_ref, wu_ref, wo_ref, g2_ref, b2_ref,
                          y_ref, pstate_ref, ext_ref, *, tm, pw, gw, hw):
    t = pl.program_id(1)
    x = x_ref[0]
    z = _dot(x.astype(BF16), win_ref[...])
    p = z[:, :pw]
    u = z[:, pw:pw + 4 * hw]
    v = _ln(z[:, pw + 4 * hw:], glng_ref[...], glnb_ref[...])

    @pl.when(t == 0)
    def _():
        ext_ref[0:POOL_HALO, :] = jnp.zeros((POOL_HALO, pw), F32)

    ext_ref[POOL_HALO:POOL_HALO + tm, :] = p
    cnt = (t * tm + lax.broadcasted_iota(jnp.int32, (tm, 1), 0) + 1).astype(F32)
    a = _pool_groups(lambda j, lanes: ext_ref[POOL_HALO - j:POOL_HALO - j + tm, lanes], p, cnt, poolw_ref, gw)
    a = a * pscale_ref[...]
    tail = ext_ref[tm:tm + POOL_HALO, :]
    pstate_ref[0] = tail
    ext_ref[0:POOL_HALO, :] = tail

    vb = v.astype(BF16)
    rows = []
    for c in range(tm // GMLP_CHUNK):
        r = slice(c * GMLP_CHUNK, (c + 1) * GMLP_CHUNK)
        rows.append(jnp.concatenate(
            [_dot(ws_ref[h], vb[r, h * hw:(h + 1) * hw]) for h in range(4)], axis=-1) + gmb_ref[...])
    gb = u * jnp.concatenate(rows, axis=0)
    y = _dot(a.astype(BF16), wout_ref[0:pw, :]) + _dot(gb.astype(BF16), wout_ref[pw:, :])
    y = _ln(DN_ALPHA * x + y, lng_ref[...], lnb_ref[...])
    y_ref[0] = _ffn_math(y, wg_ref, wu_ref, wo_ref, g2_ref, b2_ref)


def _mixer0_prompt(x, w, tm):
    b, t, d = x.shape
    pw = w["pool_scale"].shape[1]
    gw = pw // len(POOL_WINDOWS)
    hw = w["gm_ws"].shape[1]
    kern = functools.partial(_mixer0_prompt_kernel, tm=tm, pw=pw, gw=gw, hw=hw)
    ffn_specs, ffn_ops = _ffn_specs(w, 0, 1)
    ln = lambda: pl.BlockSpec((None, None, 1, d), lambda i, j: (0, 1, 0, 0), pipeline_mode=pl.Buffered(1))
    y, pstate = pl.pallas_call(
        kern,
        grid=(b, t // tm),
        in_specs=[pl.BlockSpec((1, tm, d), lambda i, j: (i, j, 0)),
                  _resident(w["w_in_even"].shape), _resident(w["pool_w"].shape), _resident(w["pool_scale"].shape),
                  _resident(w["gm_ln_g"].shape), _resident(w["gm_ln_b"].shape), _resident(w["gm_ws"].shape),
                  _resident(w["gm_b_full"].shape), _resident(w["w_out_even"].shape), ln(), ln()] + ffn_specs,
        out_specs=[pl.BlockSpec((1, tm, d), lambda i, j: (i, j, 0)),
                   pl.BlockSpec((1, POOL_HALO, pw), lambda i, j: (i, 0, 0))],
        out_shape=[jax.ShapeDtypeStruct((b, t, d), F32), jax.ShapeDtypeStruct((b, POOL_HALO, pw), F32)],
        scratch_shapes=[pltpu.VMEM((POOL_HALO + tm, pw), F32)],
        compiler_params=_cparams(2, VMEM_BIG),
        name="mixer0_ffn_prompt",
    )(x, w["w_in_even"], w["pool_w"], w["pool_scale"], w["gm_ln_g"], w["gm_ln_b"], w["gm_ws"],
      w["gm_b_full"], w["w_out_even"], w["ln_g"], w["ln_b"], *ffn_ops)
    return y, pstate[:, POOL_HALO - POOL_STATE:]


def _mixer0_sample_kernel(x_ref, past_ref, win_ref, poolw_ref, pscale_ref, glng_ref, glnb_ref, wk_ref, gmb_ref,
                          wout_ref, lng_ref, lnb_ref, y_ref, pstate_ref, gmv_ref, ext_ref,
                          *, nb, ts, pos0, pw, gw, hw):
    x = x_ref[...]
    z = _dot(x.astype(BF16), win_ref[...])
    p = z[:, :pw]
    u = z[:, pw:pw + 4 * hw]
    v = _ln(z[:, pw + 4 * hw:], glng_ref[...], glnb_ref[...])
    gmv_ref[...] = v

    p3 = p.reshape(nb, ts, pw)
    ext_ref[:, 0:POOL_HALO, :] = past_ref[...]
    ext_ref[:, POOL_HALO:POOL_HALO + ts, :] = p3
    cnt = (pos0 + lax.broadcasted_iota(jnp.int32, (1, ts, 1), 1) + 1).astype(F32)
    a = _pool_groups(lambda j, lanes: ext_ref[:, POOL_HALO - j:POOL_HALO - j + ts, lanes], p3, cnt, poolw_ref, gw)
    a = a * pscale_ref[...]
    pstate_ref[...] = ext_ref[:, ts:ts + POOL_HALO, :]

    vb = v.astype(BF16)
    mix = jnp.concatenate([_dot(wk_ref[h], vb[:, h * hw:(h + 1) * hw]) for h in range(4)], axis=-1) + gmb_ref[...]
    gb = u * mix
    y = _dot(a.astype(BF16), wout_ref[0:pw, :]) + _dot(gb.astype(BF16), wout_ref[pw:, :])
    y_ref[...] = _ln(DN_ALPHA * x + y, lng_ref[...], lnb_ref[...])


def _mixer0_sample(x, past, w, lng, lnb, nb, ts, pos0):
    m, d = x.shape
    pw = w["pool_scale"].shape[1]
    gw = pw // len(POOL_WINDOWS)
    hw = w["gm_ws"].shape[1]
    kern = functools.partial(_mixer0_sample_kernel, nb=nb, ts=ts, pos0=pos0, pw=pw, gw=gw, hw=hw)
    past16 = jnp.pad(past, ((0, 0), (POOL_HALO - POOL_STATE, 0), (0, 0)))
    args = (x, past16, w["w_in_even"], w["pool_w"], w["pool_scale"], w["gm_ln_g"], w["gm_ln_b"],
            w["gm_ws_kron"], w["gm_b_rows"], w["w_out_even"], lng, lnb)
    y, pstate, gmv = pl.pallas_call(
        kern,
        grid=(1,),
        in_specs=[_resident(a.shape) for a in args],
        out_specs=[_whole((m, d)), _whole((nb, POOL_HALO, pw)), _whole((m, 4 * hw))],
        out_shape=[jax.ShapeDtypeStruct((m, d), F32), jax.ShapeDtypeStruct((nb, POOL_HALO, pw), F32),
                   jax.ShapeDtypeStruct((m, 4 * hw), F32)],
        scratch_shapes=[pltpu.VMEM((nb, POOL_HALO + ts, pw), F32)],
        compiler_params=_cparams(1),
        name="mixer0_sample",
    )(*args)
    return y, pstate[:, POOL_HALO - POOL_STATE:], gmv


def _nsa_proj_kernel(x_ref, wq_ref, wkv_ref, wg_ref, q_ref, kc_ref, ks_ref, kw_ref, g_ref, *, kvw):
    xb = x_ref[...].astype(BF16)
    q_ref[...] = (_dot(xb, wq_ref[...]) * (NSA_HD ** -0.5)).astype(BF16)
    kv = _dot(xb, wkv_ref[...])
    kc_ref[...] = kv[:, 0:kvw]
    ks_ref[...] = kv[:, kvw:2 * kvw]
    kw_ref[...] = kv[:, 2 * kvw:3 * kvw]
    g_ref[...] = jax.nn.sigmoid(_dot(xb, wg_ref[...]))


def _nsa_proj(x, wq, wkv, wg, tm):
    m, d = x.shape
    qw = wq.shape[1]
    kvw = wkv.shape[1] // 3
    row = lambda n: pl.BlockSpec((tm, n), lambda i: (i, 0))
    return pl.pallas_call(
        functools.partial(_nsa_proj_kernel, kvw=kvw),
        grid=(m // tm,),
        in_specs=[row(d), _resident(wq.shape), _resident(wkv.shape), _resident(wg.shape)],
        out_specs=[row(qw), row(kvw), row(kvw), row(kvw), row(LANE)],
        out_shape=[jax.ShapeDtypeStruct((m, qw), BF16)] + [jax.ShapeDtypeStruct((m, kvw), F32)] * 3
        + [jax.ShapeDtypeStruct((m, LANE), F32)],
        compiler_params=_cparams(1),
        name="nsa_proj",
    )(x, wq, wkv, wg)


def _nsa_proj_prompt_kernel(x_ref, wqt_ref, wgt_ref, wkvt_ref, wkp_ref,
                            qt_ref, gt_ref, ct_ref, st_ref, wt_ref, kp_ref, vt_ref, *, kvw):
    xb = x_ref[0].astype(BF16)
    qt_ref[0] = (_dot_nt(wqt_ref[...], xb) * (NSA_HD ** -0.5)).astype(BF16)
    gt_ref[0] = jax.nn.sigmoid(_dot_nt(wgt_ref[...], xb))
    kvt = _dot_nt(wkvt_ref[...], xb)
    ct_ref[0] = kvt[0:kvw]
    st_ref[0] = kvt[kvw:2 * kvw]
    wt_ref[0] = kvt[2 * kvw:3 * kvw]
    half = kvw // 2
    vt_ref[0, 0] = jnp.concatenate([kvt[kvw + half:2 * kvw], kvt[2 * kvw + half:3 * kvw]], axis=0).astype(BF16)
    kp = _dot(xb, wkp_ref[...])
    tm = kp.shape[0]
    sub = lax.broadcasted_iota(jnp.int32, kp.shape, 1) % LANE
    grp = lax.broadcasted_iota(jnp.int32, kp.shape, 1) // LANE
    kabs = pl.program_id(1) * tm + lax.broadcasted_iota(jnp.int32, kp.shape, 0)
    blk = kabs // SEL_BLOCK
    pat = jnp.where((sub >= KEY_POS_LANE) & (sub < KEY_POS_LANE + 3), blk,
                    jnp.where((sub >= KEY_POS_LANE + 3) & (sub < KEY_POS_LANE + 6), kabs % SEL_BLOCK, 0))
    pat = jnp.where((grp < NSA_KV) & (sub - KEY_SEL_LANE == blk), 1, pat)
    kp_ref[0] = (kp + pat.astype(F32)).astype(BF16)


def _nsa_proj_prompt(x, w, tm):
    b, t, d = x.shape
    qw = w["wq_t"].shape[0]
    kvw = w["wkv_t"].shape[0] // 3
    kpw = w["wk_pad"].shape[1]
    feat = lambda n: pl.BlockSpec((1, n, tm), lambda i, j: (i, 0, j))
    tok = lambda n: pl.BlockSpec((1, tm, n), lambda i, j: (i, j, 0))
    return pl.pallas_call(
        functools.partial(_nsa_proj_prompt_kernel, kvw=kvw),
        grid=(b, t // tm),
        in_specs=[tok(d), _resident(w["wq_t"].shape), _resident(w["wg_t"].shape), _resident(w["wkv_t"].shape),
                  _resident(w["wk_pad"].shape)],
        out_specs=[feat(qw), feat(LANE), feat(kvw), feat(kvw), feat(kvw), tok(kpw),
                   pl.BlockSpec((1, 1, kvw, tm), lambda i, j: (i, j, 0, 0))],
        out_shape=[jax.ShapeDtypeStruct((b, qw, t), BF16), jax.ShapeDtypeStruct((b, LANE, t), F32)]
        + [jax.ShapeDtypeStruct((b, kvw, t), F32)] * 3
        + [jax.ShapeDtypeStruct((b, t, kpw), BF16), jax.ShapeDtypeStruct((b, t // tm, kvw, tm), BF16)],
        compiler_params=_cparams(2),
        name="nsa_proj_prompt",
    )(x, w["wq_t"], w["wg_t"], w["wkv_t"], w["wk_pad"])


def _put_chunk(xs_ref, kv, i, val):
    for c in range(val.shape[1] // LANE):
        xs_ref[kv, c, i * CHUNK_PITCH:i * CHUNK_PITCH + CMP_STRIDE, :] = val[:, c * LANE:(c + 1) * LANE]


def _put_slab(xs_ref, kv, first_chunk, slab_t):
    rows = slab_t.astype(BF16).T.astype(F32)
    for i in range(PAGE // CMP_STRIDE):
        _put_chunk(xs_ref, kv, first_chunk + i, rows[i * CMP_STRIDE:(i + 1) * CMP_STRIDE])


def _pad_chunks(xs_ref, kv, first, last):
    for i in range(first, last):
        _put_chunk(xs_ref, kv, i, jnp.zeros((CMP_STRIDE, xs_ref.shape[1] * LANE), F32))


def _compress_hidden(xs_ref, kv, n, w1_ref, bias_ref):
    nc = n + 8
    low_lanes = lax.broadcasted_iota(jnp.int32, (nc, LANE), 1) < NSA_HD
    heads = []
    for plane in range(xs_ref.shape[1]):
        even, odd = [], []
        for j in range(CMP_STRIDE // 2):
            a = xs_ref[kv, plane, pl.ds(2 * j, nc, stride=CHUNK_PITCH), :]
            b = xs_ref[kv, plane, pl.ds(2 * j + 1, nc, stride=CHUNK_PITCH), :]
            even.append(jnp.where(low_lanes, a, pltpu.roll(b, NSA_HD, 1)))
            odd.append(jnp.where(low_lanes, pltpu.roll(a, NSA_HD, 1), b))
        heads += [jnp.concatenate(v, axis=-1).astype(BF16) for v in (even, odd)]
    r = _dot(jnp.concatenate(heads, axis=0), w1_ref[kv])
    hid = jnp.concatenate(
        [r[g * nc:g * nc + n, :CMP_HIDDEN] + r[g * nc + 1:g * nc + n + 1, CMP_HIDDEN:] for g in range(NSA_KV)],
        axis=-1) + bias_ref[kv]
    return jax.nn.gelu(hid, approximate=True).astype(BF16)


def _compress_prompt_kernel(ct_ref, w1_ref, bias_ref, w2k_ref, w2v_ref, kc_ref, vct_ref, xs_ref, *, n):
    half = NSA_KV * NSA_HD
    cpp = PAGE // CMP_STRIDE
    for kv in range(2):
        for k in range(n // cpp):
            _put_slab(xs_ref, kv, k * cpp, ct_ref[0, kv * half:(kv + 1) * half, k * PAGE:(k + 1) * PAGE])
        _pad_chunks(xs_ref, kv, n, n + 8)
    valid = lax.broadcasted_iota(jnp.int32, (n, 1), 0) < n - 1
    kc = _dot(_compress_hidden(xs_ref, 0, n, w1_ref, bias_ref), w2k_ref[...])
    kc_ref[0] = jnp.where(valid, kc, 0.0).astype(BF16)
    vc = _dot(_compress_hidden(xs_ref, 1, n, w1_ref, bias_ref), w2v_ref[...])
    vct_ref[0] = jnp.where(valid, vc, 0.0).T.astype(BF16)


def _compress_prompt(kv_t, w):
    b, roww, t = kv_t.shape
    n = t // CMP_STRIDE
    half = roww // 2
    return pl.pallas_call(
        functools.partial(_compress_prompt_kernel, n=n),
        grid=(b,),
        in_specs=[pl.BlockSpec((1, roww, t), lambda i: (i, 0, 0)),
                  _resident(w["cmp_w1_cat"].shape), _resident(w["cmp_bias_rows"].shape),
                  _resident(w["cmp_w2k_pad"].shape), _resident(w["cmp_w2_bd"].shape[1:])],
        out_specs=[pl.BlockSpec((1, n, NSA_KV * LANE), lambda i: (i, 0, 0)),
                   pl.BlockSpec((1, half, n), lambda i: (i, 0, 0))],
        out_shape=[jax.ShapeDtypeStruct((b, n, NSA_KV * LANE), BF16), jax.ShapeDtypeStruct((b, half, n), BF16)],
        scratch_shapes=[pltpu.VMEM((2, half // LANE, (n + 8) * CHUNK_PITCH, LANE), F32)],
        compiler_params=_cparams(1),
        name="compress_prompt",
    )(kv_t, w["cmp_w1_cat"], w["cmp_bias_rows"], w["cmp_w2k_pad"], w["cmp_w2_bd"][1])


def _select_topk_cols(score, n_sel):
    nblk = score.shape[0]
    jj = lax.broadcasted_iota(jnp.int32, score.shape, 0)
    rank = jnp.zeros(score.shape, F32)
    for jp in range(nblk):
        row = score[jp:jp + 1, :]
        rank = rank + jnp.where(row > score, 1.0, jnp.where((row == score) & (jj > jp), 1.0, 0.0))
    return rank < float(n_sel)


def _nsa_prompt_kernel(slopes_ref, qt_ref, gt_ref, srow_ref, kc_ref, vct_ref, ks_ref, vst_ref, kw_ref, vwt_ref,
                       smat_ref, o_ref, m_ref, l_ref, acc_ref, osum_ref, *, tq, tk, nblk, nsb):
    g = pl.program_id(1)
    t = pl.program_id(2)
    nh = NSA_HPG
    hd = NSA_HD
    slopes = [slopes_ref[g * nh + h] for h in range(nh)]
    gate = lambda br, h: gt_ref[0, pl.ds(br * NSA_HEADS + g * nh + h, 1), :]
    qt = qt_ref[0]
    tcol = t * tq + lax.broadcasted_iota(jnp.int32, (1, tq), 1)

    def stack_queries(extra_rows):
        return jnp.concatenate(
            [jnp.concatenate([qt[h * hd:(h + 1) * hd], extra_rows(h)], axis=0) for h in range(nh)], axis=1)

    sc = _dot(kc_ref[0], stack_queries(lambda h: jnp.zeros((LANE - hd, tq), BF16)))
    cend = lax.broadcasted_iota(jnp.int32, (nblk, 1), 0) * CMP_STRIDE + (CMP_LEN - 1)
    dist_i = tcol - cend
    ok = dist_i >= 0
    dist = dist_i.astype(F32)
    pc = None
    for h in range(nh):
        p = _masked_softmax(sc[:, h * tq:(h + 1) * tq] - slopes[h] * dist, ok, 0)
        osum_ref[h * hd:(h + 1) * hd, :] = _dot(vct_ref[0], p.astype(BF16)) * gate(0, h)
        pc = p if pc is None else pc + p
    imp = jnp.dot(smat_ref[...], pc, precision=lax.Precision.HIGHEST, preferred_element_type=F32)
    jt = tcol // SEL_BLOCK
    jblk = lax.broadcasted_iota(jnp.int32, (nsb, tq), 0)
    forced = (jblk == 0) | (jblk == jt) | (jblk == jt - 1)
    score = jnp.where(forced, jnp.inf, jnp.where(jblk <= jt, imp, -jnp.inf))
    unsel = jnp.where(_select_topk_cols(score, min(N_SEL, nsb)), 0.0, -SEL_BIG).astype(BF16)
    pad_rows = jnp.zeros((LANE - KEY_SEL_LANE - nsb, tq), BF16)
    q4 = stack_queries(lambda h: jnp.concatenate(
        [jnp.broadcast_to(srow_ref[h], (KEY_SEL_LANE - hd, tq)).astype(BF16), unsel, pad_rows], axis=0))

    def reset():
        m_ref[...] = jnp.full(m_ref.shape, MASK_NEG, F32)
        l_ref[...] = jnp.zeros(l_ref.shape, F32)
        acc_ref[...] = jnp.zeros(acc_ref.shape, F32)

    def tile_local(k_ref, vt_ref, kt, mask):
        off = pl.multiple_of(kt * tk, tk)
        s = _dot(k_ref[0, pl.ds(off, tk), :], q4)
        if mask is not None:
            d_i = (t * tq - kt * tk + lax.broadcasted_iota(jnp.int32, (1, tq), 1)
                   - lax.broadcasted_iota(jnp.int32, (tk, 1), 0))
            okm = d_i >= 0 if mask == "causal" else d_i < WINDOW
            s = jnp.where(jnp.concatenate([okm] * nh, axis=1), s, MASK_NEG)
        m_t = jnp.max(s, 0, keepdims=True)
        eb = jnp.exp(s - m_t).astype(BF16)
        vt = jnp.concatenate([vt_ref[0, kt], jnp.ones((16, tk), BF16)], axis=0)
        pv = [_dot(vt, eb[:, h * tq:(h + 1) * tq]) for h in range(nh)]
        l_t = jnp.concatenate([r[hd:hd + 1] for r in pv], axis=1)
        return m_t, l_t, [r[:hd] for r in pv]

    def merge(piece):
        m_t, l_t, pv = piece
        m_old = m_ref[...]
        m_new = jnp.maximum(m_old, m_t)
        a_old = jnp.exp(m_old - m_new)
        a_t = jnp.where(m_t > MASK_NEG, jnp.exp(m_t - m_new), 0.0)
        l_ref[...] = a_old * l_ref[...] + a_t * l_t
        m_ref[...] = m_new
        for h in range(nh):
            rows = slice(h * hd, (h + 1) * hd)
            cols = slice(h * tq, (h + 1) * tq)
            acc_ref[rows, :] = a_old[:, cols] * acc_ref[rows, :] + a_t[:, cols] * pv[h]

    def tiles(k_ref, vt_ref, kt0, masks):
        pieces = [tile_local(k_ref, vt_ref, kt0 + i, mk) for i, mk in enumerate(masks)]
        for piece in pieces:
            merge(piece)

    def finish(br):
        for h in range(nh):
            rows = slice(h * hd, (h + 1) * hd)
            o = acc_ref[rows, :] / jnp.maximum(l_ref[:, h * tq:(h + 1) * tq], 1e-30)
            osum_ref[rows, :] = osum_ref[rows, :] + o * gate(br, h)

    nd = tq // tk
    kd = t * nd
    reset()
    tiles(ks_ref, vst_ref, kd, ["causal"] * nd)

    def slc_body(i, c):
        tiles(ks_ref, vst_ref, i * nd, [None] * nd)
        return c

    lax.fori_loop(0, t, slc_body, 0)
    finish(1)
    reset()
    tiles(kw_ref, vwt_ref, kd, ["causal"] * nd)
    nwin = WINDOW // tk

    @pl.when(kd >= nwin)
    def _():
        tiles(kw_ref, vwt_ref, kd - nwin, ["window"] * nwin)

    finish(2)
    o_ref[0] = osum_ref[...].T


def _nsa_prompt(slopes, slope_rows, qt, gt, kc, vct, kp, vt, smat, tq):
    b, qw, t = qt.shape
    nblk = kc.shape[1]
    nsb = smat.shape[0]
    tk = vt.shape[-1]
    ntile = vt.shape[1]
    gw = NSA_HPG * NSA_HD
    assert tq % tk == 0 and WINDOW % tk == 0 and tq >= WINDOW
    assert KEY_SEL_LANE + nsb <= LANE and nsb <= 256
    kern = functools.partial(_nsa_prompt_kernel, tq=tq, tk=tk, nblk=nblk, nsb=nsb)
    keys = lambda br: pl.BlockSpec((1, t, LANE), lambda i, g, j: (i, 0, br * NSA_KV + g))
    vals = lambda br: pl.BlockSpec((1, ntile, NSA_HD, tk), lambda i, g, j: (i, 0, br * NSA_KV + g, 0))
    return pl.pallas_call(
        kern,
        grid=(b, NSA_KV, t // tq),
        in_specs=[pl.BlockSpec(memory_space=pltpu.SMEM),
                  pl.BlockSpec((1, gw, tq), lambda i, g, j: (i, g, j)),
                  pl.BlockSpec((1, LANE, tq), lambda i, g, j: (i, 0, j)),
                  pl.BlockSpec((NSA_HPG,) + slope_rows.shape[1:], lambda i, g, j: (g, 0, 0)),
                  pl.BlockSpec((1, nblk, LANE), lambda i, g, j: (i, 0, g)),
                  pl.BlockSpec((1, NSA_HD, nblk), lambda i, g, j: (i, g, 0)),
                  keys(0), vals(0), keys(1), vals(1),
                  pl.BlockSpec(smat.shape, lambda i, g, j: (0, 0))],
        out_specs=pl.BlockSpec((1, tq, gw), lambda i, g, j: (i, j, g)),
        out_shape=jax.ShapeDtypeStruct((b, t, qw), F32),
        scratch_shapes=[pltpu.VMEM((1, NSA_HPG * tq), F32), pltpu.VMEM((1, NSA_HPG * tq), F32),
                        pltpu.VMEM((gw, tq), F32), pltpu.VMEM((gw, tq), F32)],
        compiler_params=_cparams(3),
        name="nsa_prompt",
    )(slopes, qt, gt, slope_rows, kc, vct, kp, vt, kp, vt, smat)


def _diag_heads(o, rows_per_g):
    return jnp.concatenate(
        [o[g * rows_per_g:(g + 1) * rows_per_g, g * NSA_HD:(g + 1) * NSA_HD] for g in range(NSA_KV)], axis=0)


def _select_topk_rows(score, n_sel):
    jj = lax.broadcasted_iota(jnp.int32, score.shape, 1)
    big = jnp.int32(score.shape[1])
    taken = jnp.zeros(score.shape, jnp.bool_)
    for _ in range(n_sel):
        cur = jnp.where(taken, -jnp.inf, score)
        m = jnp.max(cur, -1, keepdims=True)
        cand = jnp.where(jnp.logical_not(taken) & (cur >= m), jj, big)
        pick = jnp.min(cand, -1, keepdims=True)
        taken = taken | (jj == pick)
    return taken


def _page_rows(page_ref, kv):
    return page_ref[0, kv].reshape(NSA_KV * NSA_HD, PAGE)


def _cmp_sample_kernel(pt_ref, *refs, npg, nsteps, ts, pos0, nblk, nsb, nsbp):
    pages = refs[:npg]
    (nxt_ref, new_ref, w1_ref, bias_ref, w2_ref, q_ref, slope_ref, gate_ref, smat_ref,
     o_ref, sel_ref, xs_ref, kc_ref, vc_ref) = refs[npg:]
    s = pl.program_id(1)
    half = NSA_KV * NSA_HD
    n = npg * PAGE // CMP_STRIDE
    last = s == nsteps - 1
    cpp = PAGE // CMP_STRIDE
    for kv in range(2):
        for k in range(npg):
            _put_slab(xs_ref, kv, k * cpp, _page_rows(pages[k], kv))
        nxt = _page_rows(nxt_ref, kv).T[0:CMP_STRIDE]
        new = new_ref[0][:, kv * half:(kv + 1) * half]
        _put_chunk(xs_ref, kv, n, jnp.where(last, new, nxt))
        _pad_chunks(xs_ref, kv, n + 1, n + 8)

    off = pl.multiple_of(s * n, n)
    for kv, c_ref in ((0, kc_ref), (1, vc_ref)):
        c = _dot(_compress_hidden(xs_ref, kv, n, w1_ref, bias_ref), w2_ref[kv])
        c_ref[pl.ds(off, n), :] = c.astype(BF16)

    @pl.when(last)
    def _():
        nrow = NSA_HEADS * ts
        trow = pos0 + lax.broadcasted_iota(jnp.int32, (nrow, 1), 0) % ts
        cend = lax.broadcasted_iota(jnp.int32, (1, nblk), 1) * CMP_STRIDE + (CMP_LEN - 1)
        dist_i = trow - cend
        sc = _dot_nt(q_ref[0], kc_ref[...]) - slope_ref[...] * dist_i.astype(F32)
        p = _masked_softmax(sc, dist_i >= 0, -1)
        o = _diag_heads(_dot(p.astype(BF16), vc_ref[...]), NSA_HPG * ts)
        o_ref[0] = o * gate_ref[0, 0]
        p4 = p.reshape(NSA_KV, NSA_HPG, ts, nblk)
        pc = p4[:, 0]
        for h in range(1, NSA_HPG):
            pc = pc + p4[:, h]
        imp = jnp.dot(pc.reshape(NSA_KV * ts, nblk), smat_ref[...], precision=lax.Precision.HIGHEST,
                      preferred_element_type=F32)
        tsel = pos0 + lax.broadcasted_iota(jnp.int32, (NSA_KV * ts, 1), 0) % ts
        jt = tsel // SEL_BLOCK
        jblk = lax.broadcasted_iota(jnp.int32, (NSA_KV * ts, nsbp), 1)
        forced = (jblk == 0) | (jblk == jt) | (jblk == jt - 1)
        score = jnp.where(forced, jnp.inf, jnp.where(jblk <= jt, imp, -jnp.inf))
        score = jnp.where(jblk < nsb, score, -jnp.inf)
        sel_ref[0] = jnp.where(_select_topk_rows(score, min(N_SEL, nsb)), 0.0, -1.0)


def _cmp_sample(page_table, cache_t, new_chunk, qbd, slope_col, gates, smat, w, npg, ts, pos0):
    nb, npages = page_table.shape
    half = NSA_KV * NSA_HD
    nsteps = npages // npg
    nblk = npages * PAGE // CMP_STRIDE
    nsbp = smat.shape[1]
    nsb = -(-(npages * PAGE + ts) // SEL_BLOCK)
    nrow = NSA_HEADS * ts
    kern = functools.partial(_cmp_sample_kernel, npg=npg, nsteps=nsteps, ts=ts, pos0=pos0, nblk=nblk, nsb=nsb,
                             nsbp=nsbp)
    pshape = (1,) + cache_t.shape[1:]
    page_spec = lambda k: pl.BlockSpec(pshape, lambda i, s, pt: (pt[i, s * npg + k], 0, 0, 0, 0))
    nxt_spec = pl.BlockSpec(pshape, lambda i, s, pt: (pt[i, jnp.minimum((s + 1) * npg, npages - 1)], 0, 0, 0, 0))
    per_b = lambda shape: pl.BlockSpec((1,) + shape, lambda i, s, pt: (i,) + (0,) * len(shape))
    fixed = lambda a: pl.BlockSpec(a.shape, lambda i, s, pt: (0,) * a.ndim, pipeline_mode=pl.Buffered(1))
    grid_spec = pltpu.PrefetchScalarGridSpec(
        num_scalar_prefetch=1,
        grid=(nb, nsteps),
        in_specs=[page_spec(k) for k in range(npg)] + [
            nxt_spec, per_b((CMP_STRIDE, 2 * half)),
            fixed(w["cmp_w1_cat"]), fixed(w["cmp_bias_rows"]), fixed(w["cmp_w2_bd"]),
            per_b((nrow, half)), fixed(slope_col),
            pl.BlockSpec((1, 1, nrow, 1), lambda i, s, pt: (0, i, 0, 0)), fixed(smat)],
        out_specs=[per_b((nrow, NSA_HD)), per_b((NSA_KV * ts, nsbp))],
        scratch_shapes=[pltpu.VMEM((2, half // LANE, (npg * PAGE // CMP_STRIDE + 8) * CHUNK_PITCH, LANE), F32),
                        pltpu.VMEM((nblk, half), BF16),
                        pltpu.VMEM((nblk, half), BF16)],
    )
    return pl.pallas_call(
        kern,
        grid_spec=grid_spec,
        out_shape=[jax.ShapeDtypeStruct((nb, nrow, NSA_HD), F32), jax.ShapeDtypeStruct((nb, NSA_KV * ts, nsbp), F32)],
        compiler_params=_cparams(2, VMEM_BIG),
        name="cmp_select_sample",
    )(page_table, *([cache_t] * npg), cache_t, new_chunk, w["cmp_w1_cat"], w["cmp_bias_rows"], w["cmp_w2_bd"],
      qbd, slope_col, gates, smat)


def _flash_rows(s, ok, pv, m_ref, l_ref, acc_ref):
    s = jnp.where(ok, s, MASK_NEG)
    m_old = m_ref[...]
    m_new = jnp.maximum(m_old, jnp.max(s, -1, keepdims=True))
    alpha = jnp.exp(m_old - m_new)
    e = jnp.where(ok, jnp.exp(s - m_new), 0.0)
    l_ref[...] = alpha * l_ref[...] + jnp.sum(e, -1, keepdims=True)
    acc_ref[...] = alpha * acc_ref[...] + pv(e.astype(BF16))
    m_ref[...] = m_new


def _slc_sample_kernel(pt_ref, *refs, npg, nsteps, ts, pos0):
    pages = refs[:npg]
    (new_ref, q_ref, slope_ref, gate_ref, auxq_ref, auxk_ref, selt_ref, prev_ref, o_ref, m_ref, l_ref,
     acc_ref) = refs[npg:]
    s = pl.program_id(1)
    half = NSA_KV * NSA_HD
    nrow = NSA_HEADS * ts
    trow = pos0 + lax.broadcasted_iota(jnp.int32, (nrow, 1), 0) % ts

    @pl.when(s == 0)
    def _():
        m_ref[...] = jnp.full(m_ref.shape, MASK_NEG, F32)
        l_ref[...] = jnp.zeros(l_ref.shape, F32)
        acc_ref[...] = jnp.zeros(acc_ref.shape, F32)

    kt = jnp.concatenate([_page_rows(pages[k], 0) for k in range(npg)], axis=1).astype(BF16)
    vt = jnp.concatenate([_page_rows(pages[k], 1) for k in range(npg)], axis=1).astype(BF16)
    nk = npg * PAGE
    s_all = _dot(q_ref[0], kt) + _dot(auxq_ref[0, 0], auxk_ref[...])
    m_old = m_ref[...]
    m_new = jnp.maximum(m_old, jnp.max(s_all, -1, keepdims=True))
    alpha = jnp.exp(m_old - m_new)
    e = jnp.exp(s_all - m_new)
    l_ref[...] = alpha * l_ref[...] + jnp.sum(e, -1, keepdims=True)
    acc_ref[...] = alpha * acc_ref[...] + _dot_nt(e.astype(BF16), vt)
    m_ref[...] = m_new

    @pl.when(s == nsteps - 1)
    def _():
        new = new_ref[0]
        kpos_n = nsteps * nk + lax.broadcasted_iota(jnp.int32, (1, PAGE), 1)
        dist_n = trow - kpos_n
        ok_n = (dist_n >= 0) & (selt_ref[0] > -0.5) & (kpos_n < nsteps * nk + SEL_BLOCK)
        sc_n = (_dot_nt(q_ref[0], new[:, :half].astype(BF16))
                - slope_ref[...] * (dist_n - (trow - pos0)).astype(F32))
        _flash_rows(sc_n, ok_n, lambda e: _dot(e, new[:, half:].astype(BF16)), m_ref, l_ref, acc_ref)
        o = _diag_heads(acc_ref[...] / jnp.maximum(l_ref[...], 1e-30), NSA_HPG * ts)
        o_ref[0] = prev_ref[0] + o * gate_ref[0, 0]


def _slc_sample(page_table, cache_t, new_page, qbd, slope_col, gates, aux_q, aux_k, sel_tail, prev, npg, ts, pos0):
    nb, npages = page_table.shape
    half = NSA_KV * NSA_HD
    nsteps = npages // npg
    nrow = NSA_HEADS * ts
    naux = aux_k.shape[0]
    kern = functools.partial(_slc_sample_kernel, npg=npg, nsteps=nsteps, ts=ts, pos0=pos0)
    pshape = (1,) + cache_t.shape[1:]
    page_spec = lambda k: pl.BlockSpec(pshape, lambda i, s, pt: (pt[i, s * npg + k], 0, 0, 0, 0))
    per_b = lambda shape: pl.BlockSpec((1,) + shape, lambda i, s, pt: (i,) + (0,) * len(shape))
    grid_spec = pltpu.PrefetchScalarGridSpec(
        num_scalar_prefetch=1,
        grid=(nb, nsteps),
        in_specs=[page_spec(k) for k in range(npg)] + [
            per_b((PAGE, 2 * half)), per_b((nrow, half)),
            pl.BlockSpec((nrow, 1), lambda i, s, pt: (0, 0)),
            pl.BlockSpec((1, 1, nrow, 1), lambda i, s, pt: (1, i, 0, 0)),
            pl.BlockSpec((1, 1, nrow, naux), lambda i, s, pt: (i, s, 0, 0)),
            pl.BlockSpec(aux_k.shape, lambda i, s, pt: (0, 0), pipeline_mode=pl.Buffered(1)),
            per_b((nrow, 1)), per_b((nrow, NSA_HD))],
        out_specs=per_b((nrow, NSA_HD)),
        scratch_shapes=[pltpu.VMEM((nrow, 1), F32), pltpu.VMEM((nrow, 1), F32), pltpu.VMEM((nrow, half), F32)],
    )
    return pl.pallas_call(
        kern,
        grid_spec=grid_spec,
        out_shape=jax.ShapeDtypeStruct((nb, nrow, NSA_HD), F32),
        compiler_params=_cparams(2),
        name="slc_sample",
    )(page_table, *([cache_t] * npg), new_page, qbd, slope_col, gates, aux_q, aux_k, sel_tail, prev)


def _win_sample_kernel(past_ref, new_ref, q_ref, slope_ref, gate_ref, prev_ref, o_ref, *, ts, pos0, wb):
    half = NSA_KV * NSA_HD
    nrow = NSA_HEADS * ts
    trow = pos0 + lax.broadcasted_iota(jnp.int32, (nrow, 1), 0) % ts
    q = q_ref[0]
    new = new_ref[0]
    kt = past_ref[0, 0].reshape(half, wb).astype(BF16)
    vt = past_ref[0, 1].reshape(half, wb).astype(BF16)
    dist_p = trow - (pos0 - wb + lax.broadcasted_iota(jnp.int32, (1, wb), 1))
    dist_n = trow - (pos0 + lax.broadcasted_iota(jnp.int32, (1, ts), 1))
    ok_p = (dist_p >= 0) & (dist_p < WINDOW) & (trow - dist_p >= 0)
    ok_n = (dist_n >= 0) & (dist_n < WINDOW)
    s_p = jnp.where(ok_p, _dot(q, kt) - slope_ref[...] * dist_p.astype(F32), MASK_NEG)
    s_n = jnp.where(ok_n, _dot_nt(q, new[:, :half].astype(BF16)) - slope_ref[...] * dist_n.astype(F32), MASK_NEG)
    m = jnp.maximum(jnp.max(s_p, -1, keepdims=True), jnp.max(s_n, -1, keepdims=True))
    e_p = jnp.where(ok_p, jnp.exp(s_p - m), 0.0)
    e_n = jnp.where(ok_n, jnp.exp(s_n - m), 0.0)
    den = jnp.maximum(jnp.sum(e_p, -1, keepdims=True) + jnp.sum(e_n, -1, keepdims=True), 1e-30)
    o = _dot_nt((e_p / den).astype(BF16), vt) + _dot((e_n / den).astype(BF16), new[:, half:].astype(BF16))
    o_ref[0] = prev_ref[0] + _diag_heads(o, NSA_HPG * ts) * gate_ref[0, 0]


def _win_sample(win_t, new_rows, qbd, slope_col, gates, prev, ts, pos0):
    nb = win_t.shape[0]
    wb = win_t.shape[-1]
    nrow = NSA_HEADS * ts
    half = NSA_KV * NSA_HD
    per_b = lambda shape: pl.BlockSpec((1,) + shape, lambda i: (i,) + (0,) * len(shape))
    return pl.pallas_call(
        functools.partial(_win_sample_kernel, ts=ts, pos0=pos0, wb=wb),
        grid=(nb,),
        in_specs=[per_b(win_t.shape[1:]), per_b((ts, 2 * half)), per_b((nrow, half)),
                  pl.BlockSpec((nrow, 1), lambda i: (0, 0)),
                  pl.BlockSpec((1, 1, nrow, 1), lambda i: (2, i, 0, 0)),
                  per_b((nrow, NSA_HD))],
        out_specs=per_b((nrow, NSA_HD)),
        out_shape=jax.ShapeDtypeStruct((nb, nrow, NSA_HD), F32),
        compiler_params=_cparams(1),
        name="win_sample",
    )(win_t, new_rows, qbd, slope_col, gates, prev)


def _importance_matrix(nblk, nsb):
    rs = SEL_BLOCK // CMP_STRIDE
    rc = CMP_LEN // CMP_STRIDE
    s = np.zeros((nsb, nblk), np.float32)
    for j in range(nsb):
        for n in range(rc):
            for r in range(rs):
                m = rs * j + r - n
                if 0 <= m < nblk:
                    s[j, m] += 1.0
    return s


def _pad_heads(a):
    lead = a.shape[:-1]
    a = a.reshape(lead + (NSA_KV, NSA_HD))
    a = jnp.pad(a, ((0, 0),) * len(lead) + ((0, 0), (0, LANE - NSA_HD)))
    return a.reshape(lead + (NSA_KV * LANE,))


def _prep_weights(w_ffn_in, w_ffn_out, ln_g, ln_b, w_in_even, pool_w, pool_scale, gm_ln_g, gm_ln_b, gm_ws, gm_b,
                  w_out_even, w_in_odd, cmp_pe, cmp_w1, cmp_b1, cmp_w2, w_out_odd, ts, nb_s):
    d = w_in_even.shape[0]
    dff = w_ffn_out.shape[2]
    fp = -(-dff // (2 * LANE)) * (2 * LANE)
    w = {}
    w["ffn_g"] = jnp.pad(w_ffn_in[..., :dff], ((0, 0),) * 3 + ((0, fp - dff),)).astype(BF16)
    w["ffn_u"] = jnp.pad(w_ffn_in[..., dff:], ((0, 0),) * 3 + ((0, fp - dff),)).astype(BF16)
    w["ffn_o"] = jnp.pad(w_ffn_out, ((0, 0), (0, 0), (0, fp - dff), (0, 0))).astype(BF16)
    w["ln_g"] = ln_g.reshape(ln_g.shape[0], ln_g.shape[1], 1, d)
    w["ln_b"] = ln_b.reshape(ln_b.shape[0], ln_b.shape[1], 1, d)
    hw = gm_ws.shape[1]
    w["w_in_even"] = w_in_even.astype(BF16)
    w["pool_w"] = pool_w.astype(BF16)
    w["pool_scale"] = pool_scale.reshape(1, -1)
    w["gm_ln_g"] = gm_ln_g.reshape(1, -1)
    w["gm_ln_b"] = gm_ln_b.reshape(1, -1)
    w["gm_ws"] = jnp.tril(gm_ws).astype(BF16)
    w["gm_b_full"] = jnp.repeat(gm_b.T, hw, axis=1)
    cs = min(ts, GMLP_CHUNK)
    ws_s = jnp.tril(gm_ws[:, :cs, :cs])
    w["gm_ws_kron"] = jnp.einsum("ab,hts->hatbs", jnp.eye(nb_s * (ts // cs), dtype=F32), ws_s).reshape(
        gm_ws.shape[0], nb_s * ts, nb_s * ts).astype(BF16)
    w["gm_b_rows"] = jnp.tile(jnp.repeat(gm_b[:, :cs].T, hw, axis=1), (nb_s * (ts // cs), 1))
    w["w_out_even"] = w_out_even.astype(BF16)
    qw = NSA_HEADS * NSA_HD
    half = NSA_KV * NSA_HD
    kvw = 3 * 2 * half
    ng = w_in_odd.shape[1] - qw - kvw
    wq = w_in_odd[:, :qw]
    wkv = w_in_odd[:, qw:qw + kvw]
    wg = jnp.pad(w_in_odd[:, qw + kvw:], ((0, 0), (0, LANE - ng)))
    w["wq"] = wq.astype(BF16)
    w["wkv"] = wkv.astype(BF16)
    w["wg"] = wg.astype(BF16)
    w["wq_t"] = wq.T.astype(BF16)
    w["wkv_t"] = wkv.T.astype(BF16)
    w["wg_t"] = wg.T.astype(BF16)
    w["wk_pad"] = jnp.concatenate([_pad_heads(wkv[:, 2 * half:3 * half]), _pad_heads(wkv[:, 4 * half:5 * half])],
                                  axis=1).astype(BF16)
    w["w_out_odd"] = w_out_odd.astype(BF16)
    rc = CMP_LEN // CMP_STRIDE
    eye = jnp.eye(NSA_KV, dtype=F32)
    w1 = cmp_w1.reshape(2, rc, CMP_STRIDE, NSA_HD, CMP_HIDDEN)
    w["cmp_w1_cat"] = w1.transpose(0, 2, 3, 1, 4).reshape(2, CMP_STRIDE * NSA_HD, rc * CMP_HIDDEN).astype(BF16)
    pe_bias = jnp.einsum("krpd,krpdf->kf", cmp_pe.reshape(2, rc, CMP_STRIDE, NSA_HD), w1,
                         precision=lax.Precision.HIGHEST)
    w["cmp_bias_rows"] = jnp.tile(cmp_b1 + pe_bias, (1, NSA_KV)).reshape(2, 1, -1)
    w2_bd = jnp.einsum("kfd,gh->kgfhd", cmp_w2, eye).reshape(2, NSA_KV * CMP_HIDDEN, half)
    w["cmp_w2_bd"] = w2_bd.astype(BF16)
    w["cmp_w2k_pad"] = _pad_heads(w2_bd[0]).astype(BF16)
    slopes = 2.0 ** (-8.0 * (jnp.arange(NSA_HEADS, dtype=F32) + 1.0) / NSA_HEADS)
    w["slopes"] = slopes
    hi = slopes.astype(BF16).astype(F32)
    mid = (slopes - hi).astype(BF16).astype(F32)
    lo = (slopes - hi - mid).astype(BF16).astype(F32)
    pieces = jnp.stack([hi, mid, lo], axis=1)
    rows = jnp.concatenate([pieces * SEL_BLOCK, pieces,
                            jnp.zeros((NSA_HEADS, KEY_SEL_LANE - KEY_POS_LANE - 6), F32)], axis=1)
    w["slope_rows"] = rows[:, :, None]
    return w


def _rows_from_feature_major(a_t):
    b, _, t = a_t.shape
    return a_t.reshape(b, 2, NSA_KV, NSA_HD, t).transpose(0, 4, 1, 2, 3)


def _trunk_prompt(x3, w):
    b, t, d = x3.shape
    m = b * t
    tm = 256
    tf = 512
    tq = WINDOW
    tk = WINDOW // 2
    x = _ffn(x3.reshape(m, d), w, 0, 0, tf)
    x, pool_state = _mixer0_prompt(x.reshape(b, t, d), w, tm)
    x = _ffn(x.reshape(m, d), w, 1, 0, tf)

    qt, gt, ct, st, wt, kp, vt = _nsa_proj_prompt(x.reshape(b, t, d), w, tk)
    kc, vct = _compress_prompt(ct, w)
    smat = jnp.asarray(_importance_matrix(kc.shape[1], -(-t // SEL_BLOCK)))
    o = _nsa_prompt(w["slopes"], w["slope_rows"], qt, gt, kc, vct, kp, vt, smat, tq)

    x = _outproj_ffn(x, o.reshape(m, -1), w["w_out_odd"], w, 1, tf)
    win = _rows_from_feature_major(wt[:, :, t - min(WINDOW, t):])
    return x.reshape(b, t, d), pool_state, _rows_from_feature_major(ct), _rows_from_feature_major(st), win


def _split3(x):
    hi = x.astype(BF16).astype(F32)
    mid = (x - hi).astype(BF16).astype(F32)
    return hi, mid, (x - hi - mid).astype(BF16).astype(F32)


def _slc_aux_operands(sel_steps, slope_rows, slope_col, npg, ts, pos0):
    nb, nsteps, nrow, bps = sel_steps.shape
    nk = npg * PAGE
    naux = -(-(bps + 9) // 16) * 16
    pieces = jnp.repeat(slope_rows[:, :6, 0], ts, axis=0)
    start = pos0 - jnp.arange(nsteps, dtype=F32) * nk
    offs = jnp.stack(_split3(-slope_col[None, :, 0] * start[:, None]), axis=-1)
    aux_q = jnp.concatenate([
        sel_steps * SEL_BIG,
        jnp.broadcast_to(pieces, (nb, nsteps, nrow, 6)),
        jnp.broadcast_to(offs[None], (nb, nsteps, nrow, 3)),
        jnp.zeros((nb, nsteps, nrow, naux - bps - 9), F32)], axis=-1).astype(BF16)
    j = np.arange(nk)
    aux_k = np.zeros((naux, nk), np.float32)
    aux_k[j // SEL_BLOCK, j] = 1.0
    aux_k[bps:bps + 3] = j // SEL_BLOCK
    aux_k[bps + 3:bps + 6] = j % SEL_BLOCK
    aux_k[bps + 6:bps + 9] = 1.0
    return aux_q, jnp.asarray(aux_k, dtype=BF16)


def _trunk_sample(x3, pool_past, cache_cmp, cache_slc, win_past, page_table, w):
    nb, ts, d = x3.shape
    m = nb * ts
    npages = page_table.shape[1]
    pos0 = npages * PAGE
    npg = 32
    x = _ffn(x3.reshape(m, d), w, 0, 0, m)
    x, pool_state, gm_v = _mixer0_sample(x, pool_past, w, w["ln_g"][0, 1], w["ln_b"][0, 1], nb, ts, pos0)
    x = _ffn(x, w, 0, 1, m)
    x = _ffn(x, w, 1, 0, m)

    q, kvc, kvs, kvw, gates = _nsa_proj(x, w["wq"], w["wkv"], w["wg"], m)
    half = NSA_KV * NSA_HD
    roww = 2 * half
    nrow = NSA_HEADS * ts
    qr = q.reshape(nb, ts, NSA_KV, NSA_HPG, NSA_HD).transpose(0, 2, 3, 1, 4).reshape(nb, NSA_KV, NSA_HPG * ts, NSA_HD)
    qbd = jnp.einsum("bgrd,gk->bgrkd", qr, jnp.eye(NSA_KV, dtype=BF16)).reshape(nb, nrow, half)
    slope_col = jnp.repeat(w["slopes"], ts).reshape(nrow, 1)
    gcols = gates[:, :3 * NSA_HEADS].reshape(nb, ts, 3, NSA_HEADS).transpose(2, 0, 3, 1).reshape(3, nb, nrow, 1)
    feature_major = lambda c: c.transpose(0, 2, 3, 4, 1)

    nblk = npages * PAGE // CMP_STRIDE
    nsb = -(-(pos0 + ts) // SEL_BLOCK)
    nsbp = -(-nsb // LANE) * LANE
    smat = jnp.asarray(np.pad(_importance_matrix(nblk, nsb).T, ((0, 0), (0, nsbp - nsb))))
    new_chunk = jnp.pad(kvc.reshape(nb, ts, roww), ((0, 0), (0, CMP_STRIDE - ts), (0, 0)))
    o, sel = _cmp_sample(page_table, feature_major(cache_cmp), new_chunk, qbd, slope_col, gcols, smat, w, npg, ts,
                         pos0)

    bpp = PAGE // SEL_BLOCK
    nsteps = npages // npg
    sel_rows = jnp.broadcast_to(sel.reshape(nb, NSA_KV, 1, ts, nsbp), (nb, NSA_KV, NSA_HPG, ts, nsbp))
    sel_rows = sel_rows.reshape(nb, nrow, nsbp)
    sel_steps = sel_rows[:, :, :npages * bpp].reshape(nb, nrow, nsteps, npg * bpp).transpose(0, 2, 1, 3)
    sel_tail = sel_rows[:, :, npages * bpp:npages * bpp + 1]
    new_page = jnp.pad(kvs.reshape(nb, ts, roww), ((0, 0), (0, PAGE - ts), (0, 0)))
    aux_q, aux_k = _slc_aux_operands(sel_steps, w["slope_rows"], slope_col, npg, ts, pos0)
    o = _slc_sample(page_table, feature_major(cache_slc), new_page, qbd, slope_col, gcols, aux_q, aux_k, sel_tail, o,
                    npg, ts, pos0)
    win_t = feature_major(win_past)
    wb = win_t.shape[-1]
    new_win = kvw.reshape(nb, ts, roww)
    o = _win_sample(win_t, new_win, qbd, slope_col, gcols, o, ts, pos0)

    o = o.reshape(nb, NSA_KV, NSA_HPG, ts, NSA_HD).transpose(0, 3, 1, 2, 4).reshape(m, NSA_HEADS * NSA_HD)
    x = _outproj_ffn(x, o, w["w_out_odd"], w, 1, m)
    kv5 = lambda a: a.reshape(nb, ts, 2, NSA_KV, NSA_HD)
    new_win_t = feature_major(kv5(kvw))
    win_buf = jnp.concatenate([win_t, new_win_t], axis=-1)[..., -wb:].transpose(0, 4, 1, 2, 3)
    return x.reshape(nb, ts, d), pool_state, gm_v.reshape(nb, ts, -1), kv5(kvc), kv5(kvs), win_buf


def kernel(x_prompt, x_sample, state_l0_pool, cache_l1_cmp_kv, cache_l1_slc_kv, cache_l1_win_kv, page_table,
           w_ffn_in, w_ffn_out, ln_g, ln_b, w_in_even, pool_w, pool_scale, gm_ln_g, gm_ln_b, gm_ws, gm_b,
           w_out_even, w_in_odd, cmp_pe, cmp_w1, cmp_b1, cmp_w2, w_out_odd):
    w = _prep_weights(w_ffn_in, w_ffn_out, ln_g, ln_b, w_in_even, pool_w, pool_scale, gm_ln_g, gm_ln_b, gm_ws,
                      gm_b, w_out_even, w_in_odd, cmp_pe, cmp_w1, cmp_b1, cmp_w2, w_out_odd,
                      x_sample.shape[1], x_sample.shape[0])
    y_p, pool_p, cmp_p, slc_p, win_p = _trunk_prompt(x_prompt, w)
    y_s, pool_s, gmv_s, cmp_s, slc_s, win_s = _trunk_sample(
        x_sample, state_l0_pool, cache_l1_cmp_kv, cache_l1_slc_kv, cache_l1_win_kv, page_table, w)
    return (y_p, y_s, pool_p, pool_s, gmv_s, cmp_p, slc_p, win_p, cmp_s, slc_s, win_s)
```
